```python
import math
import jax, jax.numpy as jnp
from jax import lax
import numpy as np

D_MODEL = 1024
BATCH = 4
SEQ = 8192
DEPTH = 1
DEC_BATCH = 128
DEC_SEQ = 4
PAST_LEN = 8192
PAGE_SIZE = 128

ATT_WINDOWS = (128, 512, 2048)
ATT_DILATIONS = (1, 4, 16)
ATT_N_GROUPS = 3
ATT_HEADS = 8
ATT_HEAD_DIM = 64
ATT_WIDTH = ATT_HEADS * ATT_HEAD_DIM
ATT_SCALE = ATT_HEAD_DIM ** -0.5
ROT_DIM = ATT_HEAD_DIM // 4
ROPE_THETA = 500000.0
ATT_BLOCK = 128

SSM_D_INNER = 2 * D_MODEL
SSM_HEAD_DIM = 64
SSM_HEADS = SSM_D_INNER // SSM_HEAD_DIM
SSM_STATE = 128
SSM_GROUPS = 4
SSM_CONV = 4
SSM_CHUNK = 128
SSM_BC = SSM_GROUPS * SSM_STATE
SSM_CONV_DIM = SSM_D_INNER + 2 * SSM_BC

MOE_GROUPS = 4
MOE_PER_GROUP = 4
MOE_EXPERTS = MOE_GROUPS * MOE_PER_GROUP
MOE_TOPK = 2
MOE_FF = D_MODEL // 2

RMS_EPS = 1e-6
SSM_NORM_EPS = 1e-5

OFF_Z = ATT_N_GROUPS * 3 * ATT_WIDTH
OFF_XBC = OFF_Z + SSM_D_INNER
OFF_DT = OFF_XBC + SSM_CONV_DIM
OFF_GATE = OFF_DT + SSM_HEADS
N_IN = OFF_GATE + 2 * D_MODEL

F32 = jnp.float32

kernel_name = 'hybrid_dilated_ssd_hmoe_step'


def rms_norm(x, g, eps=RMS_EPS):
    xf = x.astype(F32)
    y = xf * lax.rsqrt(jnp.mean(xf * xf, axis=-1, keepdims=True) + eps)
    return (y * g.astype(F32)).astype(x.dtype)


def partial_rope(t, pos):
    half = ROT_DIM // 2
    inv = ROPE_THETA ** (-jnp.arange(half, dtype=F32) / half)
    ang = pos[:, None] * inv[None, :]
    cos = jnp.cos(ang)[:, None, :]
    sin = jnp.sin(ang)[:, None, :]
    tf = t.astype(F32)
    x1 = tf[..., :half]
    x2 = tf[..., half:ROT_DIM]
    out = jnp.concatenate([x1 * cos - x2 * sin, x2 * cos + x1 * sin, tf[..., ROT_DIM:]], axis=-1)
    return out.astype(t.dtype)


def dilated_attn_prompt(q, k, v, window, dilation):
    Bsz, S, H, Dh = q.shape
    L = S // dilation
    nb = -(-L // ATT_BLOCK)
    Lp = nb * ATT_BLOCK
    steps = window // dilation

    def to_blocks(t):
        t = t.astype(F32).reshape(Bsz, L, dilation, H, Dh).transpose(0, 2, 1, 3, 4)
        t = jnp.pad(t, ((0, 0), (0, 0), (0, Lp - L), (0, 0), (0, 0)))
        return t.reshape(Bsz, dilation, nb, ATT_BLOCK, H, Dh)

    def with_prev(t):
        prev = jnp.pad(t[:, :, :-1], ((0, 0), (0, 0), (1, 0), (0, 0), (0, 0), (0, 0)))
        return jnp.concatenate([prev, t], axis=3)

    qb = to_blocks(q)
    kk = with_prev(to_blocks(k))
    vv = with_prev(to_blocks(v))
    s = jnp.einsum('brnqhc,brnkhc->brnhqk', qb, kk) * ATT_SCALE
    qi = jnp.arange(ATT_BLOCK)[:, None] + ATT_BLOCK
    ki = jnp.arange(2 * ATT_BLOCK)[None, :]
    dist = qi - ki
    band = (dist >= 0) & (dist <= steps)
    has_prev = (jnp.arange(nb)[:, None, None] > 0) | (ki[None] >= ATT_BLOCK)
    mask = band[None] & has_prev
    s = jnp.where(mask[None, None, :, None], s, -jnp.inf)
    m = jnp.max(s, axis=-1)
    p = jnp.exp(s - m[..., None])
    l = jnp.sum(p, axis=-1)
    num = jnp.einsum('brnhqk,brnkhc->brnqhc', p, vv)

    def from_blocks(t):
        t = t.reshape((Bsz, dilation, Lp) + t.shape[4:])[:, :, :L]
        t = jnp.swapaxes(t, 1, 2)
        return t.reshape((Bsz, S) + t.shape[3:])

    return from_blocks(num), from_blocks(jnp.swapaxes(m, 3, 4)), from_blocks(jnp.swapaxes(l, 3, 4))


def dilated_attn_sample(q, k, v, kv_buf, window, dilation):
    Bsz, T, H, Dh = q.shape
    WB = kv_buf.shape[1]
    steps = window // dilation
    kv_all = jnp.concatenate([kv_buf, jnp.stack([k, v], axis=2).astype(kv_buf.dtype)], axis=1)
    idx = WB + jnp.arange(T)[:, None] - dilation * jnp.arange(steps + 1)[None, :]
    valid = idx >= 0
    kvg = kv_all[:, jnp.maximum(idx, 0)]
    s = jnp.einsum('bthc,btjhc->bthj', q.astype(F32), kvg[:, :, :, 0].astype(F32)) * ATT_SCALE
    s = jnp.where(valid[None, :, None, :], s, -jnp.inf)
    m = jnp.max(s, axis=-1)
    p = jnp.exp(s - m[..., None])
    l = jnp.sum(p, axis=-1)
    num = jnp.einsum('bthj,btjhc->bthc', p, kvg[:, :, :, 1].astype(F32))
    return num, m, l, kv_all[:, T:]


def merge_by_denominators(nums, ms, ls):
    mstk = jnp.stack(ms)
    w = jnp.exp(mstk - jnp.max(mstk, axis=0))
    num = w[0][..., None] * nums[0]
    for g in range(1, len(nums)):
        num = num + w[g][..., None] * nums[g]
    den = jnp.sum(w * jnp.stack(ls), axis=0)
    return num / den[..., None]


def causal_dwconv(u, buf, w, b):
    K = w.shape[0]
    S = u.shape[1]
    up = jnp.concatenate([buf.astype(u.dtype), u], axis=1)
    y = b + up[:, 0:S] * w[0]
    for i in range(1, K):
        y = y + up[:, i:i + S] * w[i]
    return y, up[:, S:]


def ssd_chunked(x, dt, A, Bm, Cm, h0):
    Bsz, S, H, P = x.shape
    G, N = Bm.shape[2], Bm.shape[3]
    Hg = H // G
    Q = SSM_CHUNK
    c = S // Q
    xdt = (x.astype(F32) * dt[..., None]).reshape(Bsz, c, Q, G, Hg, P)
    a = (dt * A).reshape(Bsz, c, Q, G, Hg)
    acum = jnp.cumsum(a, axis=2)
    Bc = Bm.astype(F32).reshape(Bsz, c, Q, G, N)
    Cc = Cm.astype(F32).reshape(Bsz, c, Q, G, N)
    diff = acum[:, :, :, None] - acum[:, :, None, :]
    causal = (jnp.arange(Q)[:, None] >= jnp.arange(Q)[None, :])[None, None, :, :, None, None]
    decay = jnp.exp(jnp.where(causal, diff, -jnp.inf))
    cb = jnp.einsum('bcign,bcjgn->bcijg', Cc, Bc)
    y_diag = jnp.einsum('bcijgh,bcjghp->bcighp', decay * cb[..., None], xdt)
    decay_s = jnp.exp(acum[:, :, -1:] - acum)
    states = jnp.einsum('bcjgn,bcjghp->bcghpn', Bc, xdt * decay_s[..., None])
    chunk_decay = jnp.exp(acum[:, :, -1])

    def step(h, inp):
        st, dec = inp
        return dec[..., None, None] * h + st, h

    h_last, h_prev = lax.scan(step, h0.astype(F32).reshape(Bsz, G, Hg, P, N),
                              (jnp.moveaxis(states, 1, 0), jnp.moveaxis(chunk_decay, 1, 0)))
    h_prev = jnp.moveaxis(h_prev, 0, 1)
    y_off = jnp.einsum('bcign,bcghpn->bcighp', Cc, h_prev) * jnp.exp(acum)[..., None]
    y = (y_diag + y_off).reshape(Bsz, S, H, P)
    return y, h_last.reshape(Bsz, H, P, N)


def ssd_recurrent(x, dt, A, Bm, Cm, h0):
    Bsz, T, H, P = x.shape
    G, N = Bm.shape[2], Bm.shape[3]
    Hg = H // G
    Ag = A.reshape(G, Hg)
    xt = jnp.moveaxis(x.astype(F32).reshape(Bsz, T, G, Hg, P), 1, 0)
    dtt = jnp.moveaxis(dt.reshape(Bsz, T, G, Hg), 1, 0)
    Bt = jnp.moveaxis(Bm.astype(F32), 1, 0)
    Ct = jnp.moveaxis(Cm.astype(F32), 1, 0)

    def step(h, inp):
        x_t, dt_t, b_t, c_t = inp
        h = jnp.exp(dt_t * Ag)[..., None, None] * h + jnp.einsum('bghp,bgn->bghpn', x_t * dt_t[..., None], b_t)
        return h, jnp.einsum('bghpn,bgn->bghp', h, c_t)

    h, ys = lax.scan(step, h0.astype(F32).reshape(Bsz, G, Hg, P, N), (xt, dtt, Bt, Ct))
    return jnp.moveaxis(ys, 0, 1).reshape(Bsz, T, H, P), h.reshape(Bsz, H, P, N)


def gated_rms_norm(y, z, w):
    Bsz, S, C = y.shape
    yf = (y.astype(F32) * jax.nn.silu(z.astype(F32))).reshape(Bsz, S, SSM_GROUPS, C // SSM_GROUPS)
    yf = yf * lax.rsqrt(jnp.mean(yf * yf, axis=-1, keepdims=True) + SSM_NORM_EPS)
    return yf.reshape(Bsz, S, C) * w.astype(F32)


def hier_moe(h, w_rc, b_rc, w_rf, b_rf, w_eg, w_eu, w_ed):
    Bsz, S, D = h.shape
    t = h.reshape(-1, D)
    lc = (t @ w_rc).astype(F32) + b_rc.astype(F32)
    pc = jax.nn.softmax(lc, axis=-1)
    g_sel = jnp.argmax(lc, axis=-1)
    p_sel = jnp.max(pc, axis=-1, keepdims=True)
    lf = ((t @ w_rf).astype(F32) + b_rf.astype(F32)).reshape(-1, MOE_GROUPS, MOE_PER_GROUP)
    lf_sel = jnp.einsum('ng,nge->ne', jax.nn.one_hot(g_sel, MOE_GROUPS, dtype=F32), lf)
    top_v, top_i = lax.top_k(lf_sel, MOE_TOPK)
    wts = jax.nn.softmax(top_v, axis=-1) * p_sel
    ids = g_sel[:, None] * MOE_PER_GROUP + top_i
    combine = jnp.sum(jax.nn.one_hot(ids, MOE_EXPERTS, dtype=F32) * wts[..., None], axis=1)
    out = jnp.zeros(t.shape, F32)
    for e in range(MOE_EXPERTS):
        he = jax.nn.silu(t @ w_eg[e]) * (t @ w_eu[e])
        out = out + combine[:, e:e + 1] * (he @ w_ed[e]).astype(F32)
    return out.astype(h.dtype).reshape(Bsz, S, D)


def trunk_layer(x, pos, kv_bufs, ssm_h0, conv_buf, lw):
    (norm1, w_in, conv_w, conv_b, dt_bias, a_log, d_skip, ssm_norm, w_att_out, w_ssm_out, w_o,
     norm2, w_rc, b_rc, w_rf, b_rf, w_eg, w_eu, w_ed) = lw
    Bsz, S, _ = x.shape
    sample = kv_bufs is not None
    proj = rms_norm(x, norm1) @ w_in

    nums, ms, ls, new_kv = [], [], [], []
    for g in range(ATT_N_GROUPS):
        base = g * 3 * ATT_WIDTH
        q, k, v = [proj[..., base + i * ATT_WIDTH: base + (i + 1) * ATT_WIDTH].reshape(Bsz, S, ATT_HEADS, ATT_HEAD_DIM)
                   for i in range(3)]
        q = partial_rope(q, pos)
        k = partial_rope(k, pos)
        if sample:
            num, m, l, buf = dilated_attn_sample(q, k, v, kv_bufs[g], ATT_WINDOWS[g], ATT_DILATIONS[g])
        else:
            num, m, l = dilated_attn_prompt(q, k, v, ATT_WINDOWS[g], ATT_DILATIONS[g])
            buf = jnp.stack([k, v], axis=2)[:, -min(ATT_WINDOWS[g], S):]
        nums.append(num)
        ms.append(m)
        ls.append(l)
        new_kv.append(buf)
    attn = merge_by_denominators(nums, ms, ls).reshape(Bsz, S, ATT_WIDTH).astype(x.dtype)

    z = proj[..., OFF_Z:OFF_XBC]
    xbc, new_conv = causal_dwconv(proj[..., OFF_XBC:OFF_DT], conv_buf, conv_w, conv_b)
    xbc = jax.nn.silu(xbc)
    xs = xbc[..., :SSM_D_INNER].reshape(Bsz, S, SSM_HEADS, SSM_HEAD_DIM)
    Bm = xbc[..., SSM_D_INNER:SSM_D_INNER + SSM_BC].reshape(Bsz, S, SSM_GROUPS, SSM_STATE)
    Cm = xbc[..., SSM_D_INNER + SSM_BC:].reshape(Bsz, S, SSM_GROUPS, SSM_STATE)
    dt = jax.nn.softplus(proj[..., OFF_DT:OFF_GATE].astype(F32) + dt_bias.astype(F32))
    A = -jnp.exp(a_log.astype(F32))
    if sample:
        y, h_new = ssd_recurrent(xs, dt, A, Bm, Cm, ssm_h0)
    else:
        y, h_new = ssd_chunked(xs, dt, A, Bm, Cm, ssm_h0)
    y = y + d_skip.astype(F32)[:, None] * xs.astype(F32)
    y_ssm = gated_rms_norm(y.reshape(Bsz, S, SSM_D_INNER), z, ssm_norm).astype(x.dtype)

    gates = jax.nn.sigmoid(proj[..., OFF_GATE:].astype(F32))
    mixed = (gates[..., :D_MODEL] * (attn @ w_att_out).astype(F32)
             + gates[..., D_MODEL:] * (y_ssm @ w_ssm_out).astype(F32))
    x = x + mixed.astype(x.dtype) @ w_o

    x = x + hier_moe(rms_norm(x, norm2), w_rc, b_rc, w_rf, b_rf, w_eg, w_eu, w_ed)
    return x, new_kv, h_new, new_conv


def setup_inputs(seed: int = 0) -> dict:
    key = jax.random.key(seed)
    ks = jax.random.split(key, 32)

    def nrm(i, shape, scale):
        return scale * jax.random.normal(ks[i], shape, F32)

    dt0 = jnp.exp(jax.random.uniform(ks[10], (DEPTH, SSM_HEADS), F32, math.log(1e-3), math.log(1e-1)))
    return {
        'x_prompt': nrm(0, (BATCH, SEQ, D_MODEL), 1.0),
        'x_sample': nrm(1, (DEC_BATCH, DEC_SEQ, D_MODEL), 1.0),
        'cache_kv_w128': nrm(2, (DEPTH, DEC_BATCH, min(ATT_WINDOWS[0], PAST_LEN), 2, ATT_HEADS, ATT_HEAD_DIM), 1.0),
        'cache_kv_w512': nrm(3, (DEPTH, DEC_BATCH, min(ATT_WINDOWS[1], PAST_LEN), 2, ATT_HEADS, ATT_HEAD_DIM), 1.0),
        'cache_kv_w2048': nrm(4, (DEPTH, DEC_BATCH, min(ATT_WINDOWS[2], PAST_LEN), 2, ATT_HEADS, ATT_HEAD_DIM), 1.0),
        'state_ssm': nrm(5, (DEPTH, DEC_BATCH, SSM_HEADS, SSM_HEAD_DIM, SSM_STATE), 0.5),
        'state_conv': nrm(6, (DEPTH, DEC_BATCH, SSM_CONV - 1, SSM_CONV_DIM), 1.0),
        'norm1': 1.0 + nrm(7, (DEPTH, D_MODEL), 0.02),
        'w_in': nrm(8, (DEPTH, D_MODEL, N_IN), D_MODEL ** -0.5),
        'conv_w': nrm(9, (DEPTH, SSM_CONV, SSM_CONV_DIM), SSM_CONV ** -0.5),
        'conv_b': nrm(11, (DEPTH, SSM_CONV_DIM), 0.02),
        'dt_bias': dt0 + jnp.log(-jnp.expm1(-dt0)),
        'a_log': jnp.log(jax.random.uniform(ks[12], (DEPTH, SSM_HEADS), F32, 1.0, 16.0)),
        'd_skip': 1.0 + nrm(13, (DEPTH, SSM_HEADS), 0.02),
        'ssm_norm': 1.0 + nrm(14, (DEPTH, SSM_D_INNER), 0.02),
        'w_att_out': nrm(15, (DEPTH, ATT_WIDTH, D_MODEL), ATT_WIDTH ** -0.5),
        'w_ssm_out': nrm(16, (DEPTH, SSM_D_INNER, D_MODEL), SSM_D_INNER ** -0.5),
        'w_o': nrm(17, (DEPTH, D_MODEL, D_MODEL), D_MODEL ** -0.5),
        'norm2': 1.0 + nrm(18, (DEPTH, D_MODEL), 0.02),
        'w_router_coarse': nrm(19, (DEPTH, D_MODEL, MOE_GROUPS), D_MODEL ** -0.5),
        'b_router_coarse': nrm(20, (DEPTH, MOE_GROUPS), 0.01),
        'w_router_fine': nrm(21, (DEPTH, D_MODEL, MOE_EXPERTS), D_MODEL ** -0.5),
        'b_router_fine': nrm(22, (DEPTH, MOE_EXPERTS), 0.01),
        'w_exp_gate': nrm(23, (DEPTH, MOE_EXPERTS, D_MODEL, MOE_FF), D_MODEL ** -0.5),
        'w_exp_up': nrm(24, (DEPTH, MOE_EXPERTS, D_MODEL, MOE_FF), D_MODEL ** -0.5),
        'w_exp_down': nrm(25, (DEPTH, MOE_EXPERTS, MOE_FF, D_MODEL), MOE_FF ** -0.5),
        'norm_f': 1.0 + nrm(26, (D_MODEL,), 0.02),
    }


def reference(x_prompt, x_sample, cache_kv_w128, cache_kv_w512, cache_kv_w2048, state_ssm, state_conv,
              norm1, w_in, conv_w, conv_b, dt_bias, a_log, d_skip, ssm_norm, w_att_out, w_ssm_out, w_o,
              norm2, w_router_coarse, b_router_coarse, w_router_fine, b_router_fine,
              w_exp_gate, w_exp_up, w_exp_down, norm_f):
    Bp, Sp, _ = x_prompt.shape
    Bs, Ts, _ = x_sample.shape
    pos_p = jnp.arange(Sp, dtype=F32)
    pos_s = jnp.arange(Ts, dtype=F32) + PAST_LEN
    yp, ys = x_prompt, x_sample
    kvp = ([], [], [])
    kvs = ([], [], [])
    ssm_p, ssm_s, conv_p, conv_s = [], [], [], []
    for layer in range(DEPTH):
        lw = (norm1[layer], w_in[layer], conv_w[layer], conv_b[layer], dt_bias[layer], a_log[layer],
              d_skip[layer], ssm_norm[layer], w_att_out[layer], w_ssm_out[layer], w_o[layer],
              norm2[layer], w_router_coarse[layer], b_router_coarse[layer], w_router_fine[layer],
              b_router_fine[layer], w_exp_gate[layer], w_exp_up[layer], w_exp_down[layer])
        h0_p = jnp.zeros((Bp, SSM_HEADS, SSM_HEAD_DIM, SSM_STATE), F32)
        c0_p = jnp.zeros((Bp, SSM_CONV - 1, SSM_CONV_DIM), x_prompt.dtype)
        yp, nkv_p, hp, cp = trunk_layer(yp, pos_p, None, h0_p, c0_p, lw)
        bufs = (cache_kv_w128[layer], cache_kv_w512[layer], cache_kv_w2048[layer])
        ys, nkv_s, hs, cs = trunk_layer(ys, pos_s, bufs, state_ssm[layer], state_conv[layer], lw)
        for g in range(ATT_N_GROUPS):
            kvp[g].append(nkv_p[g])
            kvs[g].append(nkv_s[g])
        ssm_p.append(hp)
        ssm_s.append(hs)
        conv_p.append(cp)
        conv_s.append(cs)
    y_prompt = rms_norm(yp, norm_f)
    y_sample = rms_norm(ys, norm_f)
    return (y_prompt, y_sample,
            jnp.stack(kvp[0]), jnp.stack(kvp[1]), jnp.stack(kvp[2]), jnp.stack(ssm_p), jnp.stack(conv_p),
            jnp.stack(kvs[0]), jnp.stack(kvs[1]), jnp.stack(kvs[2]), jnp.stack(ssm_s), jnp.stack(conv_s))
```

```python
import functools
import math

import jax
import jax.numpy as jnp
from jax import lax
from jax.experimental import pallas as pl
from jax.experimental.pallas import tpu as pltpu

F32 = jnp.float32
BF16 = jnp.bfloat16

PAST_LEN = 8192
ATT_WINDOWS = (128, 512, 2048)
ATT_DILATIONS = (1, 4, 16)
ATT_HEADS = 8
ATT_HEAD_DIM = 64
ATT_WIDTH = ATT_HEADS * ATT_HEAD_DIM
ATT_SCALE = ATT_HEAD_DIM ** -0.5
ROT_DIM = ATT_HEAD_DIM // 4
ROPE_THETA = 500000.0
ATT_BLOCK = 128

SSM_HEAD_DIM = 64
SSM_STATE = 128
SSM_GROUPS = 4
SSM_CONV = 4
SSM_CHUNK = 128
MOE_GROUPS = 4
MOE_PER_GROUP = 4
MOE_EXPERTS = MOE_GROUPS * MOE_PER_GROUP
RMS_EPS = 1e-6
SSM_NORM_EPS = 1e-5

LANES = 128
VMEM_LIMIT = 56 * 1024 * 1024
NEG_INF = float("-inf")


def _cparams(n_axes):
    return pltpu.CompilerParams(dimension_semantics=("arbitrary",) * n_axes,
                                vmem_limit_bytes=VMEM_LIMIT)


def _sigmoid(x):
    return 1.0 / (1.0 + jnp.exp(-x))


def _silu(x):
    return x * _sigmoid(x)


def _norm_kernel(x_ref, g_ref, o_ref):
    x = x_ref[...]
    y = x * lax.rsqrt(jnp.mean(x * x, axis=-1, keepdims=True) + RMS_EPS)
    o_ref[...] = (y * g_ref[...]).astype(o_ref.dtype)


def rmsnorm_bf16(x, g, tm):
    n, d = x.shape
    return pl.pallas_call(
        _norm_kernel, grid=(n // tm,),
        in_specs=[pl.BlockSpec((tm, d), lambda i: (i, 0)), pl.BlockSpec((1, d), lambda i: (0, 0))],
        out_specs=pl.BlockSpec((tm, d), lambda i: (i, 0)),
        out_shape=jax.ShapeDtypeStruct((n, d), BF16),
        compiler_params=_cparams(1), name="rmsnorm")(x, g.reshape(1, d))


def _mm_kernel(x_ref, w_ref, o_ref):
    o_ref[...] = jnp.dot(x_ref[...], w_ref[...], preferred_element_type=F32).astype(o_ref.dtype)


def matmul(x, w, out_dtype, tm, tn, name):
    m, k = x.shape
    n = w.shape[1]
    return pl.pallas_call(
        _mm_kernel, grid=(m // tm, n // tn),
        in_specs=[pl.BlockSpec((tm, k), lambda i, j: (i, 0)), pl.BlockSpec((k, tn), lambda i, j: (0, j))],
        out_specs=pl.BlockSpec((tm, tn), lambda i, j: (i, j)),
        out_shape=jax.ShapeDtypeStruct((m, n), out_dtype),
        compiler_params=_cparams(2), name=name)(x, w)


def _qkv_kernel(x_ref, w_ref, cos_ref, sa_ref, sb_ref, o_ref, acc_ref, *, d):
    j = pl.program_id(1)
    acc = jnp.dot(x_ref[...], w_ref[...], preferred_element_type=F32)
    tm = acc.shape[0]

    @pl.when(j < 2)
    def _():
        c = cos_ref[...]
        sa = sa_ref[...]
        sb = sb_ref[...]
        sc = jnp.where(j == 0, ATT_SCALE, 1.0).astype(F32)
        for ch in range(ATT_WIDTH // LANES):
            t = acc[:, ch * LANES:(ch + 1) * LANES]
            r = t * c + pltpu.roll(t, LANES - ROT_DIM // 2, 1) * sa + pltpu.roll(t, ROT_DIM // 2, 1) * sb
            acc_ref[ch] = r * sc

    @pl.when(j == 2)
    def _():
        for ch in range(ATT_WIDTH // LANES):
            acc_ref[ch] = acc[:, ch * LANES:(ch + 1) * LANES]

    for ch in range(ATT_WIDTH // LANES):
        cs = slice(ch * LANES, (ch + 1) * LANES)
        if d == 1:
            o_ref[0, :, cs] = acc_ref[ch].astype(o_ref.dtype)
        else:
            for r in range(d):
                o_ref[r, :, cs] = acc_ref[ch, pl.ds(r, tm // d, stride=d), :].astype(o_ref.dtype)


def qkv_proj(xn, w, tabs, batch, seq, d, tm, out_dtype, name):
    n, dm = xn.shape
    tpb = seq // tm
    cos, sa, sb = tabs
    tab_spec = pl.BlockSpec((tm, LANES), lambda i, j: (i % tpb, 0))
    return pl.pallas_call(
        functools.partial(_qkv_kernel, d=d), grid=(n // tm, 3),
        in_specs=[pl.BlockSpec((tm, dm), lambda i, j: (i, 0)),
                  pl.BlockSpec((dm, ATT_WIDTH), lambda i, j: (0, j)),
                  tab_spec, tab_spec, tab_spec],
        out_specs=pl.BlockSpec((None, None, d, tm // d, ATT_WIDTH),
                               lambda i, j: (j, i // tpb, 0, i % tpb, 0)),
        out_shape=jax.ShapeDtypeStruct((3, batch, d, seq // d, ATT_WIDTH), out_dtype),
        scratch_shapes=[pltpu.VMEM((ATT_WIDTH // LANES, tm, LANES), F32)],
        compiler_params=_cparams(2), name=name)(xn, w, cos, sa, sb)


def rope_tables(pos):
    half = ROT_DIM // 2
    inv = ROPE_THETA ** (-jnp.arange(half, dtype=F32) / half)
    ang = pos[:, None] * inv[None, :]
    cos = jnp.cos(ang)
    sin = jnp.sin(ang)
    s = pos.shape[0]
    ones = jnp.ones((s, ATT_HEAD_DIM - ROT_DIM), F32)
    zeros = jnp.zeros((s, ATT_HEAD_DIM - ROT_DIM), F32)
    zh = jnp.zeros((s, half), F32)
    c = jnp.concatenate([cos, cos, ones], axis=1)
    sa = jnp.concatenate([-sin, zh, zeros], axis=1)
    sb = jnp.concatenate([zh, sin, zeros], axis=1)
    rep = LANES // ATT_HEAD_DIM
    return tuple(jnp.tile(t, (1, rep)) for t in (c, sa, sb))


def _attn_kernel(q_ref, kc_ref, vc_ref, kp_ref, vp_ref, num_ref, st_ref):
    nb = pl.program_id(2)
    blk = q_ref.shape[0]
    qi = lax.broadcasted_iota(jnp.int32, (blk, blk), 0)
    kj = lax.broadcasted_iota(jnp.int32, (blk, blk), 1)
    mask_c = kj <= qi
    mask_p = jnp.logical_and(kj >= qi, nb > 0)
    lane = lax.broadcasted_iota(jnp.int32, (blk, LANES), 1)
    st = jnp.zeros((blk, LANES), F32)
    nt = (((1,), (1,)), ((), ()))
    for h in range(ATT_HEADS):
        sl = slice(h * ATT_HEAD_DIM, (h + 1) * ATT_HEAD_DIM)
        q = q_ref[:, sl]
        s_c = lax.dot_general(q, kc_ref[:, sl], nt, preferred_element_type=F32)
        s_p = lax.dot_general(q, kp_ref[:, sl], nt, preferred_element_type=F32)
        s_c = jnp.where(mask_c, s_c, NEG_INF)
        s_p = jnp.where(mask_p, s_p, NEG_INF)
        m = jnp.maximum(jnp.max(s_c, axis=-1, keepdims=True), jnp.max(s_p, axis=-1, keepdims=True))
        p_c = jnp.exp(s_c - m)
        p_p = jnp.exp(s_p - m)
        l = jnp.sum(p_p, axis=-1, keepdims=True) + jnp.sum(p_c, axis=-1, keepdims=True)
        num = (jnp.dot(p_p.astype(BF16), vp_ref[:, sl], preferred_element_type=F32)
               + jnp.dot(p_c.astype(BF16), vc_ref[:, sl], preferred_element_type=F32))
        num_ref[:, sl] = num
        st = jnp.where(lane == h, m, st)
        st = jnp.where(lane == ATT_HEADS + h, l, st)
    st_ref[...] = st


def attn_prompt(qkv):
    _, b, d, l, w = qkv.shape
    nblk = l // ATT_BLOCK

    def spec(kind, prev):
        if prev:
            return pl.BlockSpec((None, None, None, ATT_BLOCK, w),
                                lambda bi, r, n: (kind, bi, r, jnp.maximum(n - 1, 0), 0))
        return pl.BlockSpec((None, None, None, ATT_BLOCK, w), lambda bi, r, n: (kind, bi, r, n, 0))

    return pl.pallas_call(
        _attn_kernel, grid=(b, d, nblk),
        in_specs=[spec(0, False), spec(1, False), spec(2, False), spec(1, True), spec(2, True)],
        out_specs=[pl.BlockSpec((None, None, ATT_BLOCK, w), lambda bi, r, n: (bi, r, n, 0)),
                   pl.BlockSpec((None, None, ATT_BLOCK, LANES), lambda bi, r, n: (bi, r, n, 0))],
        out_shape=[jax.ShapeDtypeStruct((b, d, l, w), F32), jax.ShapeDtypeStruct((b, d, l, LANES), F32)],
        compiler_params=_cparams(3), name=f"attn_prompt_d{d}")(qkv, qkv, qkv, qkv, qkv)


def _col(mat, h, n_lanes=LANES):
    return jnp.broadcast_to(mat[:, h:h + 1], (mat.shape[0], n_lanes))


def _ssd_kernel(xbc_ref, z_ref, dt_ref, cw_ref, cb_ref, dtb_ref, alog_ref, dsk_ref, nw_ref,
                y_ref, hout_ref,
                cbuf, xs_s, xw_s, eac_s, cd_s, y_s, ht_s, *, d_inner, n_groups):
    c = pl.program_id(1)
    nc = pl.num_programs(1)
    q = SSM_CHUNK
    n_st = SSM_STATE
    gw = d_inner // n_groups
    conv_dim = xbc_ref.shape[1]
    top = 8

    @pl.when(c == 0)
    def _():
        cbuf[0:top, :] = jnp.zeros((top, conv_dim), F32)
        ht_s[...] = jnp.zeros(ht_s.shape, F32)

    cbuf[top:top + q, :] = xbc_ref[...].astype(F32)
    cwid = 512
    for j in range(conv_dim // cwid):
        cs = slice(j * cwid, (j + 1) * cwid)
        acc = cb_ref[:, cs] + cbuf[top - 3:top - 3 + q, cs] * cw_ref[0:1, cs]
        for i in range(1, SSM_CONV):
            acc = acc + cbuf[top - 3 + i:top - 3 + i + q, cs] * cw_ref[i:i + 1, cs]
        xs_s[:, cs] = _silu(acc)
    cbuf[top - 3:top, :] = cbuf[top + q - 3:top + q, :]

    dt = jax.nn.softplus(dt_ref[...] + dtb_ref[...])
    a_row = -jnp.exp(alog_ref[...])
    a = dt * a_row
    ri = lax.broadcasted_iota(jnp.int32, (q, q), 0)
    ci = lax.broadcasted_iota(jnp.int32, (q, q), 1)
    causal = ri >= ci
    tril = jnp.where(causal, 1.0, 0.0).astype(F32)
    acum = jnp.dot(tril, a, preferred_element_type=F32, precision=lax.Precision.HIGHEST)
    acum_t = acum.T
    lane = lax.broadcasted_iota(jnp.int32, (q, LANES), 1)
    lo = lane < SSM_HEAD_DIM

    hpg = gw // SSM_HEAD_DIM
    for g in range(n_groups):
        bc = xs_s[:, d_inner + g * n_st:d_inner + (g + 1) * n_st].astype(BF16)
        cc = xs_s[:, d_inner + n_groups * n_st + g * n_st:d_inner + n_groups * n_st + (g + 1) * n_st].astype(BF16)
        cb = lax.dot_general(cc, bc, (((1,), (1,)), ((), ())), preferred_element_type=F32)
        for jp in range(hpg // 2):
            h0 = g * hpg + 2 * jp
            ls = slice(g * gw + jp * LANES, g * gw + (jp + 1) * LANES)
            ac0 = _col(acum, h0)
            ac1 = _col(acum, h0 + 1)
            acum_e = jnp.where(lo, ac0, ac1)
            dt_e = jnp.where(lo, _col(dt, h0), _col(dt, h0 + 1))
            xdt = xs_s[:, ls] * dt_e
            acl_e = acum_e[q - 1:q, :]
            xw_s[:, ls] = (xdt * jnp.exp(acl_e - acum_e)).astype(BF16)
            eac_s[:, ls] = jnp.exp(acum_e)
            cd_s[:, ls] = jnp.exp(acl_e)
            xdt_b = xdt.astype(BF16)
            zero = jnp.zeros_like(xdt_b)
            m0 = (jnp.exp(jnp.where(causal, ac0 - acum_t[h0:h0 + 1, :], NEG_INF)) * cb).astype(BF16)
            m1 = (jnp.exp(jnp.where(causal, ac1 - acum_t[h0 + 1:h0 + 2, :], NEG_INF)) * cb).astype(BF16)
            y_s[:, ls] = (jnp.dot(m0, jnp.where(lo, xdt_b, zero), preferred_element_type=F32)
                          + jnp.dot(m1, jnp.where(lo, zero, xdt_b), preferred_element_type=F32))
        gs = slice(g * gw, (g + 1) * gw)
        h_prev = ht_s[g]
        y_off = jnp.dot(cc, h_prev.astype(BF16), preferred_element_type=F32) * eac_s[:, gs]
        y_s[:, gs] = y_s[:, gs] + y_off
        st = lax.dot_general(bc, xw_s[:, gs], (((0,), (0,)), ((), ())), preferred_element_type=F32)
        ht_s[g] = cd_s[:, gs] * h_prev + st

    for g in range(n_groups):
        gs = slice(g * gw, (g + 1) * gw)
        y = y_s[:, gs] + dsk_ref[:, gs] * xs_s[:, gs]
        yf = y * _silu(z_ref[:, gs].astype(F32))
        yf = yf * lax.rsqrt(jnp.mean(yf * yf, axis=-1, keepdims=True) + SSM_NORM_EPS)
        y_ref[:, gs] = (yf * nw_ref[:, gs]).astype(y_ref.dtype)

    @pl.when(c == nc - 1)
    def _():
        for g in range(n_groups):
            hout_ref[g * gw:(g + 1) * gw, :] = ht_s[g].T


def ssd_prompt(xbc, z, dt_raw, conv_w, conv_b, dtb, alog, dsk_e, nw, batch, seq):
    n, conv_dim = xbc.shape
    d_inner = z.shape[1]
    n_groups = SSM_GROUPS
    gw = d_inner // n_groups
    q = SSM_CHUNK
    cps = seq // q
    row = lambda w: pl.BlockSpec((1, w), lambda b, c: (0, 0))
    tok = lambda w: pl.BlockSpec((q, w), lambda b, c: (b * cps + c, 0))
    return pl.pallas_call(
        functools.partial(_ssd_kernel, d_inner=d_inner, n_groups=n_groups),
        grid=(batch, cps),
        in_specs=[tok(conv_dim), tok(d_inner), tok(LANES),
                  pl.BlockSpec((SSM_CONV, conv_dim), lambda b, c: (0, 0)), row(conv_dim),
                  row(LANES), row(LANES), row(d_inner), row(d_inner)],
        out_specs=[tok(d_inner), pl.BlockSpec((None, d_inner, SSM_STATE), lambda b, c: (b, 0, 0))],
        out_shape=[jax.ShapeDtypeStruct((n, d_inner), BF16),
                   jax.ShapeDtypeStruct((batch, d_inner, SSM_STATE), F32)],
        scratch_shapes=[pltpu.VMEM((8 + q, conv_dim), F32),
                        pltpu.VMEM((q, conv_dim), F32),
                        pltpu.VMEM((q, d_inner), BF16),
                        pltpu.VMEM((q, d_inner), F32),
                        pltpu.VMEM((1, d_inner), F32),
                        pltpu.VMEM((q, d_inner), F32),
                        pltpu.VMEM((n_groups, SSM_STATE, gw), F32)],
        compiler_params=_cparams(2), name="ssd_prompt")(xbc, z, dt_raw, conv_w, conv_b, dtb, alog, dsk_e, nw)


TPAD = 8


def _ssd_step_kernel(xbc_ref, z_ref, dt_ref, cst_ref, h0_ref, cw_ref, cb_ref, dtb_ref, alog_ref,
                     dsk_ref, nw_ref, exp_ref, y_ref, hout_ref,
                     cbuf, xs_s, tin_s, tt_s, ycol_s, *, d_inner, n_groups, n_tok):
    b = pl.program_id(0)
    n_st = SSM_STATE
    gw = d_inner // n_groups
    conv_dim = xbc_ref.shape[1]
    top = 8
    hist = SSM_CONV - 1

    @pl.when(b == 0)
    def _():
        tin_s[...] = jnp.zeros(tin_s.shape, F32)
        cbuf[0:top, :] = jnp.zeros((top, conv_dim), F32)

    cbuf[top - hist:top, :] = cst_ref[...]
    cbuf[top:top + TPAD, :] = xbc_ref[...]
    acc = cb_ref[...] + cbuf[top - hist:top - hist + TPAD, :] * cw_ref[0:1, :]
    for i in range(1, SSM_CONV):
        acc = acc + cbuf[top - hist + i:top - hist + i + TPAD, :] * cw_ref[i:i + 1, :]
    xs_s[...] = _silu(acc)

    dt = jax.nn.softplus(dt_ref[...] + dtb_ref[...])
    da = jnp.exp(dt * (-jnp.exp(alog_ref[...])))
    hi = lax.Precision.HIGHEST
    dt_e = jnp.dot(dt, exp_ref[...], preferred_element_type=F32, precision=hi)
    da_e = jnp.dot(da, exp_ref[...], preferred_element_type=F32, precision=hi)
    tin_s[0:TPAD, :] = xs_s[:, 0:d_inner] * dt_e
    tin_s[TPAD:2 * TPAD, :] = da_e
    tt_s[...] = tin_s[...].T

    rows = 256
    lane = lax.broadcasted_iota(jnp.int32, (rows, LANES), 1)
    for ck in range(d_inner // rows):
        g = (ck * rows) // gw
        rs = slice(ck * rows, (ck + 1) * rows)
        h = h0_ref[rs, :]
        yc = jnp.zeros((rows, LANES), F32)
        for t in range(n_tok):
            b_row = xs_s[t:t + 1, d_inner + g * n_st:d_inner + (g + 1) * n_st]
            c_row = xs_s[t:t + 1, d_inner + (n_groups + g) * n_st:d_inner + (n_groups + g + 1) * n_st]
            h = tt_s[rs, TPAD + t:TPAD + t + 1] * h + tt_s[rs, t:t + 1] * b_row
            yt = jnp.sum(h * c_row, axis=-1, keepdims=True)
            yc = jnp.where(lane == t, yt, yc)
        hout_ref[rs, :] = h
        ycol_s[rs, :] = yc

    y_rows = ycol_s[...].T
    for g in range(n_groups):
        gs = slice(g * gw, (g + 1) * gw)
        y = y_rows[0:TPAD, gs] + dsk_ref[:, gs] * xs_s[:, gs]
        yf = y * _silu(z_ref[:, gs])
        yf = yf * lax.rsqrt(jnp.mean(yf * yf, axis=-1, keepdims=True) + SSM_NORM_EPS)
        y_ref[:, gs] = yf * nw_ref[:, gs]


def ssd_sample(xbc, z, dt_raw, conv_state, h0, conv_w, conv_b, dtb, alog, dsk_e, nw, expand, n_tok):
    bsz, _, conv_dim = xbc.shape
    d_inner = z.shape[2]
    row = lambda w: pl.BlockSpec((1, w), lambda b: (0, 0))
    tok = lambda w: pl.BlockSpec((None, TPAD, w), lambda b: (b, 0, 0))
    st = pl.BlockSpec((None, d_inner, SSM_STATE), lambda b: (b, 0, 0))
    return pl.pallas_call(
        functools.partial(_ssd_step_kernel, d_inner=d_inner, n_groups=SSM_GROUPS, n_tok=n_tok),
        grid=(bsz,),
        in_specs=[tok(conv_dim), tok(d_inner), tok(LANES),
                  pl.BlockSpec((None, SSM_CONV - 1, conv_dim), lambda b: (b, 0, 0)), st,
                  pl.BlockSpec((SSM_CONV, conv_dim), lambda b: (0, 0)), row(conv_dim),
                  row(LANES), row(LANES), row(d_inner), row(d_inner),
                  pl.BlockSpec((LANES, d_inner), lambda b: (0, 0))],
        out_specs=[tok(d_inner), st],
        out_shape=[jax.ShapeDtypeStruct((bsz, TPAD, d_inner), F32),
                   jax.ShapeDtypeStruct((bsz, d_inner, SSM_STATE), F32)],
        scratch_shapes=[pltpu.VMEM((8 + TPAD, conv_dim), F32),
                        pltpu.VMEM((TPAD, conv_dim), F32),
                        pltpu.VMEM((LANES, d_inner), F32),
                        pltpu.VMEM((d_inner, LANES), F32),
                        pltpu.VMEM((d_inner, LANES), F32)],
        compiler_params=_cparams(1), name="ssd_sample")(
            xbc, z, dt_raw, conv_state, h0, conv_w, conv_b, dtb, alog, dsk_e, nw, expand)


def _attn_step_kernel(q_ref, k_ref, v_ref, c0_ref, c1_ref, c2_ref, o_ref, *, n_tok):
    wrows = c0_ref.shape[0]
    row_id = lax.broadcasted_iota(jnp.int32, (wrows, ATT_HEADS, 1), 0)

    def finish(s, vc, s_new, v_new):
        m = jnp.max(s, axis=0)
        for sn in s_new:
            m = jnp.maximum(m, sn)
        p = jnp.exp(s - m[None])
        l = jnp.sum(p, axis=0)
        num = jnp.sum(p * vc, axis=0)
        for sn, vn in zip(s_new, v_new):
            pn = jnp.exp(sn - m)
            l = l + pn
            num = num + pn * vn
        return num, m, l

    for t in range(n_tok):
        parts = []
        qt = q_ref[0, t]
        s = jnp.sum(c0_ref[:, 0] * qt[None], axis=-1, keepdims=True)
        s = jnp.where(row_id >= t, s, NEG_INF)
        s_new = [jnp.sum(k_ref[0, u] * qt, axis=-1, keepdims=True) for u in range(t + 1)]
        parts.append(finish(s, c0_ref[:, 1], s_new, [v_ref[0, u] for u in range(t + 1)]))
        for g, cref in ((1, c1_ref), (2, c2_ref)):
            qt = q_ref[g, t]
            s = jnp.sum(cref[:, t, 0] * qt[None], axis=-1, keepdims=True)
            s_new = [jnp.sum(k_ref[g, t] * qt, axis=-1, keepdims=True)]
            parts.append(finish(s, cref[:, t, 1], s_new, [v_ref[g, t]]))
        mx = jnp.maximum(jnp.maximum(parts[0][1], parts[1][1]), parts[2][1])
        w = [jnp.exp(p[1] - mx) for p in parts]
        num = w[0] * parts[0][0]
        for g in range(1, 3):
            num = num + w[g] * parts[g][0]
        den = w[0] * parts[0][2] + w[1] * parts[1][2] + w[2] * parts[2][2]
        o_ref[t] = num / den


def attn_sample(q, k, v, c0, c1, c2, n_tok):
    n_g, bsz, t, h, hd = q.shape
    new = pl.BlockSpec((n_g, None, t, h, hd), lambda b: (0, b, 0, 0, 0))
    out = pl.BlockSpec((None, t, h, hd), lambda b: (b, 0, 0, 0))
    c0s = pl.BlockSpec((None,) + c0.shape[1:], lambda b: (b, 0, 0, 0, 0))
    c1s = pl.BlockSpec((None, c1.shape[1], t, 2, h, hd), lambda b: (b, 0, 0, 0, 0, 0))
    c2s = pl.BlockSpec((None, c2.shape[1], t, 2, h, hd), lambda b: (b, 0, 0, 0, 0, 0))
    return pl.pallas_call(
        functools.partial(_attn_step_kernel, n_tok=n_tok), grid=(bsz,),
        in_specs=[new, new, new, c0s, c1s, c2s],
        out_specs=out,
        out_shape=jax.ShapeDtypeStruct((bsz, t, h, hd), F32),
        compiler_params=_cparams(1), name="attn_sample")(q, k, v, c0, c1, c2)


COPY_CHUNKS = 8


def _cache_shift_kernel(*refs, n_cache):
    caches = refs[:n_cache]
    news = refs[n_cache:2 * n_cache]
    outs = refs[2 * n_cache:3 * n_cache]
    sem = refs[3 * n_cache]
    copies = []
    for g in range(n_cache):
        bsz, w = caches[g].shape[0], caches[g].shape[1]
        t = news[g].shape[1]
        bc = bsz // COPY_CHUNKS
        for i in range(COPY_CHUNKS):
            bs = pl.ds(i * bc, bc)
            copies.append(pltpu.make_async_copy(caches[g].at[bs, pl.ds(t, w - t)],
                                                outs[g].at[bs, pl.ds(0, w - t)],
                                                sem.at[g, i]))
        copies.append(pltpu.make_async_copy(news[g], outs[g].at[pl.ds(0, bsz), pl.ds(w - t, t)],
                                            sem.at[g, COPY_CHUNKS]))
    for cp in copies:
        cp.start()
    for cp in copies:
        cp.wait()


def cache_shift(caches, news):
    n = len(caches)
    any_spec = pl.BlockSpec(memory_space=pl.ANY)
    return pl.pallas_call(
        functools.partial(_cache_shift_kernel, n_cache=n),
        in_specs=[any_spec] * (2 * n),
        out_specs=[any_spec] * n,
        out_shape=[jax.ShapeDtypeStruct(c.shape, c.dtype) for c in caches],
        scratch_shapes=[pltpu.SemaphoreType.DMA((n, COPY_CHUNKS + 1))],
        name="cache_shift")(*caches, *news)


def _router(logits):
    lanef = lax.broadcasted_iota(jnp.int32, logits.shape, 1).astype(F32)
    big = 1e9
    lc = jnp.where(lanef < MOE_GROUPS, logits, NEG_INF)
    mc = jnp.max(lc, axis=-1, keepdims=True)
    g_sel = jnp.min(jnp.where(lc == mc, lanef, big), axis=-1, keepdims=True)
    p_sel = 1.0 / jnp.sum(jnp.exp(lc - mc), axis=-1, keepdims=True)
    base = MOE_GROUPS + MOE_PER_GROUP * g_sel
    lf = jnp.where(jnp.logical_and(lanef >= base, lanef < base + MOE_PER_GROUP), logits, NEG_INF)
    v1 = jnp.max(lf, axis=-1, keepdims=True)
    i1 = jnp.min(jnp.where(lf == v1, lanef, big), axis=-1, keepdims=True)
    lf2 = jnp.where(lanef == i1, NEG_INF, lf)
    v2 = jnp.max(lf2, axis=-1, keepdims=True)
    i2 = jnp.min(jnp.where(lf2 == v2, lanef, big), axis=-1, keepdims=True)
    e2 = jnp.exp(v2 - v1)
    den = 1.0 + e2
    w1 = (1.0 / den) * p_sel
    w2 = (e2 / den) * p_sel
    return jnp.where(lanef == i1, w1, 0.0) + jnp.where(lanef == i2, w2, 0.0)


def _outproj_kernel(*refs, dils, tm):
    n_g = len(dils)
    if n_g:
        x_ref, g_ref, ys_ref = refs[:3]
        att_refs = refs[3:3 + 2 * n_g]
        rest = refs[3 + 2 * n_g:]
    else:
        x_ref, g_ref, ys_ref, attn_ref = refs[:4]
        rest = refs[4:]
    wa_ref, ws_ref, wo_ref, n2_ref, wr_ref, br_ref, x2_ref, xn_ref, cmb_ref = rest[:9]
    scr = rest[9:]

    if n_g:
        n_ch = ATT_WIDTH // LANES
        nums, stats = [], []
        for g, d in enumerate(dils):
            num_ref, st_ref = att_refs[2 * g], att_refs[2 * g + 1]
            if d == 1:
                nums.append([num_ref[0, :, ch * LANES:(ch + 1) * LANES] for ch in range(n_ch)])
                stats.append(st_ref[0])
            else:
                ns, ss = scr[2 * g], scr[2 * g + 1]
                for r in range(d):
                    for ch in range(n_ch):
                        ns[ch, pl.ds(r, tm // d, stride=d), :] = num_ref[r, :, ch * LANES:(ch + 1) * LANES]
                    ss[pl.ds(r, tm // d, stride=d), :] = st_ref[r]
                nums.append([ns[ch] for ch in range(n_ch)])
                stats.append(ss[...])
        lane = lax.broadcasted_iota(jnp.int32, (tm, LANES), 1)
        lo = lane < ATT_HEAD_DIM
        chunks = []
        for ch in range(ATT_WIDTH // LANES):
            wts, dens = [], []
            for h in (2 * ch, 2 * ch + 1):
                ms = [s[:, h:h + 1] for s in stats]
                ls = [s[:, ATT_HEADS + h:ATT_HEADS + h + 1] for s in stats]
                mx = ms[0]
                for m in ms[1:]:
                    mx = jnp.maximum(mx, m)
                w = [jnp.exp(m - mx) for m in ms]
                den = w[0] * ls[0]
                for g in range(1, n_g):
                    den = den + w[g] * ls[g]
                wts.append(w)
                dens.append(den)
            num = jnp.where(lo, wts[0][0], wts[1][0]) * nums[0][ch]
            for g in range(1, n_g):
                num = num + jnp.where(lo, wts[0][g], wts[1][g]) * nums[g][ch]
            chunks.append((num / jnp.where(lo, dens[0], dens[1])).astype(BF16))
        attn = jnp.concatenate(chunks, axis=1)
    else:
        attn = attn_ref[...].astype(BF16)

    dm = x_ref.shape[1]
    a = jnp.dot(attn, wa_ref[...], preferred_element_type=F32)
    s = jnp.dot(ys_ref[...].astype(BF16), ws_ref[...], preferred_element_type=F32)
    gates = g_ref[...].astype(F32)
    mixed = _sigmoid(gates[:, :dm]) * a + _sigmoid(gates[:, dm:]) * s
    x2 = x_ref[...] + jnp.dot(mixed.astype(BF16), wo_ref[...], preferred_element_type=F32)
    x2_ref[...] = x2
    xn = (x2 * lax.rsqrt(jnp.mean(x2 * x2, axis=-1, keepdims=True) + RMS_EPS) * n2_ref[...]).astype(BF16)
    xn_ref[...] = xn
    logits = jnp.dot(xn, wr_ref[...], preferred_element_type=F32) + br_ref[...]
    cmb_ref[...] = _router(logits)


def outproj(x, gates, yssm, att, dils, batch, seq, weights, tm):
    n, dm = x.shape
    wa, ws, wo, n2, wr, br = weights
    tpb = seq // tm
    tok = lambda w: pl.BlockSpec((tm, w), lambda i: (i, 0))
    full = lambda a: pl.BlockSpec(a.shape, lambda i: (0,) * a.ndim)
    in_specs = [tok(dm), tok(gates.shape[1]), tok(yssm.shape[1])]
    args = [x, gates, yssm]
    scratch = []
    if dils:
        for (num, st), d in zip(att, dils):
            for arr in (num, st):
                in_specs.append(pl.BlockSpec((None, d, tm // d, arr.shape[-1]),
                                             lambda i: (i // tpb, 0, i % tpb, 0)))
                args.append(arr)
                scratch.append(pltpu.VMEM((tm, LANES) if arr.shape[-1] == LANES
                                          else (arr.shape[-1] // LANES, tm, LANES), F32))
    else:
        in_specs.append(tok(att.shape[1]))
        args.append(att)
    in_specs += [full(wa), full(ws), full(wo), full(n2), full(wr), full(br)]
    args += [wa, ws, wo, n2, wr, br]
    return pl.pallas_call(
        functools.partial(_outproj_kernel, dils=tuple(dils), tm=tm), grid=(n // tm,),
        in_specs=in_specs,
        out_specs=[tok(dm), tok(dm), tok(LANES)],
        out_shape=[jax.ShapeDtypeStruct((n, dm), F32), jax.ShapeDtypeStruct((n, dm), BF16),
                   jax.ShapeDtypeStruct((n, LANES), F32)],
        scratch_shapes=scratch,
        compiler_params=_cparams(1), name="outproj")(*args)


def _moe_kernel(xn_ref, cmb_ref, x2_ref, wg_ref, wu_ref, wd_ref, nf_ref, o_ref, acc_ref, *, final_norm):
    e = pl.program_id(1)
    ne = pl.num_programs(1)

    @pl.when(e == 0)
    def _():
        acc_ref[...] = jnp.zeros(acc_ref.shape, F32)

    xn = xn_ref[...]
    he = _silu(jnp.dot(xn, wg_ref[...], preferred_element_type=F32)) * jnp.dot(
        xn, wu_ref[...], preferred_element_type=F32)
    ye = jnp.dot(he.astype(BF16), wd_ref[...], preferred_element_type=F32)
    cmb = cmb_ref[...]
    lane = lax.broadcasted_iota(jnp.int32, cmb.shape, 1)
    ce = jnp.sum(jnp.where(lane == e + MOE_GROUPS, cmb, 0.0), axis=-1, keepdims=True)
    acc_ref[...] += ce * ye

    @pl.when(e == ne - 1)
    def _():
        xf = x2_ref[...] + acc_ref[...]
        if final_norm:
            xf = xf * lax.rsqrt(jnp.mean(xf * xf, axis=-1, keepdims=True) + RMS_EPS) * nf_ref[...]
        o_ref[...] = xf


def moe_final(xn, cmb, x2, wg, wu, wd, nf, tm, final_norm):
    n, dm = xn.shape
    ne, _, ff = wg.shape
    tok = lambda w: pl.BlockSpec((tm, w), lambda i, e: (i, 0))
    return pl.pallas_call(
        functools.partial(_moe_kernel, final_norm=final_norm), grid=(n // tm, ne),
        in_specs=[tok(dm), tok(LANES), tok(dm),
                  pl.BlockSpec((None, dm, ff), lambda i, e: (e, 0, 0)),
                  pl.BlockSpec((None, dm, ff), lambda i, e: (e, 0, 0)),
                  pl.BlockSpec((None, ff, dm), lambda i, e: (e, 0, 0)),
                  pl.BlockSpec((1, dm), lambda i, e: (0, 0))],
        out_specs=tok(dm),
        out_shape=jax.ShapeDtypeStruct((n, dm), F32),
        scratch_shapes=[pltpu.VMEM((tm, dm), F32)],
        compiler_params=_cparams(2), name="moe_final")(xn, cmb, x2, wg, wu, wd, nf)


def _pad_lanes(v, width=LANES):
    return jnp.pad(v, ((0, 0), (0, width - v.shape[1])))


def _natural_rows(arr, n_rows):
    b, d, l, w = arr.shape
    tail = arr[:, :, l - n_rows // d:, :]
    return jnp.swapaxes(tail, 1, 2).reshape(b, n_rows, w)


def kernel(x_prompt, x_sample, cache_kv_w128, cache_kv_w512, cache_kv_w2048, state_ssm, state_conv,
           norm1, w_in, conv_w, conv_b, dt_bias, a_log, d_skip, ssm_norm, w_att_out, w_ssm_out, w_o,
           norm2, w_router_coarse, b_router_coarse, w_router_fine, b_router_fine,
           w_exp_gate, w_exp_up, w_exp_down, norm_f):
    bp, sp, dm = x_prompt.shape
    bs, ts, _ = x_sample.shape
    depth = w_in.shape[0]
    n_heads = dt_bias.shape[1]
    d_inner = n_heads * SSM_HEAD_DIM
    conv_dim = conv_w.shape[2]
    n_grp = len(ATT_DILATIONS)
    off_z = n_grp * 3 * ATT_WIDTH
    off_xbc = off_z + d_inner
    off_dt = off_xbc + conv_dim
    off_gate = off_dt + n_heads
    caches = (cache_kv_w128, cache_kv_w512, cache_kv_w2048)
    for g in range(n_grp):
        assert caches[g].shape[2] == ATT_WINDOWS[g] and sp % (ATT_DILATIONS[g] * ATT_BLOCK) == 0
    assert ts <= min(ATT_DILATIONS[1:]) and ts <= TPAD and sp % SSM_CHUNK == 0

    tabs_p = rope_tables(jnp.arange(sp, dtype=F32))
    tabs_s = rope_tables(jnp.tile(jnp.arange(ts, dtype=F32) + PAST_LEN, bs))
    expand = (jnp.arange(LANES)[:, None] == (jnp.arange(d_inner) // SSM_HEAD_DIM)[None, :]).astype(F32)

    xp = x_prompt.reshape(bp * sp, dm)
    xs = x_sample.reshape(bs * ts, dm)
    outs = {k: [] for k in ("kvp0", "kvp1", "kvp2", "ssm_p", "conv_p", "kvs0", "kvs1", "kvs2", "ssm_s", "conv_s")}
    n_s = bs * ts
    for layer in range(depth):
        w = w_in[layer].astype(BF16)
        w_qkv = [w[:, g * 3 * ATT_WIDTH:(g + 1) * 3 * ATT_WIDTH] for g in range(n_grp)]
        w_z, w_xbc, w_gate = w[:, off_z:off_xbc], w[:, off_xbc:off_dt], w[:, off_gate:]
        w_dt = _pad_lanes(w[:, off_dt:off_gate])
        dtb = _pad_lanes(dt_bias[layer][None])
        alog = _pad_lanes(a_log[layer][None])
        dsk_e = jnp.repeat(d_skip[layer], SSM_HEAD_DIM)[None]
        nw = ssm_norm[layer][None]
        cw, cb = conv_w[layer], conv_b[layer][None]
        w_router = _pad_lanes(jnp.concatenate([w_router_coarse[layer], w_router_fine[layer]], axis=1)).astype(BF16)
        b_router = _pad_lanes(jnp.concatenate([b_router_coarse[layer], b_router_fine[layer]])[None])
        wts = (w_att_out[layer].astype(BF16), w_ssm_out[layer].astype(BF16), w_o[layer].astype(BF16),
               norm2[layer][None], w_router, b_router)
        wg, wu, wd = (w_exp_gate[layer].astype(BF16), w_exp_up[layer].astype(BF16),
                      w_exp_down[layer].astype(BF16))

        xn = rmsnorm_bf16(xp, norm1[layer], 1024)
        att = []
        for g, d in enumerate(ATT_DILATIONS):
            qkv = qkv_proj(xn, w_qkv[g], tabs_p, bp, sp, d, 1024, BF16, f"qkv_prompt_d{d}")
            att.append(attn_prompt(qkv))
            wnd = min(ATT_WINDOWS[g], sp)
            kv = jnp.stack([_natural_rows(qkv[1], wnd), _natural_rows(qkv[2], wnd)], axis=2)
            outs[f"kvp{g}"].append(kv.astype(F32).reshape(bp, wnd, 2, ATT_HEADS, ATT_HEAD_DIM))
        z = matmul(xn, w_z, BF16, 1024, 512, "proj_z")
        xbc = matmul(xn, w_xbc, BF16, 1024, 512, "proj_xbc")
        gates = matmul(xn, w_gate, BF16, 1024, 512, "proj_gates")
        dt_raw = matmul(xn, w_dt, F32, 1024, LANES, "proj_dt")
        y_ssm, h_p = ssd_prompt(xbc, z, dt_raw, cw, cb, dtb, alog, dsk_e, nw, bp, sp)
        outs["ssm_p"].append(h_p.reshape(bp, n_heads, SSM_HEAD_DIM, SSM_STATE))
        outs["conv_p"].append(xbc.reshape(bp, sp, conv_dim)[:, sp - (SSM_CONV - 1):].astype(F32))
        x2, xn2, cmb = outproj(xp, gates, y_ssm, att, ATT_DILATIONS, bp, sp, wts, 256)
        xp = moe_final(xn2, cmb, x2, wg, wu, wd, norm_f[None], 512, layer == depth - 1)

        xn = rmsnorm_bf16(xs, norm1[layer], n_s)
        pad_t = lambda a: jnp.pad(a.reshape(bs, ts, a.shape[-1]), ((0, 0), (0, TPAD - ts), (0, 0)))
        q_s, k_s, v_s, news = [], [], [], []
        shp = (bs, ts, ATT_HEADS, ATT_HEAD_DIM)
        for g in range(n_grp):
            qkv = qkv_proj(xn, w_qkv[g], tabs_s, 1, n_s, 1, n_s, F32, f"qkv_sample_{g}")
            q_s.append(qkv[0].reshape(shp))
            k_s.append(qkv[1].reshape(shp))
            v_s.append(qkv[2].reshape(shp))
            news.append(jnp.stack([k_s[g], v_s[g]], axis=2))
        cl = [caches[g][layer] for g in range(n_grp)]
        new_caches = cache_shift(cl, news)
        for g in range(n_grp):
            outs[f"kvs{g}"].append(new_caches[g])
        views = [cl[0]] + [cl[g].reshape(bs, ATT_WINDOWS[g] // ATT_DILATIONS[g], ATT_DILATIONS[g], 2,
                                         ATT_HEADS, ATT_HEAD_DIM) for g in (1, 2)]
        attn_s = attn_sample(jnp.stack(q_s), jnp.stack(k_s), jnp.stack(v_s), views[0], views[1], views[2], ts)
        z = matmul(xn, w_z, F32, n_s, 512, "proj_z_s")
        xbc = matmul(xn, w_xbc, F32, n_s, 512, "proj_xbc_s")
        gates = matmul(xn, w_gate, F32, n_s, 512, "proj_gates_s")
        dt_raw = matmul(xn, w_dt, F32, n_s, LANES, "proj_dt_s")
        y_s, h_s = ssd_sample(pad_t(xbc), pad_t(z), pad_t(dt_raw), state_conv[layer],
                              state_ssm[layer].reshape(bs, d_inner, SSM_STATE), cw, cb, dtb, alog, dsk_e, nw,
                              expand, ts)
        outs["ssm_s"].append(h_s.reshape(bs, n_heads, SSM_HEAD_DIM, SSM_STATE))
        hist = jnp.concatenate([state_conv[layer], xbc.reshape(bs, ts, conv_dim)], axis=1)
        outs["conv_s"].append(hist[:, ts:])
        y_s = y_s[:, :ts].reshape(n_s, d_inner)
        x2, xn2, cmb = outproj(xs, gates, y_s, attn_s.reshape(n_s, ATT_WIDTH), (), 1, n_s, wts, min(256, n_s))
        xs = moe_final(xn2, cmb, x2, wg, wu, wd, norm_f[None], n_s, layer == depth - 1)

    st = lambda k: jnp.stack(outs[k])
    return (xp.reshape(bp, sp, dm), xs.reshape(bs, ts, dm),
            st("kvp0"), st("kvp1"), st("kvp2"), st("ssm_p"), st("conv_p"),
            st("kvs0"), st("kvs1"), st("kvs2"), st("ssm_s"), st("conv_s"))
```

```python
import functools
import math

import jax
import jax.numpy as jnp
from jax import lax
from jax.experimental import pallas as pl
from jax.experimental.pallas import tpu as pltpu

F32 = jnp.float32
BF16 = jnp.bfloat16

PAST_LEN = 8192
ATT_WINDOWS = (128, 512, 2048)
ATT_DILATIONS = (1, 4, 16)
ATT_HEADS = 8
ATT_HEAD_DIM = 64
ATT_WIDTH = ATT_HEADS * ATT_HEAD_DIM
ATT_SCALE = ATT_HEAD_DIM ** -0.5
ROT_DIM = ATT_HEAD_DIM // 4
ROPE_THETA = 500000.0
ATT_BLOCK = 128

SSM_HEAD_DIM = 64
SSM_STATE = 128
SSM_GROUPS = 4
SSM_CONV = 4
SSM_CHUNK = 128
MOE_GROUPS = 4
MOE_PER_GROUP = 4
MOE_EXPERTS = MOE_GROUPS * MOE_PER_GROUP
RMS_EPS = 1e-6
SSM_NORM_EPS = 1e-5

LANES = 128
VMEM_LIMIT = 56 * 1024 * 1024
NEG_INF = float("-inf")


def _cparams(n_axes):
    return pltpu.CompilerParams(dimension_semantics=("arbitrary",) * n_axes,
                                vmem_limit_bytes=VMEM_LIMIT)


def _sigmoid(x):
    return 1.0 / (1.0 + jnp.exp(-x))


def _silu(x):
    return x * _sigmoid(x)


def _norm_kernel(x_ref, g_ref, o_ref):
    x = x_ref[...]
    y = x * lax.rsqrt(jnp.mean(x * x, axis=-1, keepdims=True) + RMS_EPS)
    o_ref[...] = (y * g_ref[...]).astype(o_ref.dtype)


def rmsnorm_bf16(x, g, tm):
    n, d = x.shape
    return pl.pallas_call(
        _norm_kernel, grid=(n // tm,),
        in_specs=[pl.BlockSpec((tm, d), lambda i: (i, 0)), pl.BlockSpec((1, d), lambda i: (0, 0))],
        out_specs=pl.BlockSpec((tm, d), lambda i: (i, 0)),
        out_shape=jax.ShapeDtypeStruct((n, d), BF16),
        compiler_params=_cparams(1), name="rmsnorm")(x, g.reshape(1, d))


def _mm_kernel(x_ref, w_ref, o_ref):
    o_ref[...] = jnp.dot(x_ref[...], w_ref[...], preferred_element_type=F32).astype(o_ref.dtype)


def matmul(x, w, out_dtype, tm, tn, name):
    m, k = x.shape
    n = w.shape[1]
    return pl.pallas_call(
        _mm_kernel, grid=(m // tm, n // tn),
        in_specs=[pl.BlockSpec((tm, k), lambda i, j: (i, 0)), pl.BlockSpec((k, tn), lambda i, j: (0, j))],
        out_specs=pl.BlockSpec((tm, tn), lambda i, j: (i, j)),
        out_shape=jax.ShapeDtypeStruct((m, n), out_dtype),
        compiler_params=_cparams(2), name=name)(x, w)


def _qkv_kernel(x_ref, w_ref, cos_ref, sa_ref, sb_ref, o_ref, acc_ref, *, d):
    j = pl.program_id(1)
    acc = jnp.dot(x_ref[...], w_ref[...], preferred_element_type=F32)
    tm = acc.shape[0]

    @pl.when(j < 2)
    def _():
        c = cos_ref[...]
        sa = sa_ref[...]
        sb = sb_ref[...]
        sc = jnp.where(j == 0, ATT_SCALE, 1.0).astype(F32)
        for ch in range(ATT_WIDTH // LANES):
            t = acc[:, ch * LANES:(ch + 1) * LANES]
            r = t * c + pltpu.roll(t, LANES - ROT_DIM // 2, 1) * sa + pltpu.roll(t, ROT_DIM // 2, 1) * sb
            acc_ref[ch] = r * sc

    @pl.when(j == 2)
    def _():
        for ch in range(ATT_WIDTH // LANES):
            acc_ref[ch] = acc[:, ch * LANES:(ch + 1) * LANES]

    for ch in range(ATT_WIDTH // LANES):
        cs = slice(ch * LANES, (ch + 1) * LANES)
        if d == 1:
            o_ref[0, :, cs] = acc_ref[ch].astype(o_ref.dtype)
        else:
            for r in range(d):
                o_ref[r, :, cs] = acc_ref[ch, pl.ds(r, tm // d, stride=d), :].astype(o_ref.dtype)


def qkv_proj(xn, w, tabs, batch, seq, d, tm, out_dtype, name):
    n, dm = xn.shape
    tpb = seq // tm
    cos, sa, sb = tabs
    tab_spec = pl.BlockSpec((tm, LANES), lambda i, j: (i % tpb, 0))
    return pl.pallas_call(
        functools.partial(_qkv_kernel, d=d), grid=(n // tm, 3),
        in_specs=[pl.BlockSpec((tm, dm), lambda i, j: (i, 0)),
                  pl.BlockSpec((dm, ATT_WIDTH), lambda i, j: (0, j)),
                  tab_spec, tab_spec, tab_spec],
        out_specs=pl.BlockSpec((None, None, d, tm // d, ATT_WIDTH),
                               lambda i, j: (j, i // tpb, 0, i % tpb, 0)),
        out_shape=jax.ShapeDtypeStruct((3, batch, d, seq // d, ATT_WIDTH), out_dtype),
        scratch_shapes=[pltpu.VMEM((ATT_WIDTH // LANES, tm, LANES), F32)],
        compiler_params=_cparams(2), name=name)(xn, w, cos, sa, sb)


def rope_tables(pos):
    half = ROT_DIM // 2
    inv = ROPE_THETA ** (-jnp.arange(half, dtype=F32) / half)
    ang = pos[:, None] * inv[None, :]
    cos = jnp.cos(ang)
    sin = jnp.sin(ang)
    s = pos.shape[0]
    ones = jnp.ones((s, ATT_HEAD_DIM - ROT_DIM), F32)
    zeros = jnp.zeros((s, ATT_HEAD_DIM - ROT_DIM), F32)
    zh = jnp.zeros((s, half), F32)
    c = jnp.concatenate([cos, cos, ones], axis=1)
    sa = jnp.concatenate([-sin, zh, zeros], axis=1)
    sb = jnp.concatenate([zh, sin, zeros], axis=1)
    rep = LANES // ATT_HEAD_DIM
    return tuple(jnp.tile(t, (1, rep)) for t in (c, sa, sb))


def _attn_kernel(q_ref, kc_ref, vc_ref, kp_ref, vp_ref, num_ref, st_ref):
    nb = pl.program_id(2)
    blk = q_ref.shape[0]
    qi = lax.broadcasted_iota(jnp.int32, (blk, blk), 0)
    kj = lax.broadcasted_iota(jnp.int32, (blk, blk), 1)
    mask_c = kj <= qi
    mask_p = jnp.logical_and(kj >= qi, nb > 0)
    lane = lax.broadcasted_iota(jnp.int32, (blk, LANES), 1)
    st = jnp.zeros((blk, LANES), F32)
    nt = (((1,), (1,)), ((), ()))
    for h in range(ATT_HEADS):
        sl = slice(h * ATT_HEAD_DIM, (h + 1) * ATT_HEAD_DIM)
        q = q_ref[:, sl]
        s_c = lax.dot_general(q, kc_ref[:, sl], nt, preferred_element_type=F32)
        s_p = lax.dot_general(q, kp_ref[:, sl], nt, preferred_element_type=F32)
        s_c = jnp.where(mask_c, s_c, NEG_INF)
        s_p = jnp.where(mask_p, s_p, NEG_INF)
        m = jnp.maximum(jnp.max(s_c, axis=-1, keepdims=True), jnp.max(s_p, axis=-1, keepdims=True))
        p_c = jnp.exp(s_c - m)
        p_p = jnp.exp(s_p - m)
        l = jnp.sum(p_p, axis=-1, keepdims=True) + jnp.sum(p_c, axis=-1, keepdims=True)
        num = (jnp.dot(p_p.astype(BF16), vp_ref[:, sl], preferred_element_type=F32)
               + jnp.dot(p_c.astype(BF16), vc_ref[:, sl], preferred_element_type=F32))
        num_ref[:, sl] = num
        st = jnp.where(lane == h, m, st)
        st = jnp.where(lane == ATT_HEADS + h, l, st)
    st_ref[...] = st


def attn_prompt(qkv):
    _, b, d, l, w = qkv.shape
    nblk = l // ATT_BLOCK

    def spec(kind, prev):
        if prev:
            return pl.BlockSpec((None, None, None, ATT_BLOCK, w),
                                lambda bi, r, n: (kind, bi, r, jnp.maximum(n - 1, 0), 0))
        return pl.BlockSpec((None, None, None, ATT_BLOCK, w), lambda bi, r, n: (kind, bi, r, n, 0))

    return pl.pallas_call(
        _attn_kernel, grid=(b, d, nblk),
        in_specs=[spec(0, False), spec(1, False), spec(2, False), spec(1, True), spec(2, True)],
        out_specs=[pl.BlockSpec((None, None, ATT_BLOCK, w), lambda bi, r, n: (bi, r, n, 0)),
                   pl.BlockSpec((None, None, ATT_BLOCK, LANES), lambda bi, r, n: (bi, r, n, 0))],
        out_shape=[jax.ShapeDtypeStruct((b, d, l, w), F32), jax.ShapeDtypeStruct((b, d, l, LANES), F32)],
        compiler_params=_cparams(3), name=f"attn_prompt_d{d}")(qkv, qkv, qkv, qkv, qkv)


def _col(mat, h, n_lanes=LANES):
    return jnp.broadcast_to(mat[:, h:h + 1], (mat.shape[0], n_lanes))


def _ssd_kernel(xbc_ref, z_ref, dt_ref, cw_ref, cb_ref, dtb_ref, alog_ref, dsk_ref, nw_ref,
                y_ref, hout_ref,
                cbuf, xs_s, xw_s, eac_s, cd_s, y_s, ht_s, *, d_inner, n_groups):
    c = pl.program_id(1)
    nc = pl.num_programs(1)
    q = SSM_CHUNK
    n_st = SSM_STATE
    gw = d_inner // n_groups
    conv_dim = xbc_ref.shape[1]
    top = 8

    @pl.when(c == 0)
    def _():
        cbuf[0:top, :] = jnp.zeros((top, conv_dim), F32)
        ht_s[...] = jnp.zeros(ht_s.shape, F32)

    cbuf[top:top + q, :] = xbc_ref[...].astype(F32)
    cwid = 512
    for j in range(conv_dim // cwid):
        cs = slice(j * cwid, (j + 1) * cwid)
        acc = cb_ref[:, cs] + cbuf[top - 3:top - 3 + q, cs] * cw_ref[0:1, cs]
        for i in range(1, SSM_CONV):
            acc = acc + cbuf[top - 3 + i:top - 3 + i + q, cs] * cw_ref[i:i + 1, cs]
        xs_s[:, cs] = _silu(acc)
    cbuf[top - 3:top, :] = cbuf[top + q - 3:top + q, :]

    dt = jax.nn.softplus(dt_ref[...] + dtb_ref[...])
    a_row = -jnp.exp(alog_ref[...])
    a = dt * a_row
    ri = lax.broadcasted_iota(jnp.int32, (q, q), 0)
    ci = lax.broadcasted_iota(jnp.int32, (q, q), 1)
    causal = ri >= ci
    tril = jnp.where(causal, 1.0, 0.0).astype(F32)
    acum = jnp.dot(tril, a, preferred_element_type=F32, precision=lax.Precision.HIGHEST)
    acum_t = acum.T
    lane = lax.broadcasted_iota(jnp.int32, (q, LANES), 1)
    lo = lane < SSM_HEAD_DIM

    hpg = gw // SSM_HEAD_DIM
    for g in range(n_groups):
        bc = xs_s[:, d_inner + g * n_st:d_inner + (g + 1) * n_st].astype(BF16)
        cc = xs_s[:, d_inner + n_groups * n_st + g * n_st:d_inner + n_groups * n_st + (g + 1) * n_st].astype(BF16)
        cb = lax.dot_general(cc, bc, (((1,), (1,)), ((), ())), preferred_element_type=F32)
        for jp in range(hpg // 2):
            h0 = g * hpg + 2 * jp
            ls = slice(g * gw + jp * LANES, g * gw + (jp + 1) * LANES)
            ac0 = _col(acum, h0)
            ac1 = _col(acum, h0 + 1)
            acum_e = jnp.where(lo, ac0, ac1)
            dt_e = jnp.where(lo, _col(dt, h0), _col(dt, h0 + 1))
            xdt = xs_s[:, ls] * dt_e
            acl_e = acum_e[q - 1:q, :]
            xw_s[:, ls] = (xdt * jnp.exp(acl_e - acum_e)).astype(BF16)
            eac_s[:, ls] = jnp.exp(acum_e)
            cd_s[:, ls] = jnp.exp(acl_e)
            xdt_b = xdt.astype(BF16)
            zero = jnp.zeros_like(xdt_b)
            m0 = (jnp.exp(jnp.where(causal, ac0 - acum_t[h0:h0 + 1, :], NEG_INF)) * cb).astype(BF16)
            m1 = (jnp.exp(jnp.where(causal, ac1 - acum_t[h0 + 1:h0 + 2, :], NEG_INF)) * cb).astype(BF16)
            y_s[:, ls] = (jnp.dot(m0, jnp.where(lo, xdt_b, zero), preferred_element_type=F32)
                          + jnp.dot(m1, jnp.where(lo, zero, xdt_b), preferred_element_type=F32))
        gs = slice(g * gw, (g + 1) * gw)
        h_prev = ht_s[g]
        y_off = jnp.dot(cc, h_prev.astype(BF16), preferred_element_type=F32) * eac_s[:, gs]
        y_s[:, gs] = y_s[:, gs] + y_off
        st = lax.dot_general(bc, xw_s[:, gs], (((0,), (0,)), ((), ())), preferred_element_type=F32)
        ht_s[g] = cd_s[:, gs] * h_prev + st

    for g in range(n_groups):
        gs = slice(g * gw, (g + 1) * gw)
        y = y_s[:, gs] + dsk_ref[:, gs] * xs_s[:, gs]
        yf = y * _silu(z_ref[:, gs].astype(F32))
        yf = yf * lax.rsqrt(jnp.mean(yf * yf, axis=-1, keepdims=True) + SSM_NORM_EPS)
        y_ref[:, gs] = (yf * nw_ref[:, gs]).astype(y_ref.dtype)

    @pl.when(c == nc - 1)
    def _():
        for g in range(n_groups):
            hout_ref[g * gw:(g + 1) * gw, :] = ht_s[g].T


def ssd_prompt(xbc, z, dt_raw, conv_w, conv_b, dtb, alog, dsk_e, nw, batch, seq):
    n, conv_dim = xbc.shape
    d_inner = z.shape[1]
    n_groups = SSM_GROUPS
    gw = d_inner // n_groups
    q = SSM_CHUNK
    cps = seq // q
    row = lambda w: pl.BlockSpec((1, w), lambda b, c: (0, 0))
    tok = lambda w: pl.BlockSpec((q, w), lambda b, c: (b * cps + c, 0))
    return pl.pallas_call(
        functools.partial(_ssd_kernel, d_inner=d_inner, n_groups=n_groups),
        grid=(batch, cps),
        in_specs=[tok(conv_dim), tok(d_inner), tok(LANES),
                  pl.BlockSpec((SSM_CONV, conv_dim), lambda b, c: (0, 0)), row(conv_dim),
                  row(LANES), row(LANES), row(d_inner), row(d_inner)],
        out_specs=[tok(d_inner), pl.BlockSpec((None, d_inner, SSM_STATE), lambda b, c: (b, 0, 0))],
        out_shape=[jax.ShapeDtypeStruct((n, d_inner), BF16),
                   jax.ShapeDtypeStruct((batch, d_inner, SSM_STATE), F32)],
        scratch_shapes=[pltpu.VMEM((8 + q, conv_dim), F32),
                        pltpu.VMEM((q, conv_dim), F32),
                        pltpu.VMEM((q, d_inner), BF16),
                        pltpu.VMEM((q, d_inner), F32),
                        pltpu.VMEM((1, d_inner), F32),
                        pltpu.VMEM((q, d_inner), F32),
                        pltpu.VMEM((n_groups, SSM_STATE, gw), F32)],
        compiler_params=_cparams(2), name="ssd_prompt")(xbc, z, dt_raw, conv_w, conv_b, dtb, alog, dsk_e, nw)


TPAD = 8


def _ssd_step_kernel(xbc_ref, z_ref, dt_ref, cst_ref, h0_ref, cw_ref, cb_ref, dtb_ref, alog_ref,
                     dsk_ref, nw_ref, exp_ref, y_ref, hout_ref,
                     cbuf, xs_s, tin_s, tt_s, ycol_s, *, d_inner, n_groups, n_tok):
    b = pl.program_id(0)
    n_st = SSM_STATE
    gw = d_inner // n_groups
    conv_dim = xbc_ref.shape[1]
    top = 8
    hist = SSM_CONV - 1

    @pl.when(b == 0)
    def _():
        tin_s[...] = jnp.zeros(tin_s.shape, F32)
        cbuf[0:top, :] = jnp.zeros((top, conv_dim), F32)

    cbuf[top - hist:top, :] = cst_ref[...]
    cbuf[top:top + TPAD, :] = xbc_ref[...]
    acc = cb_ref[...] + cbuf[top - hist:top - hist + TPAD, :] * cw_ref[0:1, :]
    for i in range(1, SSM_CONV):
        acc = acc + cbuf[top - hist + i:top - hist + i + TPAD, :] * cw_ref[i:i + 1, :]
    xs_s[...] = _silu(acc)

    dt = jax.nn.softplus(dt_ref[...] + dtb_ref[...])
    da = jnp.exp(dt * (-jnp.exp(alog_ref[...])))
    hi = lax.Precision.HIGHEST
    dt_e = jnp.dot(dt, exp_ref[...], preferred_element_type=F32, precision=hi)
    da_e = jnp.dot(da, exp_ref[...], preferred_element_type=F32, precision=hi)
    tin_s[0:TPAD, :] = xs_s[:, 0:d_inner] * dt_e
    tin_s[TPAD:2 * TPAD, :] = da_e
    tt_s[...] = tin_s[...].T

    rows = 256
    lane = lax.broadcasted_iota(jnp.int32, (rows, LANES), 1)
    for ck in range(d_inner // rows):
        g = (ck * rows) // gw
        rs = slice(ck * rows, (ck + 1) * rows)
        h = h0_ref[rs, :]
        yc = jnp.zeros((rows, LANES), F32)
        for t in range(n_tok):
            b_row = xs_s[t:t + 1, d_inner + g * n_st:d_inner + (g + 1) * n_st]
            c_row = xs_s[t:t + 1, d_inner + (n_groups + g) * n_st:d_inner + (n_groups + g + 1) * n_st]
            h = tt_s[rs, TPAD + t:TPAD + t + 1] * h + tt_s[rs, t:t + 1] * b_row
            yt = jnp.sum(h * c_row, axis=-1, keepdims=True)
            yc = jnp.where(lane == t, yt, yc)
        hout_ref[rs, :] = h
        ycol_s[rs, :] = yc

    y_rows = ycol_s[...].T
    for g in range(n_groups):
        gs = slice(g * gw, (g + 1) * gw)
        y = y_rows[0:TPAD, gs] + dsk_ref[:, gs] * xs_s[:, gs]
        yf = y * _silu(z_ref[:, gs])
        yf = yf * lax.rsqrt(jnp.mean(yf * yf, axis=-1, keepdims=True) + SSM_NORM_EPS)
        y_ref[:, gs] = yf * nw_ref[:, gs]


def ssd_sample(xbc, z, dt_raw, conv_state, h0, conv_w, conv_b, dtb, alog, dsk_e, nw, expand, n_tok):
    bsz, _, conv_dim = xbc.shape
    d_inner = z.shape[2]
    row = lambda w: pl.BlockSpec((1, w), lambda b: (0, 0))
    tok = lambda w: pl.BlockSpec((None, TPAD, w), lambda b: (b, 0, 0))
    st = pl.BlockSpec((None, d_inner, SSM_STATE), lambda b: (b, 0, 0))
    return pl.pallas_call(
        functools.partial(_ssd_step_kernel, d_inner=d_inner, n_groups=SSM_GROUPS, n_tok=n_tok),
        grid=(bsz,),
        in_specs=[tok(conv_dim), tok(d_inner), tok(LANES),
                  pl.BlockSpec((None, SSM_CONV - 1, conv_dim), lambda b: (b, 0, 0)), st,
                  pl.BlockSpec((SSM_CONV, conv_dim), lambda b: (0, 0)), row(conv_dim),
                  row(LANES), row(LANES), row(d_inner), row(d_inner),
                  pl.BlockSpec((LANES, d_inner), lambda b: (0, 0))],
        out_specs=[tok(d_inner), st],
        out_shape=[jax.ShapeDtypeStruct((bsz, TPAD, d_inner), F32),
                   jax.ShapeDtypeStruct((bsz, d_inner, SSM_STATE), F32)],
        scratch_shapes=[pltpu.VMEM((8 + TPAD, conv_dim), F32),
                        pltpu.VMEM((TPAD, conv_dim), F32),
                        pltpu.VMEM((LANES, d_inner), F32),
                        pltpu.VMEM((d_inner, LANES), F32),
                        pltpu.VMEM((d_inner, LANES), F32)],
        compiler_params=_cparams(1), name="ssd_sample")(
            xbc, z, dt_raw, conv_state, h0, conv_w, conv_b, dtb, alog, dsk_e, nw, expand)


HEADS_PER_STEP = 4


def _cache_attn_kernel(new_ref, c0_ref, c1_ref, c2_ref, o_ref, n0_ref, n1_ref, n2_ref, stage, tts, *, n_tok):
    first = jnp.logical_and(pl.program_id(0) == 0, pl.program_id(1) == 0)

    @pl.when(first)
    def _():
        stage[...] = jnp.zeros(stage.shape, F32)

    crefs = (c0_ref, c1_ref, c2_ref)
    orefs = (n0_ref, n1_ref, n2_ref)
    lane = lax.broadcasted_iota(jnp.int32, (ATT_HEAD_DIM, LANES), 1)
    hd = ATT_HEAD_DIM
    for pair in range(HEADS_PER_STEP // 2):
        ps = slice(pair * LANES, (pair + 1) * LANES)
        tt = []
        for g in range(3):
            for kind in range(3):
                stage[0:TPAD, :] = new_ref[g, kind, :, ps]
                tts[3 * g + kind] = stage[...].T
            tt.append([tts.at[3 * g + kind] for kind in range(3)])
        otile = []
        for hh in range(2):
            head = pair * 2 + hh
            hs = slice(hh * hd, (hh + 1) * hd)
            ocol = jnp.zeros((hd, LANES), F32)
            for t in range(n_tok):
                parts = []
                for g, d in enumerate(ATT_DILATIONS):
                    w = crefs[g].shape[-1]
                    qc = tt[g][0][hs, t:t + 1]
                    kt = crefs[g][0, head]
                    r = lax.broadcasted_iota(jnp.int32, (1, w), 1)
                    s = jnp.sum(kt * qc, axis=0, keepdims=True)
                    valid = (r >= t) if d == 1 else ((r & (d - 1)) == t)
                    s = jnp.where(valid, s, NEG_INF)
                    news = range(t + 1) if d == 1 else (t,)
                    s_new = [jnp.sum(tt[g][1][hs, u:u + 1] * qc, axis=0, keepdims=True) for u in news]
                    m = jnp.max(s, axis=1, keepdims=True)
                    for sn in s_new:
                        m = jnp.maximum(m, sn)
                    p = jnp.exp(s - m)
                    l = jnp.sum(p, axis=1, keepdims=True)
                    num = jnp.sum(crefs[g][1, head] * p, axis=1, keepdims=True)
                    for sn, u in zip(s_new, news):
                        pn = jnp.exp(sn - m)
                        l = l + pn
                        num = num + pn * tt[g][2][hs, u:u + 1]
                    parts.append((num, m, l))
                mx = jnp.maximum(jnp.maximum(parts[0][1], parts[1][1]), parts[2][1])
                wg = [jnp.exp(p_[1] - mx) for p_ in parts]
                num = wg[0] * parts[0][0]
                for g in range(1, 3):
                    num = num + wg[g] * parts[g][0]
                den = wg[0] * parts[0][2] + wg[1] * parts[1][2] + wg[2] * parts[2][2]
                ocol = jnp.where(lane == t, num / den, ocol)
            otile.append(ocol)
            for g in range(3):
                w = crefs[g].shape[-1]
                n_col = w // LANES
                for kv in range(2):
                    cur = pltpu.roll(crefs[g][kv, head, :, 0:LANES], LANES - n_tok, 1)
                    for j in range(n_col):
                        if j + 1 < n_col:
                            nxt = pltpu.roll(crefs[g][kv, head, :, (j + 1) * LANES:(j + 2) * LANES],
                                             LANES - n_tok, 1)
                        else:
                            nxt = pltpu.roll(tt[g][1 + kv][hs, :], LANES - n_tok, 1)
                        orefs[g][kv, head, :, j * LANES:(j + 1) * LANES] = jnp.where(
                            lane < LANES - n_tok, cur, nxt)
                        cur = nxt
        o_ref[:, ps] = jnp.concatenate(otile, axis=0).T[0:TPAD, :]


def cache_attn(new, caches, n_tok):
    bsz = new.shape[2]
    hps = HEADS_PER_STEP
    cspec = lambda c: pl.BlockSpec((None, 2, hps, ATT_HEAD_DIM, c.shape[-1]), lambda b, h: (b, 0, h, 0, 0))
    return pl.pallas_call(
        functools.partial(_cache_attn_kernel, n_tok=n_tok), grid=(bsz, ATT_HEADS // hps),
        in_specs=[pl.BlockSpec((3, 3, None, TPAD, hps * ATT_HEAD_DIM), lambda b, h: (0, 0, b, 0, h))]
        + [cspec(c) for c in caches],
        out_specs=[pl.BlockSpec((None, TPAD, hps * ATT_HEAD_DIM), lambda b, h: (b, 0, h))]
        + [cspec(c) for c in caches],
        out_shape=[jax.ShapeDtypeStruct((bsz, TPAD, ATT_WIDTH), F32)]
        + [jax.ShapeDtypeStruct(c.shape, c.dtype) for c in caches],
        scratch_shapes=[pltpu.VMEM((LANES, LANES), F32), pltpu.VMEM((9, LANES, LANES), F32)],
        compiler_params=_cparams(2), name="cache_attn")(new, *caches)


def _router(logits):
    lanef = lax.broadcasted_iota(jnp.int32, logits.shape, 1).astype(F32)
    big = 1e9
    lc = jnp.where(lanef < MOE_GROUPS, logits, NEG_INF)
    mc = jnp.max(lc, axis=-1, keepdims=True)
    g_sel = jnp.min(jnp.where(lc == mc, lanef, big), axis=-1, keepdims=True)
    p_sel = 1.0 / jnp.sum(jnp.exp(lc - mc), axis=-1, keepdims=True)
    base = MOE_GROUPS + MOE_PER_GROUP * g_sel
    lf = jnp.where(jnp.logical_and(lanef >= base, lanef < base + MOE_PER_GROUP), logits, NEG_INF)
    v1 = jnp.max(lf, axis=-1, keepdims=True)
    i1 = jnp.min(jnp.where(lf == v1, lanef, big), axis=-1, keepdims=True)
    lf2 = jnp.where(lanef == i1, NEG_INF, lf)
    v2 = jnp.max(lf2, axis=-1, keepdims=True)
    i2 = jnp.min(jnp.where(lf2 == v2, lanef, big), axis=-1, keepdims=True)
    e2 = jnp.exp(v2 - v1)
    den = 1.0 + e2
    w1 = (1.0 / den) * p_sel
    w2 = (e2 / den) * p_sel
    return jnp.where(lanef == i1, w1, 0.0) + jnp.where(lanef == i2, w2, 0.0)


def _outproj_kernel(*refs, dils, tm):
    n_g = len(dils)
    if n_g:
        x_ref, g_ref, ys_ref = refs[:3]
        att_refs = refs[3:3 + 2 * n_g]
        rest = refs[3 + 2 * n_g:]
    else:
        x_ref, g_ref, ys_ref, attn_ref = refs[:4]
        rest = refs[4:]
    wa_ref, ws_ref, wo_ref, n2_ref, wr_ref, br_ref, x2_ref, xn_ref, cmb_ref = rest[:9]
    scr = rest[9:]

    if n_g:
        n_ch = ATT_WIDTH // LANES
        nums, stats = [], []
        for g, d in enumerate(dils):
            num_ref, st_ref = att_refs[2 * g], att_refs[2 * g + 1]
            if d == 1:
                nums.append([num_ref[0, :, ch * LANES:(ch + 1) * LANES] for ch in range(n_ch)])
                stats.append(st_ref[0])
            else:
                ns, ss = scr[2 * g], scr[2 * g + 1]
                for r in range(d):
                    for ch in range(n_ch):
                        ns[ch, pl.ds(r, tm // d, stride=d), :] = num_ref[r, :, ch * LANES:(ch + 1) * LANES]
                    ss[pl.ds(r, tm // d, stride=d), :] = st_ref[r]
                nums.append([ns[ch] for ch in range(n_ch)])
                stats.append(ss[...])
        lane = lax.broadcasted_iota(jnp.int32, (tm, LANES), 1)
        lo = lane < ATT_HEAD_DIM
        chunks = []
        for ch in range(ATT_WIDTH // LANES):
            wts, dens = [], []
            for h in (2 * ch, 2 * ch + 1):
                ms = [s[:, h:h + 1] for s in stats]
                ls = [s[:, ATT_HEADS + h:ATT_HEADS + h + 1] for s in stats]
                mx = ms[0]
                for m in ms[1:]:
                    mx = jnp.maximum(mx, m)
                w = [jnp.exp(m - mx) for m in ms]
                den = w[0] * ls[0]
                for g in range(1, n_g):
                    den = den + w[g] * ls[g]
                wts.append(w)
                dens.append(den)
            num = jnp.where(lo, wts[0][0], wts[1][0]) * nums[0][ch]
            for g in range(1, n_g):
                num = num + jnp.where(lo, wts[0][g], wts[1][g]) * nums[g][ch]
            chunks.append((num / jnp.where(lo, dens[0], dens[1])).astype(BF16))
        attn = jnp.concatenate(chunks, axis=1)
    else:
        attn = attn_ref[...].astype(BF16)

    dm = x_ref.shape[1]
    a = jnp.dot(attn, wa_ref[...], preferred_element_type=F32)
    s = jnp.dot(ys_ref[...].astype(BF16), ws_ref[...], preferred_element_type=F32)
    gates = g_ref[...].astype(F32)
    mixed = _sigmoid(gates[:, :dm]) * a + _sigmoid(gates[:, dm:]) * s
    x2 = x_ref[...] + jnp.dot(mixed.astype(BF16), wo_ref[...], preferred_element_type=F32)
    x2_ref[...] = x2
    xn = (x2 * lax.rsqrt(jnp.mean(x2 * x2, axis=-1, keepdims=True) + RMS_EPS) * n2_ref[...]).astype(BF16)
    xn_ref[...] = xn
    logits = jnp.dot(xn, wr_ref[...], preferred_element_type=F32) + br_ref[...]
    cmb_ref[...] = _router(logits)


def outproj(x, gates, yssm, att, dils, batch, seq, weights, tm):
    n, dm = x.shape
    wa, ws, wo, n2, wr, br = weights
    tpb = seq // tm
    tok = lambda w: pl.BlockSpec((tm, w), lambda i: (i, 0))
    full = lambda a: pl.BlockSpec(a.shape, lambda i: (0,) * a.ndim)
    in_specs = [tok(dm), tok(gates.shape[1]), tok(yssm.shape[1])]
    args = [x, gates, yssm]
    scratch = []
    if dils:
        for (num, st), d in zip(att, dils):
            for arr in (num, st):
                in_specs.append(pl.BlockSpec((None, d, tm // d, arr.shape[-1]),
                                             lambda i: (i // tpb, 0, i % tpb, 0)))
                args.append(arr)
                scratch.append(pltpu.VMEM((tm, LANES) if arr.shape[-1] == LANES
                                          else (arr.shape[-1] // LANES, tm, LANES), F32))
    else:
        in_specs.append(tok(att.shape[1]))
        args.append(att)
    in_specs += [full(wa), full(ws), full(wo), full(n2), full(wr), full(br)]
    args += [wa, ws, wo, n2, wr, br]
    return pl.pallas_call(
        functools.partial(_outproj_kernel, dils=tuple(dils), tm=tm), grid=(n // tm,),
        in_specs=in_specs,
        out_specs=[tok(dm), tok(dm), tok(LANES)],
        out_shape=[jax.ShapeDtypeStruct((n, dm), F32), jax.ShapeDtypeStruct((n, dm), BF16),
                   jax.ShapeDtypeStruct((n, LANES), F32)],
        scratch_shapes=scratch,
        compiler_params=_cparams(1), name="outproj")(*args)


def _moe_kernel(xn_ref, cmb_ref, x2_ref, wg_ref, wu_ref, wd_ref, nf_ref, o_ref, acc_ref, *, final_norm):
    e = pl.program_id(1)
    ne = pl.num_programs(1)

    @pl.when(e == 0)
    def _():
        acc_ref[...] = jnp.zeros(acc_ref.shape, F32)

    xn = xn_ref[...]
    he = _silu(jnp.dot(xn, wg_ref[...], preferred_element_type=F32)) * jnp.dot(
        xn, wu_ref[...], preferred_element_type=F32)
    ye = jnp.dot(he.astype(BF16), wd_ref[...], preferred_element_type=F32)
    cmb = cmb_ref[...]
    lane = lax.broadcasted_iota(jnp.int32, cmb.shape, 1)
    ce = jnp.sum(jnp.where(lane == e + MOE_GROUPS, cmb, 0.0), axis=-1, keepdims=True)
    acc_ref[...] += ce * ye

    @pl.when(e == ne - 1)
    def _():
        xf = x2_ref[...] + acc_ref[...]
        if final_norm:
            xf = xf * lax.rsqrt(jnp.mean(xf * xf, axis=-1, keepdims=True) + RMS_EPS) * nf_ref[...]
        o_ref[...] = xf


def moe_final(xn, cmb, x2, wg, wu, wd, nf, tm, final_norm):
    n, dm = xn.shape
    ne, _, ff = wg.shape
    tok = lambda w: pl.BlockSpec((tm, w), lambda i, e: (i, 0))
    return pl.pallas_call(
        functools.partial(_moe_kernel, final_norm=final_norm), grid=(n // tm, ne),
        in_specs=[tok(dm), tok(LANES), tok(dm),
                  pl.BlockSpec((None, dm, ff), lambda i, e: (e, 0, 0)),
                  pl.BlockSpec((None, dm, ff), lambda i, e: (e, 0, 0)),
                  pl.BlockSpec((None, ff, dm), lambda i, e: (e, 0, 0)),
                  pl.BlockSpec((1, dm), lambda i, e: (0, 0))],
        out_specs=tok(dm),
        out_shape=jax.ShapeDtypeStruct((n, dm), F32),
        scratch_shapes=[pltpu.VMEM((tm, dm), F32)],
        compiler_params=_cparams(2), name="moe_final")(xn, cmb, x2, wg, wu, wd, nf)


def _pad_lanes(v, width=LANES):
    return jnp.pad(v, ((0, 0), (0, width - v.shape[1])))


def _natural_rows(arr, n_rows):
    b, d, l, w = arr.shape
    tail = arr[:, :, l - n_rows // d:, :]
    return jnp.swapaxes(tail, 1, 2).reshape(b, n_rows, w)


def kernel(x_prompt, x_sample, cache_kv_w128, cache_kv_w512, cache_kv_w2048, state_ssm, state_conv,
           norm1, w_in, conv_w, conv_b, dt_bias, a_log, d_skip, ssm_norm, w_att_out, w_ssm_out, w_o,
           norm2, w_router_coarse, b_router_coarse, w_router_fine, b_router_fine,
           w_exp_gate, w_exp_up, w_exp_down, norm_f):
    bp, sp, dm = x_prompt.shape
    bs, ts, _ = x_sample.shape
    depth = w_in.shape[0]
    n_heads = dt_bias.shape[1]
    d_inner = n_heads * SSM_HEAD_DIM
    conv_dim = conv_w.shape[2]
    n_grp = len(ATT_DILATIONS)
    off_z = n_grp * 3 * ATT_WIDTH
    off_xbc = off_z + d_inner
    off_dt = off_xbc + conv_dim
    off_gate = off_dt + n_heads
    caches = (cache_kv_w128, cache_kv_w512, cache_kv_w2048)
    for g in range(n_grp):
        assert caches[g].shape[2] == ATT_WINDOWS[g] and sp % (ATT_DILATIONS[g] * ATT_BLOCK) == 0
    assert ts <= min(ATT_DILATIONS[1:]) and ts <= TPAD and sp % SSM_CHUNK == 0

    tabs_p = rope_tables(jnp.arange(sp, dtype=F32))
    tabs_s = rope_tables(jnp.tile(jnp.arange(ts, dtype=F32) + PAST_LEN, bs))
    expand = (jnp.arange(LANES)[:, None] == (jnp.arange(d_inner) // SSM_HEAD_DIM)[None, :]).astype(F32)

    xp = x_prompt.reshape(bp * sp, dm)
    xs = x_sample.reshape(bs * ts, dm)
    outs = {k: [] for k in ("kvp0", "kvp1", "kvp2", "ssm_p", "conv_p", "kvs0", "kvs1", "kvs2", "ssm_s", "conv_s")}
    n_s = bs * ts
    for layer in range(depth):
        w = w_in[layer].astype(BF16)
        w_qkv = [w[:, g * 3 * ATT_WIDTH:(g + 1) * 3 * ATT_WIDTH] for g in range(n_grp)]
        w_z, w_xbc, w_gate = w[:, off_z:off_xbc], w[:, off_xbc:off_dt], w[:, off_gate:]
        w_dt = _pad_lanes(w[:, off_dt:off_gate])
        dtb = _pad_lanes(dt_bias[layer][None])
        alog = _pad_lanes(a_log[layer][None])
        dsk_e = jnp.repeat(d_skip[layer], SSM_HEAD_DIM)[None]
        nw = ssm_norm[layer][None]
        cw, cb = conv_w[layer], conv_b[layer][None]
        w_router = _pad_lanes(jnp.concatenate([w_router_coarse[layer], w_router_fine[layer]], axis=1)).astype(BF16)
        b_router = _pad_lanes(jnp.concatenate([b_router_coarse[layer], b_router_fine[layer]])[None])
        wts = (w_att_out[layer].astype(BF16), w_ssm_out[layer].astype(BF16), w_o[layer].astype(BF16),
               norm2[layer][None], w_router, b_router)
        wg, wu, wd = (w_exp_gate[layer].astype(BF16), w_exp_up[layer].astype(BF16),
                      w_exp_down[layer].astype(BF16))

        xn = rmsnorm_bf16(xp, norm1[layer], 1024)
        att = []
        for g, d in enumerate(ATT_DILATIONS):
            qkv = qkv_proj(xn, w_qkv[g], tabs_p, bp, sp, d, 1024, BF16, f"qkv_prompt_d{d}")
            att.append(attn_prompt(qkv))
            wnd = min(ATT_WINDOWS[g], sp)
            kv = jnp.stack([_natural_rows(qkv[1], wnd), _natural_rows(qkv[2], wnd)], axis=2)
            outs[f"kvp{g}"].append(kv.astype(F32).reshape(bp, wnd, 2, ATT_HEADS, ATT_HEAD_DIM))
        z = matmul(xn, w_z, BF16, 1024, 512, "proj_z")
        xbc = matmul(xn, w_xbc, BF16, 1024, 512, "proj_xbc")
        gates = matmul(xn, w_gate, BF16, 1024, 512, "proj_gates")
        dt_raw = matmul(xn, w_dt, F32, 1024, LANES, "proj_dt")
        y_ssm, h_p = ssd_prompt(xbc, z, dt_raw, cw, cb, dtb, alog, dsk_e, nw, bp, sp)
        outs["ssm_p"].append(h_p.reshape(bp, n_heads, SSM_HEAD_DIM, SSM_STATE))
        outs["conv_p"].append(xbc.reshape(bp, sp, conv_dim)[:, sp - (SSM_CONV - 1):].astype(F32))
        x2, xn2, cmb = outproj(xp, gates, y_ssm, att, ATT_DILATIONS, bp, sp, wts, 256)
        xp = moe_final(xn2, cmb, x2, wg, wu, wd, norm_f[None], 512, layer == depth - 1)

        xn = rmsnorm_bf16(xs, norm1[layer], n_s)
        pad_t = lambda a: jnp.pad(a.reshape(bs, ts, a.shape[-1]), ((0, 0), (0, TPAD - ts), (0, 0)))
        new = jnp.stack([qkv_proj(xn, w_qkv[g], tabs_s, 1, n_s, 1, n_s, F32, f"qkv_sample_{g}")
                         .reshape(3, bs, ts, ATT_WIDTH) for g in range(n_grp)])
        new = jnp.pad(new, ((0, 0), (0, 0), (0, 0), (0, TPAD - ts), (0, 0)))
        cl = [jnp.transpose(caches[g][layer], (0, 2, 3, 4, 1)) for g in range(n_grp)]
        res = cache_attn(new, cl, ts)
        attn_s = res[0][:, :ts]
        for g in range(n_grp):
            outs[f"kvs{g}"].append(jnp.transpose(res[1 + g], (0, 4, 1, 2, 3)))
        z = matmul(xn, w_z, F32, n_s, 512, "proj_z_s")
        xbc = matmul(xn, w_xbc, F32, n_s, 512, "proj_xbc_s")
        gates = matmul(xn, w_gate, F32, n_s, 512, "proj_gates_s")
        dt_raw = matmul(xn, w_dt, F32, n_s, LANES, "proj_dt_s")
        y_s, h_s = ssd_sample(pad_t(xbc), pad_t(z), pad_t(dt_raw), state_conv[layer],
                              state_ssm[layer].reshape(bs, d_inner, SSM_STATE), cw, cb, dtb, alog, dsk_e, nw,
                              expand, ts)
        outs["ssm_s"].append(h_s.reshape(bs, n_heads, SSM_HEAD_DIM, SSM_STATE))
        hist = jnp.concatenate([state_conv[layer], xbc.reshape(bs, ts, conv_dim)], axis=1)
        outs["conv_s"].append(hist[:, ts:])
        y_s = y_s[:, :ts].reshape(n_s, d_inner)
        x2, xn2, cmb = outproj(xs, gates, y_s, attn_s.reshape(n_s, ATT_WIDTH), (), 1, n_s, wts, min(256, n_s))
        xs = moe_final(xn2, cmb, x2, wg, wu, wd, norm_f[None], n_s, layer == depth - 1)

    st = lambda k: jnp.stack(outs[k])
    return (xp.reshape(bp, sp, dm), xs.reshape(bs, ts, dm),
            st("kvp0"), st("kvp1"), st("kvp2"), st("ssm_p"), st("conv_p"),
            st("kvs0"), st("kvs1"), st("kvs2"), st("ssm_s"), st("conv_s"))
```

```python
import functools
import math

import jax
import jax.numpy as jnp
from jax import lax
from jax.experimental import pallas as pl
from jax.experimental.pallas import tpu as pltpu

F32 = jnp.float32
BF16 = jnp.bfloat16

PAST_LEN = 8192
ATT_WINDOWS = (128, 512, 2048)
ATT_DILATIONS = (1, 4, 16)
ATT_HEADS = 8
ATT_HEAD_DIM = 64
ATT_WIDTH = ATT_HEADS * ATT_HEAD_DIM
ATT_SCALE = ATT_HEAD_DIM ** -0.5
ROT_DIM = ATT_HEAD_DIM // 4
ROPE_THETA = 500000.0
ATT_BLOCK = 128

SSM_HEAD_DIM = 64
SSM_STATE = 128
SSM_GROUPS = 4
SSM_CONV = 4
SSM_CHUNK = 128
MOE_GROUPS = 4
MOE_PER_GROUP = 4
MOE_EXPERTS = MOE_GROUPS * MOE_PER_GROUP
RMS_EPS = 1e-6
SSM_NORM_EPS = 1e-5

LANES = 128
VMEM_LIMIT = 56 * 1024 * 1024
NEG_INF = float("-inf")


def _cparams(n_axes):
    return pltpu.CompilerParams(dimension_semantics=("arbitrary",) * n_axes,
                                vmem_limit_bytes=VMEM_LIMIT)


def _sigmoid(x):
    return 1.0 / (1.0 + jnp.exp(-x))


def _silu(x):
    return x * _sigmoid(x)


def _norm_kernel(x_ref, g_ref, o_ref):
    x = x_ref[...]
    y = x * lax.rsqrt(jnp.mean(x * x, axis=-1, keepdims=True) + RMS_EPS)
    o_ref[...] = (y * g_ref[...]).astype(o_ref.dtype)


def rmsnorm_bf16(x, g, tm):
    n, d = x.shape
    return pl.pallas_call(
        _norm_kernel, grid=(n // tm,),
        in_specs=[pl.BlockSpec((tm, d), lambda i: (i, 0)), pl.BlockSpec((1, d), lambda i: (0, 0))],
        out_specs=pl.BlockSpec((tm, d), lambda i: (i, 0)),
        out_shape=jax.ShapeDtypeStruct((n, d), BF16),
        compiler_params=_cparams(1), name="rmsnorm")(x, g.reshape(1, d))


def _mm_kernel(x_ref, w_ref, o_ref):
    o_ref[...] = jnp.dot(x_ref[...], w_ref[...], preferred_element_type=F32).astype(o_ref.dtype)


def matmul(x, w, out_dtype, tm, tn, name):
    m, k = x.shape
    n = w.shape[1]
    return pl.pallas_call(
        _mm_kernel, grid=(m // tm, n // tn),
        in_specs=[pl.BlockSpec((tm, k), lambda i, j: (i, 0)), pl.BlockSpec((k, tn), lambda i, j: (0, j))],
        out_specs=pl.BlockSpec((tm, tn), lambda i, j: (i, j)),
        out_shape=jax.ShapeDtypeStruct((m, n), out_dtype),
        compiler_params=_cparams(2), name=name)(x, w)


QKV_CHUNK = 256


def _qkv_kernel(x_ref, w_ref, cos_ref, sa_ref, sb_ref, o_ref, acc_ref, *, d):
    sc = jnp.where(pl.program_id(1) == 0, ATT_SCALE, 1.0).astype(F32)
    tm = x_ref.shape[0]
    ck = min(QKV_CHUNK, tm)
    for c0 in range(0, tm, ck):
        rows = slice(c0, c0 + ck)
        acc = jnp.dot(x_ref[rows, :], w_ref[...], preferred_element_type=F32)
        c = cos_ref[rows, :] * sc
        sa = sa_ref[rows, :] * sc
        sb = sb_ref[rows, :] * sc
        for ch in range(ATT_WIDTH // LANES):
            cs = slice(ch * LANES, (ch + 1) * LANES)
            t = acc[:, cs]
            r = t * c + pltpu.roll(t, LANES - ROT_DIM // 2, 1) * sa + pltpu.roll(t, ROT_DIM // 2, 1) * sb
            if d == 1:
                o_ref[0, rows, cs] = r.astype(o_ref.dtype)
            else:
                acc_ref[ch, rows, :] = r
                for res in range(d):
                    o_ref[res, c0 // d:(c0 + ck) // d, cs] = acc_ref[
                        ch, pl.ds(c0 + res, ck // d, stride=d), :].astype(o_ref.dtype)


def qkv_proj(xn, w, tabs, batch, seq, d, tm, out_dtype, name):
    n, dm = xn.shape
    tpb = seq // tm
    cos, sa, sb = tabs
    tab_spec = pl.BlockSpec((None, tm, LANES), lambda i, j: (jnp.where(j == 2, 1, 0), i % tpb, 0))
    return pl.pallas_call(
        functools.partial(_qkv_kernel, d=d), grid=(n // tm, 3),
        in_specs=[pl.BlockSpec((tm, dm), lambda i, j: (i, 0)),
                  pl.BlockSpec((dm, ATT_WIDTH), lambda i, j: (0, j)),
                  tab_spec, tab_spec, tab_spec],
        out_specs=pl.BlockSpec((None, None, d, tm // d, ATT_WIDTH),
                               lambda i, j: (j, i // tpb, 0, i % tpb, 0)),
        out_shape=jax.ShapeDtypeStruct((3, batch, d, seq // d, ATT_WIDTH), out_dtype),
        scratch_shapes=[pltpu.VMEM((ATT_WIDTH // LANES, tm, LANES), F32)],
        compiler_params=_cparams(2), name=name)(xn, w, cos, sa, sb)


def rope_tables(pos):
    half = ROT_DIM // 2
    inv = ROPE_THETA ** (-jnp.arange(half, dtype=F32) / half)
    ang = pos[:, None] * inv[None, :]
    cos = jnp.cos(ang)
    sin = jnp.sin(ang)
    s = pos.shape[0]
    ones = jnp.ones((s, ATT_HEAD_DIM - ROT_DIM), F32)
    zeros = jnp.zeros((s, ATT_HEAD_DIM - ROT_DIM), F32)
    zh = jnp.zeros((s, half), F32)
    c = jnp.concatenate([cos, cos, ones], axis=1)
    sa = jnp.concatenate([-sin, zh, zeros], axis=1)
    sb = jnp.concatenate([zh, sin, zeros], axis=1)
    rep = LANES // ATT_HEAD_DIM
    c, sa, sb = (jnp.tile(t, (1, rep)) for t in (c, sa, sb))
    return (jnp.stack([c, jnp.ones_like(c)]), jnp.stack([sa, jnp.zeros_like(sa)]),
            jnp.stack([sb, jnp.zeros_like(sb)]))


ATT_ROW_CHUNK = 32


def _attn_kernel(q_ref, kc_ref, vc_ref, kp_ref, vp_ref, num_ref, st_ref, s_scr, p_scr):
    nb = pl.program_id(2)
    blk = q_ref.shape[0]
    nt = (((1,), (1,)), ((), ()))
    heads = [slice(h * ATT_HEAD_DIM, (h + 1) * ATT_HEAD_DIM) for h in range(ATT_HEADS)]
    for h, sl in enumerate(heads):
        q = q_ref[:, sl]
        s_scr[h, :, 0:blk] = lax.dot_general(q, kp_ref[:, sl], nt, preferred_element_type=F32)
        s_scr[h, :, blk:2 * blk] = lax.dot_general(q, kc_ref[:, sl], nt, preferred_element_type=F32)
    rc = ATT_ROW_CHUNK
    qi = lax.broadcasted_iota(jnp.int32, (rc, 2 * blk), 0)
    kj = lax.broadcasted_iota(jnp.int32, (rc, 2 * blk), 1)
    lane = lax.broadcasted_iota(jnp.int32, (rc, LANES), 1)
    for r0 in range(0, blk, rc):
        qa = qi + r0
        mask = jnp.logical_or(jnp.logical_and(jnp.logical_and(kj < blk, kj >= qa), nb > 0),
                              jnp.logical_and(kj >= blk, kj - blk <= qa))
        st = jnp.zeros((rc, LANES), F32)
        for h in range(ATT_HEADS):
            s = jnp.where(mask, s_scr[h, r0:r0 + rc, :], NEG_INF)
            m = jnp.max(s, axis=-1, keepdims=True)
            p = jnp.exp(s - m)
            l = jnp.sum(p, axis=-1, keepdims=True)
            p_scr[h, r0:r0 + rc, :] = p.astype(BF16)
            st = jnp.where(lane == h, m, st)
            st = jnp.where(lane == ATT_HEADS + h, l, st)
        st_ref[r0:r0 + rc, :] = st
    for h, sl in enumerate(heads):
        num_ref[:, sl] = (jnp.dot(p_scr[h, :, 0:blk], vp_ref[:, sl], preferred_element_type=F32)
                          + jnp.dot(p_scr[h, :, blk:2 * blk], vc_ref[:, sl], preferred_element_type=F32))


def attn_prompt(qkv):
    _, b, d, l, w = qkv.shape
    nblk = l // ATT_BLOCK

    def spec(kind, prev):
        if prev:
            return pl.BlockSpec((None, None, None, ATT_BLOCK, w),
                                lambda bi, r, n: (kind, bi, r, jnp.maximum(n - 1, 0), 0))
        return pl.BlockSpec((None, None, None, ATT_BLOCK, w), lambda bi, r, n: (kind, bi, r, n, 0))

    return pl.pallas_call(
        _attn_kernel, grid=(b, d, nblk),
        in_specs=[spec(0, False), spec(1, False), spec(2, False), spec(1, True), spec(2, True)],
        out_specs=[pl.BlockSpec((None, None, ATT_BLOCK, w), lambda bi, r, n: (bi, r, n, 0)),
                   pl.BlockSpec((None, None, ATT_BLOCK, LANES), lambda bi, r, n: (bi, r, n, 0))],
        out_shape=[jax.ShapeDtypeStruct((b, d, l, w), F32), jax.ShapeDtypeStruct((b, d, l, LANES), F32)],
        scratch_shapes=[pltpu.VMEM((ATT_HEADS, ATT_BLOCK, 2 * ATT_BLOCK), F32),
                        pltpu.VMEM((ATT_HEADS, ATT_BLOCK, 2 * ATT_BLOCK), BF16)],
        compiler_params=_cparams(3), name=f"attn_prompt_d{d}")(qkv, qkv, qkv, qkv, qkv)


def _col(mat, h, n_lanes=LANES):
    return jnp.broadcast_to(mat[:, h:h + 1], (mat.shape[0], n_lanes))


def _ssd_kernel(xbc_ref, z_ref, dt_ref, cw_ref, cb_ref, dtb_ref, alog_ref, dsk_ref, nw_ref,
                y_ref, hout_ref,
                cbuf, xs_s, xw_s, eac_s, cd_s, y_s, ht_s, *, d_inner, n_groups):
    c = pl.program_id(1)
    nc = pl.num_programs(1)
    q = SSM_CHUNK
    n_st = SSM_STATE
    gw = d_inner // n_groups
    conv_dim = xbc_ref.shape[1]
    top = 8

    @pl.when(c == 0)
    def _():
        cbuf[0:top, :] = jnp.zeros((top, conv_dim), F32)
        ht_s[...] = jnp.zeros(ht_s.shape, F32)

    cbuf[top:top + q, :] = xbc_ref[...].astype(F32)
    cwid = 512
    for j in range(conv_dim // cwid):
        cs = slice(j * cwid, (j + 1) * cwid)
        acc = cb_ref[:, cs] + cbuf[top - 3:top - 3 + q, cs] * cw_ref[0:1, cs]
        for i in range(1, SSM_CONV):
            acc = acc + cbuf[top - 3 + i:top - 3 + i + q, cs] * cw_ref[i:i + 1, cs]
        xs_s[:, cs] = _silu(acc)
    cbuf[top - 3:top, :] = cbuf[top + q - 3:top + q, :]

    dt = jax.nn.softplus(dt_ref[...] + dtb_ref[...])
    a_row = -jnp.exp(alog_ref[...])
    a = dt * a_row
    ri = lax.broadcasted_iota(jnp.int32, (q, q), 0)
    ci = lax.broadcasted_iota(jnp.int32, (q, q), 1)
    causal = ri >= ci
    tril = jnp.where(causal, 1.0, 0.0).astype(F32)
    acum = jnp.dot(tril, a, preferred_element_type=F32, precision=lax.Precision.HIGHEST)
    acum_t = acum.T
    lane = lax.broadcasted_iota(jnp.int32, (q, LANES), 1)
    lo = lane < SSM_HEAD_DIM

    hpg = gw // SSM_HEAD_DIM
    for g in range(n_groups):
        bc = xs_s[:, d_inner + g * n_st:d_inner + (g + 1) * n_st].astype(BF16)
        cc = xs_s[:, d_inner + n_groups * n_st + g * n_st:d_inner + n_groups * n_st + (g + 1) * n_st].astype(BF16)
        cb = lax.dot_general(cc, bc, (((1,), (1,)), ((), ())), preferred_element_type=F32)
        for jp in range(hpg // 2):
            h0 = g * hpg + 2 * jp
            ls = slice(g * gw + jp * LANES, g * gw + (jp + 1) * LANES)
            ac0 = _col(acum, h0)
            ac1 = _col(acum, h0 + 1)
            acum_e = jnp.where(lo, ac0, ac1)
            dt_e = jnp.where(lo, _col(dt, h0), _col(dt, h0 + 1))
            xdt = xs_s[:, ls] * dt_e
            acl_e = acum_e[q - 1:q, :]
            xw_s[:, ls] = (xdt * jnp.exp(acl_e - acum_e)).astype(BF16)
            eac_s[:, ls] = jnp.exp(acum_e)
            cd_s[:, ls] = jnp.exp(acl_e)
            xdt_b = xdt.astype(BF16)
            zero = jnp.zeros_like(xdt_b)
            m0 = (jnp.exp(jnp.where(causal, ac0 - acum_t[h0:h0 + 1, :], NEG_INF)) * cb).astype(BF16)
            m1 = (jnp.exp(jnp.where(causal, ac1 - acum_t[h0 + 1:h0 + 2, :], NEG_INF)) * cb).astype(BF16)
            y_s[:, ls] = (jnp.dot(m0, jnp.where(lo, xdt_b, zero), preferred_element_type=F32)
                          + jnp.dot(m1, jnp.where(lo, zero, xdt_b), preferred_element_type=F32))
        gs = slice(g * gw, (g + 1) * gw)
        h_prev = ht_s[g]
        y_off = jnp.dot(cc, h_prev.astype(BF16), preferred_element_type=F32) * eac_s[:, gs]
        y_s[:, gs] = y_s[:, gs] + y_off
        st = lax.dot_general(bc, xw_s[:, gs], (((0,), (0,)), ((), ())), preferred_element_type=F32)
        ht_s[g] = cd_s[:, gs] * h_prev + st

    for g in range(n_groups):
        gs = slice(g * gw, (g + 1) * gw)
        y = y_s[:, gs] + dsk_ref[:, gs] * xs_s[:, gs]
        yf = y * _silu(z_ref[:, gs].astype(F32))
        yf = yf * lax.rsqrt(jnp.mean(yf * yf, axis=-1, keepdims=True) + SSM_NORM_EPS)
        y_ref[:, gs] = (yf * nw_ref[:, gs]).astype(y_ref.dtype)

    @pl.when(c == nc - 1)
    def _():
        for g in range(n_groups):
            hout_ref[g * gw:(g + 1) * gw, :] = ht_s[g].T


def ssd_prompt(xbc, z, dt_raw, conv_w, conv_b, dtb, alog, dsk_e, nw, batch, seq):
    n, conv_dim = xbc.shape
    d_inner = z.shape[1]
    n_groups = SSM_GROUPS
    gw = d_inner // n_groups
    q = SSM_CHUNK
    cps = seq // q
    row = lambda w: pl.BlockSpec((1, w), lambda b, c: (0, 0))
    tok = lambda w: pl.BlockSpec((q, w), lambda b, c: (b * cps + c, 0))
    return pl.pallas_call(
        functools.partial(_ssd_kernel, d_inner=d_inner, n_groups=n_groups),
        grid=(batch, cps),
        in_specs=[tok(conv_dim), tok(d_inner), tok(LANES),
                  pl.BlockSpec((SSM_CONV, conv_dim), lambda b, c: (0, 0)), row(conv_dim),
                  row(LANES), row(LANES), row(d_inner), row(d_inner)],
        out_specs=[tok(d_inner), pl.BlockSpec((None, d_inner, SSM_STATE), lambda b, c: (b, 0, 0))],
        out_shape=[jax.ShapeDtypeStruct((n, d_inner), BF16),
                   jax.ShapeDtypeStruct((batch, d_inner, SSM_STATE), F32)],
        scratch_shapes=[pltpu.VMEM((8 + q, conv_dim), F32),
                        pltpu.VMEM((q, conv_dim), F32),
                        pltpu.VMEM((q, d_inner), BF16),
                        pltpu.VMEM((q, d_inner), F32),
                        pltpu.VMEM((1, d_inner), F32),
                        pltpu.VMEM((q, d_inner), F32),
                        pltpu.VMEM((n_groups, SSM_STATE, gw), F32)],
        compiler_params=_cparams(2), name="ssd_prompt")(xbc, z, dt_raw, conv_w, conv_b, dtb, alog, dsk_e, nw)


TPAD = 8


def _ssd_step_kernel(xbc_ref, z_ref, dt_ref, cst_ref, h0_ref, cw_ref, cb_ref, dtb_ref, alog_ref,
                     dsk_ref, nw_ref, exp_ref, y_ref, hout_ref,
                     cbuf, xs_s, f_s, *, d_inner, n_groups, n_tok):
    b = pl.program_id(0)
    n_st = SSM_STATE
    gw = d_inner // n_groups
    conv_dim = xbc_ref.shape[1]
    top = 8
    hist = SSM_CONV - 1

    @pl.when(b == 0)
    def _():
        cbuf[0:top, :] = jnp.zeros((top, conv_dim), F32)

    cbuf[top - hist:top, :] = cst_ref[...]
    cbuf[top:top + TPAD, :] = xbc_ref[...]
    acc = cb_ref[...] + cbuf[top - hist:top - hist + TPAD, :] * cw_ref[0:1, :]
    for i in range(1, SSM_CONV):
        acc = acc + cbuf[top - hist + i:top - hist + i + TPAD, :] * cw_ref[i:i + 1, :]
    xs_s[...] = _silu(acc)

    rid = lax.broadcasted_iota(jnp.int32, (TPAD, LANES), 0)
    dt = jax.nn.softplus(dt_ref[...] + dtb_ref[...])
    da = dt * (-jnp.exp(alog_ref[...]))
    cum = da
    for k in range(1, n_tok):
        cum = cum + jnp.where(rid >= k, pltpu.roll(da, k, 0), 0.0)
    facs = [dt, jnp.exp(cum)]
    for t in range(n_tok):
        facs.append(jnp.where(rid <= t, jnp.exp(cum[t:t + 1, :] - cum), 0.0))
    hi = lax.Precision.HIGHEST
    f_s[...] = jnp.dot(jnp.concatenate(facs, axis=0), exp_ref[...], preferred_element_type=F32, precision=hi)

    nt = (((1,), (1,)), ((), ()))
    row8 = lax.broadcasted_iota(jnp.int32, (TPAD, gw), 0)
    for g in range(n_groups):
        gs = slice(g * gw, (g + 1) * gw)
        xdt = xs_s[:, gs] * f_s[0:TPAD, gs]
        b_f = xs_s[:, d_inner + g * n_st:d_inner + (g + 1) * n_st]
        c_f = xs_s[:, d_inner + (n_groups + g) * n_st:d_inner + (n_groups + g + 1) * n_st]
        b16 = b_f.astype(BF16)
        c16 = c_f.astype(BF16)
        h0g = h0_ref[gs, :]
        y = f_s[TPAD:2 * TPAD, gs] * lax.dot_general(c16, h0g.astype(BF16), nt, preferred_element_type=F32)
        bc = lax.dot_general(b16, c16, nt, preferred_element_type=F32)
        for t in range(n_tok):
            term = bc[:, t:t + 1] * f_s[(2 + t) * TPAD:(3 + t) * TPAD, gs] * xdt
            y = y + jnp.where(row8 == t, jnp.sum(term, axis=0, keepdims=True), 0.0)
        yv = y + dsk_ref[:, gs] * xs_s[:, gs]
        yf = yv * _silu(z_ref[:, gs])
        yf = yf * lax.rsqrt(jnp.mean(yf * yf, axis=-1, keepdims=True) + SSM_NORM_EPS)
        y_ref[:, gs] = yf * nw_ref[:, gs]

        dend = f_s[(1 + n_tok) * TPAD:(2 + n_tok) * TPAD, gs]
        pend = f_s[TPAD + n_tok - 1:TPAD + n_tok, gs]
        lhs = jnp.where(row8 == n_tok, pend, dend * xdt)
        rhs = jnp.concatenate([jnp.where(rid < n_tok, b_f, 0.0), jnp.where(rid == n_tok, 1.0, 0.0)], axis=1)
        res = lax.dot_general(lhs, rhs, (((0,), (0,)), ((), ())), preferred_element_type=F32, precision=hi)
        hout_ref[gs, :] = res[:, n_st:] * h0g + res[:, :n_st]


def ssd_sample(xbc, z, dt_raw, conv_state, h0, conv_w, conv_b, dtb, alog, dsk_e, nw, expand, n_tok):
    bsz, _, conv_dim = xbc.shape
    d_inner = z.shape[2]
    row = lambda w: pl.BlockSpec((1, w), lambda b: (0, 0))
    tok = lambda w: pl.BlockSpec((None, TPAD, w), lambda b: (b, 0, 0))
    st = pl.BlockSpec((None, d_inner, SSM_STATE), lambda b: (b, 0, 0))
    return pl.pallas_call(
        functools.partial(_ssd_step_kernel, d_inner=d_inner, n_groups=SSM_GROUPS, n_tok=n_tok),
        grid=(bsz,),
        in_specs=[tok(conv_dim), tok(d_inner), tok(LANES),
                  pl.BlockSpec((None, SSM_CONV - 1, conv_dim), lambda b: (b, 0, 0)), st,
                  pl.BlockSpec((SSM_CONV, conv_dim), lambda b: (0, 0)), row(conv_dim),
                  row(LANES), row(LANES), row(d_inner), row(d_inner),
                  pl.BlockSpec((LANES, d_inner), lambda b: (0, 0))],
        out_specs=[tok(d_inner), st],
        out_shape=[jax.ShapeDtypeStruct((bsz, TPAD, d_inner), F32),
                   jax.ShapeDtypeStruct((bsz, d_inner, SSM_STATE), F32)],
        scratch_shapes=[pltpu.VMEM((8 + TPAD, conv_dim), F32),
                        pltpu.VMEM((TPAD, conv_dim), F32),
                        pltpu.VMEM(((2 + n_tok) * TPAD, d_inner), F32)],
        compiler_params=_cparams(1), name="ssd_sample")(
            xbc, z, dt_raw, conv_state, h0, conv_w, conv_b, dtb, alog, dsk_e, nw, expand)


HEADS_PER_STEP = 4


def _cache_attn_kernel(new_ref, c0_ref, c1_ref, c2_ref, o_ref, n0_ref, n1_ref, n2_ref, stage, tts, *, n_tok):
    first = jnp.logical_and(pl.program_id(0) == 0, pl.program_id(1) == 0)

    @pl.when(first)
    def _():
        stage[...] = jnp.zeros(stage.shape, F32)

    crefs = (c0_ref, c1_ref, c2_ref)
    orefs = (n0_ref, n1_ref, n2_ref)
    hd = ATT_HEAD_DIM
    nt = (((1,), (1,)), ((), ()))
    lane = lax.broadcasted_iota(jnp.int32, (hd, LANES), 1)
    t_new = lax.broadcasted_iota(jnp.int32, (TPAD, TPAD), 0)
    u_new = lax.broadcasted_iota(jnp.int32, (TPAD, TPAD), 1)

    for head in range(HEADS_PER_STEP):
        hs = slice(head * hd, (head + 1) * hd)
        parts = []
        for g, d in enumerate(ATT_DILATIONS):
            w = crefs[g].shape[-1]
            q = new_ref[g, 0, :, hs].astype(BF16)
            kn = new_ref[g, 1, :, hs].astype(BF16)
            vn = new_ref[g, 2, :, hs].astype(BF16)
            t_id = lax.broadcasted_iota(jnp.int32, (TPAD, w), 0)
            r_id = lax.broadcasted_iota(jnp.int32, (TPAD, w), 1)
            s = jnp.dot(q, crefs[g][0, head].astype(BF16), preferred_element_type=F32)
            s = jnp.where((r_id >= t_id) if d == 1 else ((r_id & (d - 1)) == t_id), s, NEG_INF)
            sn = lax.dot_general(q, kn, nt, preferred_element_type=F32)
            sn = jnp.where((u_new <= t_new) if d == 1 else (u_new == t_new), sn, NEG_INF)
            m = jnp.maximum(jnp.max(s, axis=1, keepdims=True), jnp.max(sn, axis=1, keepdims=True))
            p = jnp.exp(s - m)
            pn = jnp.exp(sn - m)
            l = jnp.sum(p, axis=1, keepdims=True) + jnp.sum(pn, axis=1, keepdims=True)
            num = (lax.dot_general(p.astype(BF16), crefs[g][1, head].astype(BF16), nt,
                                   preferred_element_type=F32)
                   + jnp.dot(pn.astype(BF16), vn, preferred_element_type=F32))
            parts.append((num, m, l))
        mx = jnp.maximum(jnp.maximum(parts[0][1], parts[1][1]), parts[2][1])
        wg = [jnp.exp(p_[1] - mx) for p_ in parts]
        num = wg[0] * parts[0][0]
        for g in range(1, 3):
            num = num + wg[g] * parts[g][0]
        den = wg[0] * parts[0][2] + wg[1] * parts[1][2] + wg[2] * parts[2][2]
        o_ref[:, hs] = num / den

    for pair in range(HEADS_PER_STEP // 2):
        ps = slice(pair * LANES, (pair + 1) * LANES)
        for g in range(3):
            for kv in range(2):
                stage[0:TPAD, :] = new_ref[g, 1 + kv, :, ps]
                tts[2 * g + kv] = stage[...].T
        for hh in range(2):
            head = pair * 2 + hh
            hs = slice(hh * hd, (hh + 1) * hd)
            for g in range(3):
                n_col = crefs[g].shape[-1] // LANES
                for kv in range(2):
                    cur = pltpu.roll(crefs[g][kv, head, :, 0:LANES], LANES - n_tok, 1)
                    for j in range(n_col):
                        if j + 1 < n_col:
                            nxt = pltpu.roll(crefs[g][kv, head, :, (j + 1) * LANES:(j + 2) * LANES],
                                             LANES - n_tok, 1)
                        else:
                            nxt = pltpu.roll(tts[2 * g + kv, hs, :], LANES - n_tok, 1)
                        orefs[g][kv, head, :, j * LANES:(j + 1) * LANES] = jnp.where(
                            lane < LANES - n_tok, cur, nxt)
                        cur = nxt


def cache_attn(new, caches, n_tok):
    bsz = new.shape[2]
    hps = HEADS_PER_STEP
    cspec = lambda c: pl.BlockSpec((None, 2, hps, ATT_HEAD_DIM, c.shape[-1]), lambda b, h: (b, 0, h, 0, 0))
    return pl.pallas_call(
        functools.partial(_cache_attn_kernel, n_tok=n_tok), grid=(bsz, ATT_HEADS // hps),
        in_specs=[pl.BlockSpec((3, 3, None, TPAD, hps * ATT_HEAD_DIM), lambda b, h: (0, 0, b, 0, h))]
        + [cspec(c) for c in caches],
        out_specs=[pl.BlockSpec((None, TPAD, hps * ATT_HEAD_DIM), lambda b, h: (b, 0, h))]
        + [cspec(c) for c in caches],
        out_shape=[jax.ShapeDtypeStruct((bsz, TPAD, ATT_WIDTH), F32)]
        + [jax.ShapeDtypeStruct(c.shape, c.dtype) for c in caches],
        scratch_shapes=[pltpu.VMEM((LANES, LANES), F32), pltpu.VMEM((6, LANES, LANES), F32)],
        compiler_params=_cparams(2), name="cache_attn")(new, *caches)


def _router(logits):
    lanef = lax.broadcasted_iota(jnp.int32, logits.shape, 1).astype(F32)
    big = 1e9
    lc = jnp.where(lanef < MOE_GROUPS, logits, NEG_INF)
    mc = jnp.max(lc, axis=-1, keepdims=True)
    g_sel = jnp.min(jnp.where(lc == mc, lanef, big), axis=-1, keepdims=True)
    p_sel = 1.0 / jnp.sum(jnp.exp(lc - mc), axis=-1, keepdims=True)
    base = MOE_GROUPS + MOE_PER_GROUP * g_sel
    lf = jnp.where(jnp.logical_and(lanef >= base, lanef < base + MOE_PER_GROUP), logits, NEG_INF)
    v1 = jnp.max(lf, axis=-1, keepdims=True)
    i1 = jnp.min(jnp.where(lf == v1, lanef, big), axis=-1, keepdims=True)
    lf2 = jnp.where(lanef == i1, NEG_INF, lf)
    v2 = jnp.max(lf2, axis=-1, keepdims=True)
    i2 = jnp.min(jnp.where(lf2 == v2, lanef, big), axis=-1, keepdims=True)
    e2 = jnp.exp(v2 - v1)
    den = 1.0 + e2
    w1 = (1.0 / den) * p_sel
    w2 = (e2 / den) * p_sel
    return jnp.where(lanef == i1, w1, 0.0) + jnp.where(lanef == i2, w2, 0.0)


def _outproj_kernel(*refs, dils, tm):
    n_g = len(dils)
    if n_g:
        x_ref, g_ref, ys_ref = refs[:3]
        att_refs = refs[3:3 + 2 * n_g]
        rest = refs[3 + 2 * n_g:]
    else:
        x_ref, g_ref, ys_ref, attn_ref = refs[:4]
        rest = refs[4:]
    wa_ref, ws_ref, wo_ref, n2_ref, wr_ref, br_ref, x2_ref, xn_ref, cmb_ref = rest[:9]
    scr = rest[9:]

    if n_g:
        n_ch = ATT_WIDTH // LANES
        nums, stats = [], []
        for g, d in enumerate(dils):
            num_ref, st_ref = att_refs[2 * g], att_refs[2 * g + 1]
            if d == 1:
                nums.append([num_ref[0, :, ch * LANES:(ch + 1) * LANES] for ch in range(n_ch)])
                stats.append(st_ref[0])
            else:
                ns, ss = scr[2 * g], scr[2 * g + 1]
                for r in range(d):
                    for ch in range(n_ch):
                        ns[ch, pl.ds(r, tm // d, stride=d), :] = num_ref[r, :, ch * LANES:(ch + 1) * LANES]
                    ss[pl.ds(r, tm // d, stride=d), :] = st_ref[r]
                nums.append([ns[ch] for ch in range(n_ch)])
                stats.append(ss[...])
        lane = lax.broadcasted_iota(jnp.int32, (tm, LANES), 1)
        lo = lane < ATT_HEAD_DIM
        chunks = []
        for ch in range(ATT_WIDTH // LANES):
            wts, dens = [], []
            for h in (2 * ch, 2 * ch + 1):
                ms = [s[:, h:h + 1] for s in stats]
                ls = [s[:, ATT_HEADS + h:ATT_HEADS + h + 1] for s in stats]
                mx = ms[0]
                for m in ms[1:]:
                    mx = jnp.maximum(mx, m)
                w = [jnp.exp(m - mx) for m in ms]
                den = w[0] * ls[0]
                for g in range(1, n_g):
                    den = den + w[g] * ls[g]
                wts.append(w)
                dens.append(den)
            num = jnp.where(lo, wts[0][0], wts[1][0]) * nums[0][ch]
            for g in range(1, n_g):
                num = num + jnp.where(lo, wts[0][g], wts[1][g]) * nums[g][ch]
            chunks.append((num / jnp.where(lo, dens[0], dens[1])).astype(BF16))
        attn = jnp.concatenate(chunks, axis=1)
    else:
        attn = attn_ref[...].astype(BF16)

    dm = x_ref.shape[1]
    a = jnp.dot(attn, wa_ref[...], preferred_element_type=F32)
    s = jnp.dot(ys_ref[...].astype(BF16), ws_ref[...], preferred_element_type=F32)
    gates = g_ref[...].astype(F32)
    mixed = _sigmoid(gates[:, :dm]) * a + _sigmoid(gates[:, dm:]) * s
    x2 = x_ref[...] + jnp.dot(mixed.astype(BF16), wo_ref[...], preferred_element_type=F32)
    x2_ref[...] = x2
    xn = (x2 * lax.rsqrt(jnp.mean(x2 * x2, axis=-1, keepdims=True) + RMS_EPS) * n2_ref[...]).astype(BF16)
    xn_ref[...] = xn
    logits = jnp.dot(xn, wr_ref[...], preferred_element_type=F32) + br_ref[...]
    cmb_ref[...] = _router(logits)


def outproj(x, gates, yssm, att, dils, batch, seq, weights, tm):
    n, dm = x.shape
    wa, ws, wo, n2, wr, br = weights
    tpb = seq // tm
    tok = lambda w: pl.BlockSpec((tm, w), lambda i: (i, 0))
    full = lambda a: pl.BlockSpec(a.shape, lambda i: (0,) * a.ndim)
    in_specs = [tok(dm), tok(gates.shape[1]), tok(yssm.shape[1])]
    args = [x, gates, yssm]
    scratch = []
    if dils:
        for (num, st), d in zip(att, dils):
            for arr in (num, st):
                in_specs.append(pl.BlockSpec((None, d, tm // d, arr.shape[-1]),
                                             lambda i: (i // tpb, 0, i % tpb, 0)))
                args.append(arr)
                scratch.append(pltpu.VMEM((tm, LANES) if arr.shape[-1] == LANES
                                          else (arr.shape[-1] // LANES, tm, LANES), F32))
    else:
        in_specs.append(tok(att.shape[1]))
        args.append(att)
    in_specs += [full(wa), full(ws), full(wo), full(n2), full(wr), full(br)]
    args += [wa, ws, wo, n2, wr, br]
    return pl.pallas_call(
        functools.partial(_outproj_kernel, dils=tuple(dils), tm=tm), grid=(n // tm,),
        in_specs=in_specs,
        out_specs=[tok(dm), tok(dm), tok(LANES)],
        out_shape=[jax.ShapeDtypeStruct((n, dm), F32), jax.ShapeDtypeStruct((n, dm), BF16),
                   jax.ShapeDtypeStruct((n, LANES), F32)],
        scratch_shapes=scratch,
        compiler_params=_cparams(1), name="outproj")(*args)


def _moe_kernel(xn_ref, cmb_ref, x2_ref, wg_ref, wu_ref, wd_ref, nf_ref, o_ref, acc_ref, *, final_norm):
    e = pl.program_id(1)
    ne = pl.num_programs(1)

    @pl.when(e == 0)
    def _():
        acc_ref[...] = jnp.zeros(acc_ref.shape, F32)

    xn = xn_ref[...]
    he = _silu(jnp.dot(xn, wg_ref[...], preferred_element_type=F32)) * jnp.dot(
        xn, wu_ref[...], preferred_element_type=F32)
    ye = jnp.dot(he.astype(BF16), wd_ref[...], preferred_element_type=F32)
    cmb = cmb_ref[...]
    lane = lax.broadcasted_iota(jnp.int32, cmb.shape, 1)
    ce = jnp.sum(jnp.where(lane == e + MOE_GROUPS, cmb, 0.0), axis=-1, keepdims=True)
    acc_ref[...] += ce * ye

    @pl.when(e == ne - 1)
    def _():
        xf = x2_ref[...] + acc_ref[...]
        if final_norm:
            xf = xf * lax.rsqrt(jnp.mean(xf * xf, axis=-1, keepdims=True) + RMS_EPS) * nf_ref[...]
        o_ref[...] = xf


def moe_final(xn, cmb, x2, wg, wu, wd, nf, tm, final_norm):
    n, dm = xn.shape
    ne, _, ff = wg.shape
    tok = lambda w: pl.BlockSpec((tm, w), lambda i, e: (i, 0))
    return pl.pallas_call(
        functools.partial(_moe_kernel, final_norm=final_norm), grid=(n // tm, ne),
        in_specs=[tok(dm), tok(LANES), tok(dm),
                  pl.BlockSpec((None, dm, ff), lambda i, e: (e, 0, 0)),
                  pl.BlockSpec((None, dm, ff), lambda i, e: (e, 0, 0)),
                  pl.BlockSpec((None, ff, dm), lambda i, e: (e, 0, 0)),
                  pl.BlockSpec((1, dm), lambda i, e: (0, 0))],
        out_specs=tok(dm),
        out_shape=jax.ShapeDtypeStruct((n, dm), F32),
        scratch_shapes=[pltpu.VMEM((tm, dm), F32)],
        compiler_params=_cparams(2), name="moe_final")(xn, cmb, x2, wg, wu, wd, nf)


def _pad_lanes(v, width=LANES):
    return jnp.pad(v, ((0, 0), (0, width - v.shape[1])))


def _natural_rows(arr, n_rows):
    b, d, l, w = arr.shape
    tail = arr[:, :, l - n_rows // d:, :]
    return jnp.swapaxes(tail, 1, 2).reshape(b, n_rows, w)


def kernel(x_prompt, x_sample, cache_kv_w128, cache_kv_w512, cache_kv_w2048, state_ssm, state_conv,
           norm1, w_in, conv_w, conv_b, dt_bias, a_log, d_skip, ssm_norm, w_att_out, w_ssm_out, w_o,
           norm2, w_router_coarse, b_router_coarse, w_router_fine, b_router_fine,
           w_exp_gate, w_exp_up, w_exp_down, norm_f):
    bp, sp, dm = x_prompt.shape
    bs, ts, _ = x_sample.shape
    depth = w_in.shape[0]
    n_heads = dt_bias.shape[1]
    d_inner = n_heads * SSM_HEAD_DIM
    conv_dim = conv_w.shape[2]
    n_grp = len(ATT_DILATIONS)
    off_z = n_grp * 3 * ATT_WIDTH
    off_xbc = off_z + d_inner
    off_dt = off_xbc + conv_dim
    off_gate = off_dt + n_heads
    caches = (cache_kv_w128, cache_kv_w512, cache_kv_w2048)
    for g in range(n_grp):
        assert caches[g].shape[2] == ATT_WINDOWS[g] and sp % (ATT_DILATIONS[g] * ATT_BLOCK) == 0
    assert ts <= min(ATT_DILATIONS[1:]) and ts < TPAD and sp % SSM_CHUNK == 0

    tabs_p = rope_tables(jnp.arange(sp, dtype=F32))
    tabs_s = rope_tables(jnp.tile(jnp.arange(ts, dtype=F32) + PAST_LEN, bs))
    expand = (jnp.arange(LANES)[:, None] == (jnp.arange(d_inner) // SSM_HEAD_DIM)[None, :]).astype(F32)

    xp = x_prompt.reshape(bp * sp, dm)
    xs = x_sample.reshape(bs * ts, dm)
    outs = {k: [] for k in ("kvp0", "kvp1", "kvp2", "ssm_p", "conv_p", "kvs0", "kvs1", "kvs2", "ssm_s", "conv_s")}
    n_s = bs * ts
    for layer in range(depth):
        w = w_in[layer].astype(BF16)
        w_qkv = [w[:, g * 3 * ATT_WIDTH:(g + 1) * 3 * ATT_WIDTH] for g in range(n_grp)]
        w_z, w_xbc, w_gate = w[:, off_z:off_xbc], w[:, off_xbc:off_dt], w[:, off_gate:]
        w_dt = _pad_lanes(w[:, off_dt:off_gate])
        dtb = _pad_lanes(dt_bias[layer][None])
        alog = _pad_lanes(a_log[layer][None])
        dsk_e = jnp.repeat(d_skip[layer], SSM_HEAD_DIM)[None]
        nw = ssm_norm[layer][None]
        cw, cb = conv_w[layer], conv_b[layer][None]
        w_router = _pad_lanes(jnp.concatenate([w_router_coarse[layer], w_router_fine[layer]], axis=1)).astype(BF16)
        b_router = _pad_lanes(jnp.concatenate([b_router_coarse[layer], b_router_fine[layer]])[None])
        wts = (w_att_out[layer].astype(BF16), w_ssm_out[layer].astype(BF16), w_o[layer].astype(BF16),
               norm2[layer][None], w_router, b_router)
        wg, wu, wd = (w_exp_gate[layer].astype(BF16), w_exp_up[layer].astype(BF16),
                      w_exp_down[layer].astype(BF16))

        xn = rmsnorm_bf16(xp, norm1[layer], 1024)
        att = []
        for g, d in enumerate(ATT_DILATIONS):
            qkv = qkv_proj(xn, w_qkv[g], tabs_p, bp, sp, d, 1024, BF16, f"qkv_prompt_d{d}")
            att.append(attn_prompt(qkv))
            wnd = min(ATT_WINDOWS[g], sp)
            kv = jnp.stack([_natural_rows(qkv[1], wnd), _natural_rows(qkv[2], wnd)], axis=2)
            outs[f"kvp{g}"].append(kv.astype(F32).reshape(bp, wnd, 2, ATT_HEADS, ATT_HEAD_DIM))
        z = matmul(xn, w_z, BF16, 2048, 1024, "proj_z")
        xbc = matmul(xn, w_xbc, BF16, 2048, 1024, "proj_xbc")
        gates = matmul(xn, w_gate, BF16, 2048, 1024, "proj_gates")
        dt_raw = matmul(xn, w_dt, F32, 2048, LANES, "proj_dt")
        y_ssm, h_p = ssd_prompt(xbc, z, dt_raw, cw, cb, dtb, alog, dsk_e, nw, bp, sp)
        outs["ssm_p"].append(h_p.reshape(bp, n_heads, SSM_HEAD_DIM, SSM_STATE))
        outs["conv_p"].append(xbc.reshape(bp, sp, conv_dim)[:, sp - (SSM_CONV - 1):].astype(F32))
        x2, xn2, cmb = outproj(xp, gates, y_ssm, att, ATT_DILATIONS, bp, sp, wts, 256)
        xp = moe_final(xn2, cmb, x2, wg, wu, wd, norm_f[None], 512, layer == depth - 1)

        xn = rmsnorm_bf16(xs, norm1[layer], n_s)
        pad_t = lambda a: jnp.pad(a.reshape(bs, ts, a.shape[-1]), ((0, 0), (0, TPAD - ts), (0, 0)))
        new = jnp.stack([qkv_proj(xn, w_qkv[g], tabs_s, 1, n_s, 1, n_s, F32, f"qkv_sample_{g}")
                         .reshape(3, bs, ts, ATT_WIDTH) for g in range(n_grp)])
        new = jnp.pad(new, ((0, 0), (0, 0), (0, 0), (0, TPAD - ts), (0, 0)))
        cl = [jnp.transpose(caches[g][layer], (0, 2, 3, 4, 1)) for g in range(n_grp)]
        res = cache_attn(new, cl, ts)
        attn_s = res[0][:, :ts]
        for g in range(n_grp):
            outs[f"kvs{g}"].append(jnp.transpose(res[1 + g], (0, 4, 1, 2, 3)))
        z = matmul(xn, w_z, F32, n_s, 512, "proj_z_s")
        xbc = matmul(xn, w_xbc, F32, n_s, 512, "proj_xbc_s")
        gates = matmul(xn, w_gate, F32, n_s, 512, "proj_gates_s")
        dt_raw = matmul(xn, w_dt, F32, n_s, LANES, "proj_dt_s")
        y_s, h_s = ssd_sample(pad_t(xbc), pad_t(z), pad_t(dt_raw), state_conv[layer],
                              state_ssm[layer].reshape(bs, d_inner, SSM_STATE), cw, cb, dtb, alog, dsk_e, nw,
                              expand, ts)
        outs["ssm_s"].append(h_s.reshape(bs, n_heads, SSM_HEAD_DIM, SSM_STATE))
        hist = jnp.concatenate([state_conv[layer], xbc.reshape(bs, ts, conv_dim)], axis=1)
        outs["conv_s"].append(hist[:, ts:])
        y_s = y_s[:, :ts].reshape(n_s, d_inner)
        x2, xn2, cmb = outproj(xs, gates, y_s, attn_s.reshape(n_s, ATT_WIDTH), (), 1, n_s, wts, min(256, n_s))
        xs = moe_final(xn2, cmb, x2, wg, wu, wd, norm_f[None], n_s, layer == depth - 1)

    st = lambda k: jnp.stack(outs[k])
    return (xp.reshape(bp, sp, dm), xs.reshape(bs, ts, dm),
            st("kvp0"), st("kvp1"), st("kvp2"), st("ssm_p"), st("conv_p"),
            st("kvs0"), st("kvs1"), st("kvs2"), st("ssm_s"), st("conv_s"))
```

```python
import functools
import math

import jax
import jax.numpy as jnp
from jax import lax
from jax.experimental import pallas as pl
from jax.experimental.pallas import tpu as pltpu

F32 = jnp.float32
BF16 = jnp.bfloat16

PAST_LEN = 8192
ATT_WINDOWS = (128, 512, 2048)
ATT_DILATIONS = (1, 4, 16)
ATT_HEADS = 8
ATT_HEAD_DIM = 64
ATT_WIDTH = ATT_HEADS * ATT_HEAD_DIM
ATT_SCALE = ATT_HEAD_DIM ** -0.5
ROT_DIM = ATT_HEAD_DIM // 4
ROPE_THETA = 500000.0
ATT_BLOCK = 128

SSM_HEAD_DIM = 64
SSM_STATE = 128
SSM_GROUPS = 4
SSM_CONV = 4
SSM_CHUNK = 128
MOE_GROUPS = 4
MOE_PER_GROUP = 4
MOE_EXPERTS = MOE_GROUPS * MOE_PER_GROUP
RMS_EPS = 1e-6
SSM_NORM_EPS = 1e-5

LANES = 128
VMEM_LIMIT = 56 * 1024 * 1024
NEG_INF = float("-inf")


def _cparams(n_axes):
    return pltpu.CompilerParams(dimension_semantics=("arbitrary",) * n_axes,
                                vmem_limit_bytes=VMEM_LIMIT)


def _sigmoid(x):
    return 1.0 / (1.0 + jnp.exp(-x))


def _silu(x):
    return x * _sigmoid(x)


def _norm_kernel(x_ref, g_ref, o_ref):
    x = x_ref[...]
    y = x * lax.rsqrt(jnp.mean(x * x, axis=-1, keepdims=True) + RMS_EPS)
    o_ref[...] = (y * g_ref[...]).astype(o_ref.dtype)


def rmsnorm_bf16(x, g, tm):
    n, d = x.shape
    return pl.pallas_call(
        _norm_kernel, grid=(n // tm,),
        in_specs=[pl.BlockSpec((tm, d), lambda i: (i, 0)), pl.BlockSpec((1, d), lambda i: (0, 0))],
        out_specs=pl.BlockSpec((tm, d), lambda i: (i, 0)),
        out_shape=jax.ShapeDtypeStruct((n, d), BF16),
        compiler_params=_cparams(1), name="rmsnorm")(x, g.reshape(1, d))


def _mm_kernel(x_ref, w_ref, o_ref):
    o_ref[...] = jnp.dot(x_ref[...], w_ref[...], preferred_element_type=F32).astype(o_ref.dtype)


def matmul(x, w, out_dtype, tm, tn, name):
    m, k = x.shape
    n = w.shape[1]
    return pl.pallas_call(
        _mm_kernel, grid=(m // tm, n // tn),
        in_specs=[pl.BlockSpec((tm, k), lambda i, j: (i, 0)), pl.BlockSpec((k, tn), lambda i, j: (0, j))],
        out_specs=pl.BlockSpec((tm, tn), lambda i, j: (i, j)),
        out_shape=jax.ShapeDtypeStruct((m, n), out_dtype),
        compiler_params=_cparams(2), name=name)(x, w)


QKV_CHUNK = 256


def _qkv_kernel(x_ref, w_ref, cos_ref, sa_ref, sb_ref, o_ref, acc_ref, *, d):
    sc = jnp.where(pl.program_id(1) == 0, ATT_SCALE, 1.0).astype(F32)
    tm = x_ref.shape[0]
    ck = min(QKV_CHUNK, tm)
    for c0 in range(0, tm, ck):
        rows = slice(c0, c0 + ck)
        acc = jnp.dot(x_ref[rows, :], w_ref[...], preferred_element_type=F32)
        c = cos_ref[rows, :] * sc
        sa = sa_ref[rows, :] * sc
        sb = sb_ref[rows, :] * sc
        for ch in range(ATT_WIDTH // LANES):
            cs = slice(ch * LANES, (ch + 1) * LANES)
            t = acc[:, cs]
            r = t * c + pltpu.roll(t, LANES - ROT_DIM // 2, 1) * sa + pltpu.roll(t, ROT_DIM // 2, 1) * sb
            if d == 1:
                o_ref[0, rows, cs] = r.astype(o_ref.dtype)
            else:
                acc_ref[ch, rows, :] = r
                for res in range(d):
                    o_ref[res, c0 // d:(c0 + ck) // d, cs] = acc_ref[
                        ch, pl.ds(c0 + res, ck // d, stride=d), :].astype(o_ref.dtype)


def qkv_proj(xn, w, tabs, batch, seq, d, tm, out_dtype, name):
    n, dm = xn.shape
    tpb = seq // tm
    cos, sa, sb = tabs
    tab_spec = pl.BlockSpec((None, tm, LANES), lambda i, j: (jnp.where(j == 2, 1, 0), i % tpb, 0))
    return pl.pallas_call(
        functools.partial(_qkv_kernel, d=d), grid=(n // tm, 3),
        in_specs=[pl.BlockSpec((tm, dm), lambda i, j: (i, 0)),
                  pl.BlockSpec((dm, ATT_WIDTH), lambda i, j: (0, j)),
                  tab_spec, tab_spec, tab_spec],
        out_specs=pl.BlockSpec((None, None, d, tm // d, ATT_WIDTH),
                               lambda i, j: (j, i // tpb, 0, i % tpb, 0)),
        out_shape=jax.ShapeDtypeStruct((3, batch, d, seq // d, ATT_WIDTH), out_dtype),
        scratch_shapes=[pltpu.VMEM((ATT_WIDTH // LANES, tm, LANES), F32)],
        compiler_params=_cparams(2), name=name)(xn, w, cos, sa, sb)


def rope_tables(pos):
    half = ROT_DIM // 2
    inv = ROPE_THETA ** (-jnp.arange(half, dtype=F32) / half)
    ang = pos[:, None] * inv[None, :]
    cos = jnp.cos(ang)
    sin = jnp.sin(ang)
    s = pos.shape[0]
    ones = jnp.ones((s, ATT_HEAD_DIM - ROT_DIM), F32)
    zeros = jnp.zeros((s, ATT_HEAD_DIM - ROT_DIM), F32)
    zh = jnp.zeros((s, half), F32)
    c = jnp.concatenate([cos, cos, ones], axis=1)
    sa = jnp.concatenate([-sin, zh, zeros], axis=1)
    sb = jnp.concatenate([zh, sin, zeros], axis=1)
    rep = LANES // ATT_HEAD_DIM
    c, sa, sb = (jnp.tile(t, (1, rep)) for t in (c, sa, sb))
    return (jnp.stack([c, jnp.ones_like(c)]), jnp.stack([sa, jnp.zeros_like(sa)]),
            jnp.stack([sb, jnp.zeros_like(sb)]))


ATT_ROW_CHUNK = 32


def _attn_kernel(q_ref, kc_ref, vc_ref, kp_ref, vp_ref, num_ref, st_ref, s_scr, p_scr):
    nb = pl.program_id(2)
    blk = q_ref.shape[0]
    nt = (((1,), (1,)), ((), ()))
    heads = [slice(h * ATT_HEAD_DIM, (h + 1) * ATT_HEAD_DIM) for h in range(ATT_HEADS)]
    for h, sl in enumerate(heads):
        q = q_ref[:, sl]
        s_scr[h, :, 0:blk] = lax.dot_general(q, kp_ref[:, sl], nt, preferred_element_type=F32)
        s_scr[h, :, blk:2 * blk] = lax.dot_general(q, kc_ref[:, sl], nt, preferred_element_type=F32)
    rc = ATT_ROW_CHUNK
    qi = lax.broadcasted_iota(jnp.int32, (rc, 2 * blk), 0)
    kj = lax.broadcasted_iota(jnp.int32, (rc, 2 * blk), 1)
    lane = lax.broadcasted_iota(jnp.int32, (rc, LANES), 1)
    for r0 in range(0, blk, rc):
        qa = qi + r0
        mask = jnp.logical_or(jnp.logical_and(jnp.logical_and(kj < blk, kj >= qa), nb > 0),
                              jnp.logical_and(kj >= blk, kj - blk <= qa))
        st = jnp.zeros((rc, LANES), F32)
        for h in range(ATT_HEADS):
            s = jnp.where(mask, s_scr[h, r0:r0 + rc, :], NEG_INF)
            m = jnp.max(s, axis=-1, keepdims=True)
            p = jnp.exp(s - m)
            l = jnp.sum(p, axis=-1, keepdims=True)
            p_scr[h, r0:r0 + rc, :] = p.astype(BF16)
            st = jnp.where(lane == h, m, st)
            st = jnp.where(lane == ATT_HEADS + h, l, st)
        st_ref[r0:r0 + rc, :] = st
    for h, sl in enumerate(heads):
        num_ref[:, sl] = (jnp.dot(p_scr[h, :, 0:blk], vp_ref[:, sl], preferred_element_type=F32)
                          + jnp.dot(p_scr[h, :, blk:2 * blk], vc_ref[:, sl], preferred_element_type=F32))


def attn_prompt(qkv):
    _, b, d, l, w = qkv.shape
    nblk = l // ATT_BLOCK

    def spec(kind, prev):
        if prev:
            return pl.BlockSpec((None, None, None, ATT_BLOCK, w),
                                lambda bi, r, n: (kind, bi, r, jnp.maximum(n - 1, 0), 0))
        return pl.BlockSpec((None, None, None, ATT_BLOCK, w), lambda bi, r, n: (kind, bi, r, n, 0))

    return pl.pallas_call(
        _attn_kernel, grid=(b, d, nblk),
        in_specs=[spec(0, False), spec(1, False), spec(2, False), spec(1, True), spec(2, True)],
        out_specs=[pl.BlockSpec((None, None, ATT_BLOCK, w), lambda bi, r, n: (bi, r, n, 0)),
                   pl.BlockSpec((None, None, ATT_BLOCK, LANES), lambda bi, r, n: (bi, r, n, 0))],
        out_shape=[jax.ShapeDtypeStruct((b, d, l, w), F32), jax.ShapeDtypeStruct((b, d, l, LANES), F32)],
        scratch_shapes=[pltpu.VMEM((ATT_HEADS, ATT_BLOCK, 2 * ATT_BLOCK), F32),
                        pltpu.VMEM((ATT_HEADS, ATT_BLOCK, 2 * ATT_BLOCK), BF16)],
        compiler_params=_cparams(3), name=f"attn_prompt_d{d}")(qkv, qkv, qkv, qkv, qkv)


def _col(mat, h, n_lanes=LANES):
    return jnp.broadcast_to(mat[:, h:h + 1], (mat.shape[0], n_lanes))


def _ssd_kernel(xbc_ref, z_ref, dt_ref, cw_ref, cb_ref, dtb_ref, alog_ref, dsk_ref, nw_ref,
                y_ref, hout_ref,
                cbuf, xs_s, xw_s, eac_s, cd_s, y_s, ht_s, *, d_inner, n_groups):
    c = pl.program_id(1)
    nc = pl.num_programs(1)
    q = SSM_CHUNK
    n_st = SSM_STATE
    gw = d_inner // n_groups
    conv_dim = xbc_ref.shape[1]
    top = 8

    @pl.when(c == 0)
    def _():
        cbuf[0:top, :] = jnp.zeros((top, conv_dim), F32)
        ht_s[...] = jnp.zeros(ht_s.shape, F32)

    cbuf[top:top + q, :] = xbc_ref[...].astype(F32)
    cwid = 512
    for j in range(conv_dim // cwid):
        cs = slice(j * cwid, (j + 1) * cwid)
        acc = cb_ref[:, cs] + cbuf[top - 3:top - 3 + q, cs] * cw_ref[0:1, cs]
        for i in range(1, SSM_CONV):
            acc = acc + cbuf[top - 3 + i:top - 3 + i + q, cs] * cw_ref[i:i + 1, cs]
        xs_s[:, cs] = _silu(acc)
    cbuf[top - 3:top, :] = cbuf[top + q - 3:top + q, :]

    dt = jax.nn.softplus(dt_ref[...] + dtb_ref[...])
    a_row = -jnp.exp(alog_ref[...])
    a = dt * a_row
    ri = lax.broadcasted_iota(jnp.int32, (q, q), 0)
    ci = lax.broadcasted_iota(jnp.int32, (q, q), 1)
    causal = ri >= ci
    tril = jnp.where(causal, 1.0, 0.0).astype(F32)
    acum = jnp.dot(tril, a, preferred_element_type=F32, precision=lax.Precision.HIGHEST)
    acum_t = acum.T
    lane = lax.broadcasted_iota(jnp.int32, (q, LANES), 1)
    lo = lane < SSM_HEAD_DIM

    hpg = gw // SSM_HEAD_DIM
    for g in range(n_groups):
        bc = xs_s[:, d_inner + g * n_st:d_inner + (g + 1) * n_st].astype(BF16)
        cc = xs_s[:, d_inner + n_groups * n_st + g * n_st:d_inner + n_groups * n_st + (g + 1) * n_st].astype(BF16)
        cb = lax.dot_general(cc, bc, (((1,), (1,)), ((), ())), preferred_element_type=F32)
        for jp in range(hpg // 2):
            h0 = g * hpg + 2 * jp
            ls = slice(g * gw + jp * LANES, g * gw + (jp + 1) * LANES)
            ac0 = _col(acum, h0)
            ac1 = _col(acum, h0 + 1)
            acum_e = jnp.where(lo, ac0, ac1)
            dt_e = jnp.where(lo, _col(dt, h0), _col(dt, h0 + 1))
            xdt = xs_s[:, ls] * dt_e
            acl_e = acum_e[q - 1:q, :]
            xw_s[:, ls] = (xdt * jnp.exp(acl_e - acum_e)).astype(BF16)
            eac_s[:, ls] = jnp.exp(acum_e)
            cd_s[:, ls] = jnp.exp(acl_e)
            xdt_b = xdt.astype(BF16)
            zero = jnp.zeros_like(xdt_b)
            m0 = (jnp.exp(jnp.where(causal, ac0 - acum_t[h0:h0 + 1, :], NEG_INF)) * cb).astype(BF16)
            m1 = (jnp.exp(jnp.where(causal, ac1 - acum_t[h0 + 1:h0 + 2, :], NEG_INF)) * cb).astype(BF16)
            y_s[:, ls] = (jnp.dot(m0, jnp.where(lo, xdt_b, zero), preferred_element_type=F32)
                          + jnp.dot(m1, jnp.where(lo, zero, xdt_b), preferred_element_type=F32))
        gs = slice(g * gw, (g + 1) * gw)
        h_prev = ht_s[g]
        y_off = jnp.dot(cc, h_prev.astype(BF16), preferred_element_type=F32) * eac_s[:, gs]
        y_s[:, gs] = y_s[:, gs] + y_off
        st = lax.dot_general(bc, xw_s[:, gs], (((0,), (0,)), ((), ())), preferred_element_type=F32)
        ht_s[g] = cd_s[:, gs] * h_prev + st

    for g in range(n_groups):
        gs = slice(g * gw, (g + 1) * gw)
        y = y_s[:, gs] + dsk_ref[:, gs] * xs_s[:, gs]
        yf = y * _silu(z_ref[:, gs].astype(F32))
        yf = yf * lax.rsqrt(jnp.mean(yf * yf, axis=-1, keepdims=True) + SSM_NORM_EPS)
        y_ref[:, gs] = (yf * nw_ref[:, gs]).astype(y_ref.dtype)

    @pl.when(c == nc - 1)
    def _():
        for g in range(n_groups):
            hout_ref[g * gw:(g + 1) * gw, :] = ht_s[g].T


def ssd_prompt(xbc, z, dt_raw, conv_w, conv_b, dtb, alog, dsk_e, nw, batch, seq):
    n, conv_dim = xbc.shape
    d_inner = z.shape[1]
    n_groups = SSM_GROUPS
    gw = d_inner // n_groups
    q = SSM_CHUNK
    cps = seq // q
    row = lambda w: pl.BlockSpec((1, w), lambda b, c: (0, 0))
    tok = lambda w: pl.BlockSpec((q, w), lambda b, c: (b * cps + c, 0))
    return pl.pallas_call(
        functools.partial(_ssd_kernel, d_inner=d_inner, n_groups=n_groups),
        grid=(batch, cps),
        in_specs=[tok(conv_dim), tok(d_inner), tok(LANES),
                  pl.BlockSpec((SSM_CONV, conv_dim), lambda b, c: (0, 0)), row(conv_dim),
                  row(LANES), row(LANES), row(d_inner), row(d_inner)],
        out_specs=[tok(d_inner), pl.BlockSpec((None, d_inner, SSM_STATE), lambda b, c: (b, 0, 0))],
        out_shape=[jax.ShapeDtypeStruct((n, d_inner), BF16),
                   jax.ShapeDtypeStruct((batch, d_inner, SSM_STATE), F32)],
        scratch_shapes=[pltpu.VMEM((8 + q, conv_dim), F32),
                        pltpu.VMEM((q, conv_dim), F32),
                        pltpu.VMEM((q, d_inner), BF16),
                        pltpu.VMEM((q, d_inner), F32),
                        pltpu.VMEM((1, d_inner), F32),
                        pltpu.VMEM((q, d_inner), F32),
                        pltpu.VMEM((n_groups, SSM_STATE, gw), F32)],
        compiler_params=_cparams(2), name="ssd_prompt")(xbc, z, dt_raw, conv_w, conv_b, dtb, alog, dsk_e, nw)


TPAD = 8


def _ssd_step_kernel(xbc_ref, z_ref, dt_ref, cst_ref, h0_ref, cw_ref, cb_ref, dtb_ref, alog_ref,
                     dsk_ref, nw_ref, exp_ref, y_ref, hout_ref,
                     cbuf, xs_s, f_s, *, d_inner, n_groups, n_tok):
    b = pl.program_id(0)
    n_st = SSM_STATE
    gw = d_inner // n_groups
    conv_dim = xbc_ref.shape[1]
    top = 8
    hist = SSM_CONV - 1

    @pl.when(b == 0)
    def _():
        cbuf[0:top, :] = jnp.zeros((top, conv_dim), F32)

    cbuf[top - hist:top, :] = cst_ref[...]
    cbuf[top:top + TPAD, :] = xbc_ref[...]
    acc = cb_ref[...] + cbuf[top - hist:top - hist + TPAD, :] * cw_ref[0:1, :]
    for i in range(1, SSM_CONV):
        acc = acc + cbuf[top - hist + i:top - hist + i + TPAD, :] * cw_ref[i:i + 1, :]
    xs_s[...] = _silu(acc)

    rid = lax.broadcasted_iota(jnp.int32, (TPAD, LANES), 0)
    dt = jax.nn.softplus(dt_ref[...] + dtb_ref[...])
    da = dt * (-jnp.exp(alog_ref[...]))
    cum = da
    for k in range(1, n_tok):
        cum = cum + jnp.where(rid >= k, pltpu.roll(da, k, 0), 0.0)
    facs = [dt, jnp.exp(cum)]
    for t in range(n_tok):
        facs.append(jnp.where(rid <= t, jnp.exp(cum[t:t + 1, :] - cum), 0.0))
    hi = lax.Precision.HIGHEST
    f_s[...] = jnp.dot(jnp.concatenate(facs, axis=0), exp_ref[...], preferred_element_type=F32, precision=hi)

    nt = (((1,), (1,)), ((), ()))
    row8 = lax.broadcasted_iota(jnp.int32, (TPAD, gw), 0)
    for g in range(n_groups):
        gs = slice(g * gw, (g + 1) * gw)
        xdt = xs_s[:, gs] * f_s[0:TPAD, gs]
        b_f = xs_s[:, d_inner + g * n_st:d_inner + (g + 1) * n_st]
        c_f = xs_s[:, d_inner + (n_groups + g) * n_st:d_inner + (n_groups + g + 1) * n_st]
        b16 = b_f.astype(BF16)
        c16 = c_f.astype(BF16)
        h0g = h0_ref[gs, :]
        y = f_s[TPAD:2 * TPAD, gs] * lax.dot_general(c16, h0g.astype(BF16), nt, preferred_element_type=F32)
        bc = lax.dot_general(b16, c16, nt, preferred_element_type=F32)
        for t in range(n_tok):
            term = bc[:, t:t + 1] * f_s[(2 + t) * TPAD:(3 + t) * TPAD, gs] * xdt
            y = y + jnp.where(row8 == t, jnp.sum(term, axis=0, keepdims=True), 0.0)
        yv = y + dsk_ref[:, gs] * xs_s[:, gs]
        yf = yv * _silu(z_ref[:, gs])
        yf = yf * lax.rsqrt(jnp.mean(yf * yf, axis=-1, keepdims=True) + SSM_NORM_EPS)
        y_ref[:, gs] = yf * nw_ref[:, gs]

        dend = f_s[(1 + n_tok) * TPAD:(2 + n_tok) * TPAD, gs]
        pend = f_s[TPAD + n_tok - 1:TPAD + n_tok, gs]
        lhs = jnp.where(row8 == n_tok, pend, dend * xdt)
        rhs = jnp.concatenate([jnp.where(rid < n_tok, b_f, 0.0), jnp.where(rid == n_tok, 1.0, 0.0)], axis=1)
        res = lax.dot_general(lhs, rhs, (((0,), (0,)), ((), ())), preferred_element_type=F32, precision=hi)
        hout_ref[gs, :] = res[:, n_st:] * h0g + res[:, :n_st]


def ssd_sample(xbc, z, dt_raw, conv_state, h0, conv_w, conv_b, dtb, alog, dsk_e, nw, expand, n_tok):
    bsz, _, conv_dim = xbc.shape
    d_inner = z.shape[2]
    row = lambda w: pl.BlockSpec((1, w), lambda b: (0, 0))
    tok = lambda w: pl.BlockSpec((None, TPAD, w), lambda b: (b, 0, 0))
    st = pl.BlockSpec((None, d_inner, SSM_STATE), lambda b: (b, 0, 0))
    return pl.pallas_call(
        functools.partial(_ssd_step_kernel, d_inner=d_inner, n_groups=SSM_GROUPS, n_tok=n_tok),
        grid=(bsz,),
        in_specs=[tok(conv_dim), tok(d_inner), tok(LANES),
                  pl.BlockSpec((None, SSM_CONV - 1, conv_dim), lambda b: (b, 0, 0)), st,
                  pl.BlockSpec((SSM_CONV, conv_dim), lambda b: (0, 0)), row(conv_dim),
                  row(LANES), row(LANES), row(d_inner), row(d_inner),
                  pl.BlockSpec((LANES, d_inner), lambda b: (0, 0))],
        out_specs=[tok(d_inner), st],
        out_shape=[jax.ShapeDtypeStruct((bsz, TPAD, d_inner), F32),
                   jax.ShapeDtypeStruct((bsz, d_inner, SSM_STATE), F32)],
        scratch_shapes=[pltpu.VMEM((8 + TPAD, conv_dim), F32),
                        pltpu.VMEM((TPAD, conv_dim), F32),
                        pltpu.VMEM(((2 + n_tok) * TPAD, d_inner), F32)],
        compiler_params=_cparams(1), name="ssd_sample")(
            xbc, z, dt_raw, conv_state, h0, conv_w, conv_b, dtb, alog, dsk_e, nw, expand)


HEADS_PER_STEP = 4


def _cache_attn_kernel(new_ref, c0_ref, c1_ref, c2_ref, o_ref, n0_ref, n1_ref, n2_ref, stage, tts, *, n_tok):
    first = jnp.logical_and(pl.program_id(0) == 0, pl.program_id(1) == 0)

    @pl.when(first)
    def _():
        stage[...] = jnp.zeros(stage.shape, F32)

    crefs = (c0_ref, c1_ref, c2_ref)
    orefs = (n0_ref, n1_ref, n2_ref)
    hd = ATT_HEAD_DIM
    nt = (((1,), (1,)), ((), ()))
    lane = lax.broadcasted_iota(jnp.int32, (hd, LANES), 1)
    t_new = lax.broadcasted_iota(jnp.int32, (TPAD, TPAD), 0)
    u_new = lax.broadcasted_iota(jnp.int32, (TPAD, TPAD), 1)

    for head in range(HEADS_PER_STEP):
        hs = slice(head * hd, (head + 1) * hd)
        parts = []
        for g, d in enumerate(ATT_DILATIONS):
            w = crefs[g].shape[-1]
            q = new_ref[g, 0, :, hs].astype(BF16)
            kn = new_ref[g, 1, :, hs].astype(BF16)
            vn = new_ref[g, 2, :, hs].astype(BF16)
            t_id = lax.broadcasted_iota(jnp.int32, (TPAD, w), 0)
            r_id = lax.broadcasted_iota(jnp.int32, (TPAD, w), 1)
            s = jnp.dot(q, crefs[g][0, head].astype(BF16), preferred_element_type=F32)
            s = jnp.where((r_id >= t_id) if d == 1 else ((r_id & (d - 1)) == t_id), s, NEG_INF)
            sn = lax.dot_general(q, kn, nt, preferred_element_type=F32)
            sn = jnp.where((u_new <= t_new) if d == 1 else (u_new == t_new), sn, NEG_INF)
            m = jnp.maximum(jnp.max(s, axis=1, keepdims=True), jnp.max(sn, axis=1, keepdims=True))
            p = jnp.exp(s - m)
            pn = jnp.exp(sn - m)
            l = jnp.sum(p, axis=1, keepdims=True) + jnp.sum(pn, axis=1, keepdims=True)
            num = (lax.dot_general(p.astype(BF16), crefs[g][1, head].astype(BF16), nt,
                                   preferred_element_type=F32)
                   + jnp.dot(pn.astype(BF16), vn, preferred_element_type=F32))
            parts.append((num, m, l))
        mx = jnp.maximum(jnp.maximum(parts[0][1], parts[1][1]), parts[2][1])
        wg = [jnp.exp(p_[1] - mx) for p_ in parts]
        num = wg[0] * parts[0][0]
        for g in range(1, 3):
            num = num + wg[g] * parts[g][0]
        den = wg[0] * parts[0][2] + wg[1] * parts[1][2] + wg[2] * parts[2][2]
        o_ref[:, hs] = num / den

    for pair in range(HEADS_PER_STEP // 2):
        ps = slice(pair * LANES, (pair + 1) * LANES)
        for g in range(3):
            for kv in range(2):
                stage[0:TPAD, :] = new_ref[g, 1 + kv, :, ps]
                tts[2 * g + kv] = stage[...].T
        for hh in range(2):
            head = pair * 2 + hh
            hs = slice(hh * hd, (hh + 1) * hd)
            for g in range(3):
                n_col = crefs[g].shape[-1] // LANES
                for kv in range(2):
                    cur = pltpu.roll(crefs[g][kv, head, :, 0:LANES], LANES - n_tok, 1)
                    for j in range(n_col):
                        if j + 1 < n_col:
                            nxt = pltpu.roll(crefs[g][kv, head, :, (j + 1) * LANES:(j + 2) * LANES],
                                             LANES - n_tok, 1)
                        else:
                            nxt = pltpu.roll(tts[2 * g + kv, hs, :], LANES - n_tok, 1)
                        orefs[g][kv, head, :, j * LANES:(j + 1) * LANES] = jnp.where(
                            lane < LANES - n_tok, cur, nxt)
                        cur = nxt


def cache_attn(new, caches, n_tok):
    bsz = new.shape[2]
    hps = HEADS_PER_STEP
    cspec = lambda c: pl.BlockSpec((None, 2, hps, ATT_HEAD_DIM, c.shape[-1]), lambda b, h: (b, 0, h, 0, 0))
    return pl.pallas_call(
        functools.partial(_cache_attn_kernel, n_tok=n_tok), grid=(bsz, ATT_HEADS // hps),
        in_specs=[pl.BlockSpec((3, 3, None, TPAD, hps * ATT_HEAD_DIM), lambda b, h: (0, 0, b, 0, h))]
        + [cspec(c) for c in caches],
        out_specs=[pl.BlockSpec((None, TPAD, hps * ATT_HEAD_DIM), lambda b, h: (b, 0, h))]
        + [cspec(c) for c in caches],
        out_shape=[jax.ShapeDtypeStruct((bsz, TPAD, ATT_WIDTH), F32)]
        + [jax.ShapeDtypeStruct(c.shape, c.dtype) for c in caches],
        scratch_shapes=[pltpu.VMEM((LANES, LANES), F32), pltpu.VMEM((6, LANES, LANES), F32)],
        compiler_params=_cparams(2), name="cache_attn")(new, *caches)


def _router(logits):
    lanef = lax.broadcasted_iota(jnp.int32, logits.shape, 1).astype(F32)
    big = 1e9
    lc = jnp.where(lanef < MOE_GROUPS, logits, NEG_INF)
    mc = jnp.max(lc, axis=-1, keepdims=True)
    g_sel = jnp.min(jnp.where(lc == mc, lanef, big), axis=-1, keepdims=True)
    p_sel = 1.0 / jnp.sum(jnp.exp(lc - mc), axis=-1, keepdims=True)
    base = MOE_GROUPS + MOE_PER_GROUP * g_sel
    lf = jnp.where(jnp.logical_and(lanef >= base, lanef < base + MOE_PER_GROUP), logits, NEG_INF)
    v1 = jnp.max(lf, axis=-1, keepdims=True)
    i1 = jnp.min(jnp.where(lf == v1, lanef, big), axis=-1, keepdims=True)
    lf2 = jnp.where(lanef == i1, NEG_INF, lf)
    v2 = jnp.max(lf2, axis=-1, keepdims=True)
    i2 = jnp.min(jnp.where(lf2 == v2, lanef, big), axis=-1, keepdims=True)
    e2 = jnp.exp(v2 - v1)
    den = 1.0 + e2
    w1 = (1.0 / den) * p_sel
    w2 = (e2 / den) * p_sel
    out = jnp.where(lanef == 0.0, i1 - MOE_GROUPS, 0.0)
    out = jnp.where(lanef == 1.0, i2 - MOE_GROUPS, out)
    out = jnp.where(lanef == 2.0, w1, out)
    return jnp.where(lanef == 3.0, w2, out)


def _outproj_kernel(*refs, dils, tm):
    n_g = len(dils)
    if n_g:
        x_ref, g_ref, ys_ref = refs[:3]
        att_refs = refs[3:3 + 2 * n_g]
        rest = refs[3 + 2 * n_g:]
    else:
        x_ref, g_ref, ys_ref, attn_ref = refs[:4]
        rest = refs[4:]
    wa_ref, ws_ref, wo_ref, n2_ref, wr_ref, br_ref, x2_ref, xn_ref, cmb_ref = rest[:9]
    scr = rest[9:]

    if n_g:
        n_ch = ATT_WIDTH // LANES
        nums, stats = [], []
        for g, d in enumerate(dils):
            num_ref, st_ref = att_refs[2 * g], att_refs[2 * g + 1]
            if d == 1:
                nums.append([num_ref[0, :, ch * LANES:(ch + 1) * LANES] for ch in range(n_ch)])
                stats.append(st_ref[0])
            else:
                ns, ss = scr[2 * g], scr[2 * g + 1]
                for r in range(d):
                    for ch in range(n_ch):
                        ns[ch, pl.ds(r, tm // d, stride=d), :] = num_ref[r, :, ch * LANES:(ch + 1) * LANES]
                    ss[pl.ds(r, tm // d, stride=d), :] = st_ref[r]
                nums.append([ns[ch] for ch in range(n_ch)])
                stats.append(ss[...])
        lane = lax.broadcasted_iota(jnp.int32, (tm, LANES), 1)
        lo = lane < ATT_HEAD_DIM
        chunks = []
        for ch in range(ATT_WIDTH // LANES):
            wts, dens = [], []
            for h in (2 * ch, 2 * ch + 1):
                ms = [s[:, h:h + 1] for s in stats]
                ls = [s[:, ATT_HEADS + h:ATT_HEADS + h + 1] for s in stats]
                mx = ms[0]
                for m in ms[1:]:
                    mx = jnp.maximum(mx, m)
                w = [jnp.exp(m - mx) for m in ms]
                den = w[0] * ls[0]
                for g in range(1, n_g):
                    den = den + w[g] * ls[g]
                wts.append(w)
                dens.append(den)
            num = jnp.where(lo, wts[0][0], wts[1][0]) * nums[0][ch]
            for g in range(1, n_g):
                num = num + jnp.where(lo, wts[0][g], wts[1][g]) * nums[g][ch]
            chunks.append((num / jnp.where(lo, dens[0], dens[1])).astype(BF16))
        attn = jnp.concatenate(chunks, axis=1)
    else:
        attn = attn_ref[...].astype(BF16)

    dm = x_ref.shape[1]
    a = jnp.dot(attn, wa_ref[...], preferred_element_type=F32)
    s = jnp.dot(ys_ref[...].astype(BF16), ws_ref[...], preferred_element_type=F32)
    gates = g_ref[...].astype(F32)
    mixed = _sigmoid(gates[:, :dm]) * a + _sigmoid(gates[:, dm:]) * s
    x2 = x_ref[...] + jnp.dot(mixed.astype(BF16), wo_ref[...], preferred_element_type=F32)
    x2_ref[...] = x2
    xn = x2 * lax.rsqrt(jnp.mean(x2 * x2, axis=-1, keepdims=True) + RMS_EPS) * n2_ref[...]
    xn_ref[...] = xn
    logits = jnp.dot(xn.astype(BF16), wr_ref[...], preferred_element_type=F32) + br_ref[...]
    cmb_ref[...] = _router(logits)


def outproj(x, gates, yssm, att, dils, batch, seq, weights, tm):
    n, dm = x.shape
    wa, ws, wo, n2, wr, br = weights
    tpb = seq // tm
    tok = lambda w: pl.BlockSpec((tm, w), lambda i: (i, 0))
    full = lambda a: pl.BlockSpec(a.shape, lambda i: (0,) * a.ndim)
    in_specs = [tok(dm), tok(gates.shape[1]), tok(yssm.shape[1])]
    args = [x, gates, yssm]
    scratch = []
    if dils:
        for (num, st), d in zip(att, dils):
            for arr in (num, st):
                in_specs.append(pl.BlockSpec((None, d, tm // d, arr.shape[-1]),
                                             lambda i: (i // tpb, 0, i % tpb, 0)))
                args.append(arr)
                scratch.append(pltpu.VMEM((tm, LANES) if arr.shape[-1] == LANES
                                          else (arr.shape[-1] // LANES, tm, LANES), F32))
    else:
        in_specs.append(tok(att.shape[1]))
        args.append(att)
    in_specs += [full(wa), full(ws), full(wo), full(n2), full(wr), full(br)]
    args += [wa, ws, wo, n2, wr, br]
    return pl.pallas_call(
        functools.partial(_outproj_kernel, dils=tuple(dils), tm=tm), grid=(n // tm,),
        in_specs=in_specs,
        out_specs=[tok(dm), tok(dm), tok(LANES)],
        out_shape=[jax.ShapeDtypeStruct((n, dm), F32), jax.ShapeDtypeStruct((n, dm), F32),
                   jax.ShapeDtypeStruct((n, LANES), F32)],
        scratch_shapes=scratch,
        compiler_params=_cparams(1), name="outproj")(*args)


MOE_TILE = 256


def moe_plan(e1, e2, n_exp, tile):
    n = e1.shape[0]
    flat = jnp.stack([e1, e2], axis=1).reshape(-1)
    onehot = (flat[:, None] == jnp.arange(n_exp, dtype=jnp.int32)[None, :]).astype(jnp.int32)
    rank = jnp.sum((jnp.cumsum(onehot, axis=0) - onehot) * onehot, axis=1)
    counts = jnp.sum(onehot, axis=0)
    padded = (counts + tile - 1) // tile * tile
    ends = jnp.cumsum(padded)
    slot = (ends - padded)[flat] + rank
    n_tiles = (2 * n) // tile + n_exp
    row_token = jnp.zeros((n_tiles * tile,), jnp.int32).at[slot].set(jnp.arange(2 * n, dtype=jnp.int32) // 2)
    tile_start = jnp.arange(n_tiles, dtype=jnp.int32) * tile
    tile_expert = jnp.minimum(jnp.searchsorted(ends, tile_start, side="right"), n_exp - 1).astype(jnp.int32)
    return slot.reshape(n, 2), row_token.reshape(n_tiles, 1, tile), tile_expert


def _gather_rows(idx_cur, idx_next, src_hbm, buf, sem):
    i = pl.program_id(0)
    rows = buf.shape[1]
    slot = i % 2

    def row_copy(idx_ref, j, s):
        return pltpu.make_async_copy(src_hbm.at[pl.ds(idx_ref[0, j], 1), :], buf.at[s, pl.ds(j, 1), :], sem.at[s])

    def wait_rows(s):
        pltpu.make_async_copy(src_hbm.at[pl.ds(0, rows), :], buf.at[s], sem.at[s]).wait()

    @pl.when(i == 0)
    def _():
        def body(j, c):
            row_copy(idx_cur, j, 0).start()
            return c
        lax.fori_loop(0, rows, body, 0)

    wait_rows(slot)
    for j in range(rows):
        row_copy(idx_next, j, 1 - slot).start()
    return slot, lambda: wait_rows(1 - slot)


def _moe_group_kernel(te_ref, tokc_ref, tokn_ref, x_hbm, wg_ref, wu_ref, wd_ref, o_ref, buf, sem):
    del te_ref
    slot, drain = _gather_rows(tokc_ref, tokn_ref, x_hbm, buf, sem)
    x = buf[slot].astype(BF16)
    he = _silu(jnp.dot(x, wg_ref[...], preferred_element_type=F32)) * jnp.dot(
        x, wu_ref[...], preferred_element_type=F32)
    o_ref[...] = jnp.dot(he.astype(BF16), wd_ref[...], preferred_element_type=F32)
    pl.when(pl.program_id(0) == pl.num_programs(0) - 1)(drain)


def moe_grouped(x, row_token, tile_expert, wg, wu, wd):
    n_tiles, _, tile = row_token.shape
    dm = x.shape[1]
    ff = wg.shape[2]
    tok = lambda f: pl.BlockSpec((None, 1, tile), f, memory_space=pltpu.SMEM)
    grid_spec = pltpu.PrefetchScalarGridSpec(
        num_scalar_prefetch=1, grid=(n_tiles,),
        in_specs=[tok(lambda i, te: (i, 0, 0)),
                  tok(lambda i, te: (jnp.minimum(i + 1, n_tiles - 1), 0, 0)),
                  pl.BlockSpec(memory_space=pl.ANY),
                  pl.BlockSpec((None, dm, ff), lambda i, te: (te[i], 0, 0)),
                  pl.BlockSpec((None, dm, ff), lambda i, te: (te[i], 0, 0)),
                  pl.BlockSpec((None, ff, dm), lambda i, te: (te[i], 0, 0))],
        out_specs=pl.BlockSpec((tile, dm), lambda i, te: (i, 0)),
        scratch_shapes=[pltpu.VMEM((2, tile, dm), F32), pltpu.SemaphoreType.DMA((2,))])
    return pl.pallas_call(
        _moe_group_kernel, grid_spec=grid_spec,
        out_shape=jax.ShapeDtypeStruct((n_tiles * tile, dm), F32),
        compiler_params=_cparams(1), name="moe_grouped")(tile_expert, row_token, row_token, x, wg, wu, wd)


def _moe_combine_kernel(sc_ref, sn_ref, ys_hbm, x2_ref, rt_ref, nf_ref, o_ref, buf, sem, *, final_norm):
    slot, drain = _gather_rows(sc_ref, sn_ref, ys_hbm, buf, sem)
    tm = buf.shape[1] // 2
    rt = rt_ref[...]
    xf = x2_ref[...] + (rt[:, 2:3] * buf[slot, 0:tm, :] + rt[:, 3:4] * buf[slot, tm:2 * tm, :])
    if final_norm:
        xf = xf * lax.rsqrt(jnp.mean(xf * xf, axis=-1, keepdims=True) + RMS_EPS) * nf_ref[...]
    o_ref[...] = xf
    pl.when(pl.program_id(0) == pl.num_programs(0) - 1)(drain)


def moe_combine(ys, slots, x2, route, nf, tm, final_norm):
    n, dm = x2.shape
    nt = n // tm
    srows = slots.reshape(nt, tm, 2).transpose(0, 2, 1).reshape(nt, 1, 2 * tm)
    sspec = lambda f: pl.BlockSpec((None, 1, 2 * tm), f, memory_space=pltpu.SMEM)
    return pl.pallas_call(
        functools.partial(_moe_combine_kernel, final_norm=final_norm), grid=(nt,),
        in_specs=[sspec(lambda i: (i, 0, 0)), sspec(lambda i: (jnp.minimum(i + 1, nt - 1), 0, 0)),
                  pl.BlockSpec(memory_space=pl.ANY),
                  pl.BlockSpec((tm, dm), lambda i: (i, 0)),
                  pl.BlockSpec((tm, LANES), lambda i: (i, 0)),
                  pl.BlockSpec((1, dm), lambda i: (0, 0))],
        out_specs=pl.BlockSpec((tm, dm), lambda i: (i, 0)),
        out_shape=jax.ShapeDtypeStruct((n, dm), F32),
        scratch_shapes=[pltpu.VMEM((2, 2 * tm, dm), F32), pltpu.SemaphoreType.DMA((2,))],
        compiler_params=_cparams(1), name="moe_combine")(srows, srows, ys, x2, route, nf)


def moe_final(xn, route, x2, wg, wu, wd, nf, final_norm):
    e1 = route[:, 0].astype(jnp.int32)
    e2 = route[:, 1].astype(jnp.int32)
    slots, row_token, tile_expert = moe_plan(e1, e2, wg.shape[0], MOE_TILE)
    ys = moe_grouped(xn, row_token, tile_expert, wg, wu, wd)
    return moe_combine(ys, slots, x2, route, nf, min(MOE_TILE, x2.shape[0]), final_norm)


def _pad_lanes(v, width=LANES):
    return jnp.pad(v, ((0, 0), (0, width - v.shape[1])))


def _natural_rows(arr, n_rows):
    b, d, l, w = arr.shape
    tail = arr[:, :, l - n_rows // d:, :]
    return jnp.swapaxes(tail, 1, 2).reshape(b, n_rows, w)


def kernel(x_prompt, x_sample, cache_kv_w128, cache_kv_w512, cache_kv_w2048, state_ssm, state_conv,
           norm1, w_in, conv_w, conv_b, dt_bias, a_log, d_skip, ssm_norm, w_att_out, w_ssm_out, w_o,
           norm2, w_router_coarse, b_router_coarse, w_router_fine, b_router_fine,
           w_exp_gate, w_exp_up, w_exp_down, norm_f):
    bp, sp, dm = x_prompt.shape
    bs, ts, _ = x_sample.shape
    depth = w_in.shape[0]
    n_heads = dt_bias.shape[1]
    d_inner = n_heads * SSM_HEAD_DIM
    conv_dim = conv_w.shape[2]
    n_grp = len(ATT_DILATIONS)
    off_z = n_grp * 3 * ATT_WIDTH
    off_xbc = off_z + d_inner
    off_dt = off_xbc + conv_dim
    off_gate = off_dt + n_heads
    caches = (cache_kv_w128, cache_kv_w512, cache_kv_w2048)
    for g in range(n_grp):
        assert caches[g].shape[2] == ATT_WINDOWS[g] and sp % (ATT_DILATIONS[g] * ATT_BLOCK) == 0
    assert ts <= min(ATT_DILATIONS[1:]) and ts < TPAD and sp % SSM_CHUNK == 0

    tabs_p = rope_tables(jnp.arange(sp, dtype=F32))
    tabs_s = rope_tables(jnp.tile(jnp.arange(ts, dtype=F32) + PAST_LEN, bs))
    expand = (jnp.arange(LANES)[:, None] == (jnp.arange(d_inner) // SSM_HEAD_DIM)[None, :]).astype(F32)

    xp = x_prompt.reshape(bp * sp, dm)
    xs = x_sample.reshape(bs * ts, dm)
    outs = {k: [] for k in ("kvp0", "kvp1", "kvp2", "ssm_p", "conv_p", "kvs0", "kvs1", "kvs2", "ssm_s", "conv_s")}
    n_s = bs * ts
    for layer in range(depth):
        w = w_in[layer].astype(BF16)
        w_qkv = [w[:, g * 3 * ATT_WIDTH:(g + 1) * 3 * ATT_WIDTH] for g in range(n_grp)]
        w_z, w_xbc, w_gate = w[:, off_z:off_xbc], w[:, off_xbc:off_dt], w[:, off_gate:]
        w_dt = _pad_lanes(w[:, off_dt:off_gate])
        dtb = _pad_lanes(dt_bias[layer][None])
        alog = _pad_lanes(a_log[layer][None])
        dsk_e = jnp.repeat(d_skip[layer], SSM_HEAD_DIM)[None]
        nw = ssm_norm[layer][None]
        cw, cb = conv_w[layer], conv_b[layer][None]
        w_router = _pad_lanes(jnp.concatenate([w_router_coarse[layer], w_router_fine[layer]], axis=1)).astype(BF16)
        b_router = _pad_lanes(jnp.concatenate([b_router_coarse[layer], b_router_fine[layer]])[None])
        wts = (w_att_out[layer].astype(BF16), w_ssm_out[layer].astype(BF16), w_o[layer].astype(BF16),
               norm2[layer][None], w_router, b_router)
        wg, wu, wd = (w_exp_gate[layer].astype(BF16), w_exp_up[layer].astype(BF16),
                      w_exp_down[layer].astype(BF16))

        xn = rmsnorm_bf16(xp, norm1[layer], 1024)
        att = []
        for g, d in enumerate(ATT_DILATIONS):
            qkv = qkv_proj(xn, w_qkv[g], tabs_p, bp, sp, d, 1024, BF16, f"qkv_prompt_d{d}")
            att.append(attn_prompt(qkv))
            wnd = min(ATT_WINDOWS[g], sp)
            kv = jnp.stack([_natural_rows(qkv[1], wnd), _natural_rows(qkv[2], wnd)], axis=2)
            outs[f"kvp{g}"].append(kv.astype(F32).reshape(bp, wnd, 2, ATT_HEADS, ATT_HEAD_DIM))
        z = matmul(xn, w_z, BF16, 2048, 1024, "proj_z")
        xbc = matmul(xn, w_xbc, BF16, 2048, 1024, "proj_xbc")
        gates = matmul(xn, w_gate, BF16, 2048, 1024, "proj_gates")
        dt_raw = matmul(xn, w_dt, F32, 2048, LANES, "proj_dt")
        y_ssm, h_p = ssd_prompt(xbc, z, dt_raw, cw, cb, dtb, alog, dsk_e, nw, bp, sp)
        outs["ssm_p"].append(h_p.reshape(bp, n_heads, SSM_HEAD_DIM, SSM_STATE))
        outs["conv_p"].append(xbc.reshape(bp, sp, conv_dim)[:, sp - (SSM_CONV - 1):].astype(F32))
        x2, xn2, cmb = outproj(xp, gates, y_ssm, att, ATT_DILATIONS, bp, sp, wts, 256)
        xp = moe_final(xn2, cmb, x2, wg, wu, wd, norm_f[None], layer == depth - 1)

        xn = rmsnorm_bf16(xs, norm1[layer], n_s)
        pad_t = lambda a: jnp.pad(a.reshape(bs, ts, a.shape[-1]), ((0, 0), (0, TPAD - ts), (0, 0)))
        new = jnp.stack([qkv_proj(xn, w_qkv[g], tabs_s, 1, n_s, 1, n_s, F32, f"qkv_sample_{g}")
                         .reshape(3, bs, ts, ATT_WIDTH) for g in range(n_grp)])
        new = jnp.pad(new, ((0, 0), (0, 0), (0, 0), (0, TPAD - ts), (0, 0)))
        cl = [jnp.transpose(caches[g][layer], (0, 2, 3, 4, 1)) for g in range(n_grp)]
        res = cache_attn(new, cl, ts)
        attn_s = res[0][:, :ts]
        for g in range(n_grp):
            outs[f"kvs{g}"].append(jnp.transpose(res[1 + g], (0, 4, 1, 2, 3)))
        z = matmul(xn, w_z, F32, n_s, 512, "proj_z_s")
        xbc = matmul(xn, w_xbc, F32, n_s, 512, "proj_xbc_s")
        gates = matmul(xn, w_gate, F32, n_s, 512, "proj_gates_s")
        dt_raw = matmul(xn, w_dt, F32, n_s, LANES, "proj_dt_s")
        y_s, h_s = ssd_sample(pad_t(xbc), pad_t(z), pad_t(dt_raw), state_conv[layer],
                              state_ssm[layer].reshape(bs, d_inner, SSM_STATE), cw, cb, dtb, alog, dsk_e, nw,
                              expand, ts)
        outs["ssm_s"].append(h_s.reshape(bs, n_heads, SSM_HEAD_DIM, SSM_STATE))
        hist = jnp.concatenate([state_conv[layer], xbc.reshape(bs, ts, conv_dim)], axis=1)
        outs["conv_s"].append(hist[:, ts:])
        y_s = y_s[:, :ts].reshape(n_s, d_inner)
        x2, xn2, cmb = outproj(xs, gates, y_s, attn_s.reshape(n_s, ATT_WIDTH), (), 1, n_s, wts, min(256, n_s))
        xs = moe_final(xn2, cmb, x2, wg, wu, wd, norm_f[None], layer == depth - 1)

    st = lambda k: jnp.stack(outs[k])
    return (xp.reshape(bp, sp, dm), xs.reshape(bs, ts, dm),
            st("kvp0"), st("kvp1"), st("kvp2"), st("ssm_p"), st("conv_p"),
            st("kvs0"), st("kvs1"), st("kvs2"), st("ssm_s"), st("conv_s"))
```

```python
import functools
import math

import jax
import jax.numpy as jnp
from jax import lax
from jax.experimental import pallas as pl
from jax.experimental.pallas import tpu as pltpu

F32 = jnp.float32
BF16 = jnp.bfloat16

PAST_LEN = 8192
ATT_WINDOWS = (128, 512, 2048)
ATT_DILATIONS = (1, 4, 16)
ATT_HEADS = 8
ATT_HEAD_DIM = 64
ATT_WIDTH = ATT_HEADS * ATT_HEAD_DIM
ATT_SCALE = ATT_HEAD_DIM ** -0.5
ROT_DIM = ATT_HEAD_DIM // 4
ROPE_THETA = 500000.0
ATT_BLOCK = 128

SSM_HEAD_DIM = 64
SSM_STATE = 128
SSM_GROUPS = 4
SSM_CONV = 4
SSM_CHUNK = 128
MOE_GROUPS = 4
MOE_PER_GROUP = 4
MOE_EXPERTS = MOE_GROUPS * MOE_PER_GROUP
RMS_EPS = 1e-6
SSM_NORM_EPS = 1e-5

LANES = 128
VMEM_LIMIT = 56 * 1024 * 1024
NEG_INF = float("-inf")


def _cparams(n_axes):
    return pltpu.CompilerParams(dimension_semantics=("arbitrary",) * n_axes,
                                vmem_limit_bytes=VMEM_LIMIT)


def _sigmoid(x):
    return 1.0 / (1.0 + jnp.exp(-x))


def _silu(x):
    return x * _sigmoid(x)


def _norm_kernel(x_ref, g_ref, o_ref):
    x = x_ref[...]
    y = x * lax.rsqrt(jnp.mean(x * x, axis=-1, keepdims=True) + RMS_EPS)
    o_ref[...] = (y * g_ref[...]).astype(o_ref.dtype)


def rmsnorm_bf16(x, g, tm):
    n, d = x.shape
    return pl.pallas_call(
        _norm_kernel, grid=(n // tm,),
        in_specs=[pl.BlockSpec((tm, d), lambda i: (i, 0)), pl.BlockSpec((1, d), lambda i: (0, 0))],
        out_specs=pl.BlockSpec((tm, d), lambda i: (i, 0)),
        out_shape=jax.ShapeDtypeStruct((n, d), BF16),
        compiler_params=_cparams(1), name="rmsnorm")(x, g.reshape(1, d))


def _mm_kernel(x_ref, w_ref, o_ref):
    o_ref[...] = jnp.dot(x_ref[...], w_ref[...], preferred_element_type=F32).astype(o_ref.dtype)


def matmul(x, w, out_dtype, tm, tn, name):
    m, k = x.shape
    n = w.shape[1]
    return pl.pallas_call(
        _mm_kernel, grid=(m // tm, n // tn),
        in_specs=[pl.BlockSpec((tm, k), lambda i, j: (i, 0)), pl.BlockSpec((k, tn), lambda i, j: (0, j))],
        out_specs=pl.BlockSpec((tm, tn), lambda i, j: (i, j)),
        out_shape=jax.ShapeDtypeStruct((m, n), out_dtype),
        compiler_params=_cparams(2), name=name)(x, w)


QKV_CHUNK = 256


def _qkv_kernel(x_ref, w_ref, cos_ref, sa_ref, sb_ref, o_ref, acc_ref, *, d):
    sc = jnp.where(pl.program_id(1) == 0, ATT_SCALE, 1.0).astype(F32)
    tm = x_ref.shape[0]
    ck = min(QKV_CHUNK, tm)
    for c0 in range(0, tm, ck):
        rows = slice(c0, c0 + ck)
        acc = jnp.dot(x_ref[rows, :], w_ref[...], preferred_element_type=F32)
        c = cos_ref[rows, :] * sc
        sa = sa_ref[rows, :] * sc
        sb = sb_ref[rows, :] * sc
        for ch in range(ATT_WIDTH // LANES):
            cs = slice(ch * LANES, (ch + 1) * LANES)
            t = acc[:, cs]
            r = t * c + pltpu.roll(t, LANES - ROT_DIM // 2, 1) * sa + pltpu.roll(t, ROT_DIM // 2, 1) * sb
            if d == 1:
                o_ref[0, rows, cs] = r.astype(o_ref.dtype)
            else:
                acc_ref[ch, rows, :] = r
                for res in range(d):
                    o_ref[res, c0 // d:(c0 + ck) // d, cs] = acc_ref[
                        ch, pl.ds(c0 + res, ck // d, stride=d), :].astype(o_ref.dtype)


def qkv_proj(xn, w, tabs, batch, seq, d, tm, out_dtype, name):
    n, dm = xn.shape
    tpb = seq // tm
    cos, sa, sb = tabs
    tab_spec = pl.BlockSpec((None, tm, LANES), lambda i, j: (jnp.where(j == 2, 1, 0), i % tpb, 0))
    return pl.pallas_call(
        functools.partial(_qkv_kernel, d=d), grid=(n // tm, 3),
        in_specs=[pl.BlockSpec((tm, dm), lambda i, j: (i, 0)),
                  pl.BlockSpec((dm, ATT_WIDTH), lambda i, j: (0, j)),
                  tab_spec, tab_spec, tab_spec],
        out_specs=pl.BlockSpec((None, None, d, tm // d, ATT_WIDTH),
                               lambda i, j: (j, i // tpb, 0, i % tpb, 0)),
        out_shape=jax.ShapeDtypeStruct((3, batch, d, seq // d, ATT_WIDTH), out_dtype),
        scratch_shapes=[pltpu.VMEM((ATT_WIDTH // LANES, tm, LANES), F32)],
        compiler_params=_cparams(2), name=name)(xn, w, cos, sa, sb)


def rope_tables(pos):
    half = ROT_DIM // 2
    inv = ROPE_THETA ** (-jnp.arange(half, dtype=F32) / half)
    ang = pos[:, None] * inv[None, :]
    cos = jnp.cos(ang)
    sin = jnp.sin(ang)
    s = pos.shape[0]
    ones = jnp.ones((s, ATT_HEAD_DIM - ROT_DIM), F32)
    zeros = jnp.zeros((s, ATT_HEAD_DIM - ROT_DIM), F32)
    zh = jnp.zeros((s, half), F32)
    c = jnp.concatenate([cos, cos, ones], axis=1)
    sa = jnp.concatenate([-sin, zh, zeros], axis=1)
    sb = jnp.concatenate([zh, sin, zeros], axis=1)
    rep = LANES // ATT_HEAD_DIM
    c, sa, sb = (jnp.tile(t, (1, rep)) for t in (c, sa, sb))
    return (jnp.stack([c, jnp.ones_like(c)]), jnp.stack([sa, jnp.zeros_like(sa)]),
            jnp.stack([sb, jnp.zeros_like(sb)]))


ATT_ROW_CHUNK = 32


def _attn_kernel(q_ref, kc_ref, vc_ref, kp_ref, vp_ref, num_ref, st_ref, s_scr, p_scr):
    nb = pl.program_id(2)
    blk = q_ref.shape[0]
    nt = (((1,), (1,)), ((), ()))
    heads = [slice(h * ATT_HEAD_DIM, (h + 1) * ATT_HEAD_DIM) for h in range(ATT_HEADS)]
    for h, sl in enumerate(heads):
        q = q_ref[:, sl]
        s_scr[h, :, 0:blk] = lax.dot_general(q, kp_ref[:, sl], nt, preferred_element_type=F32)
        s_scr[h, :, blk:2 * blk] = lax.dot_general(q, kc_ref[:, sl], nt, preferred_element_type=F32)
    rc = ATT_ROW_CHUNK
    qi = lax.broadcasted_iota(jnp.int32, (rc, 2 * blk), 0)
    kj = lax.broadcasted_iota(jnp.int32, (rc, 2 * blk), 1)
    lane = lax.broadcasted_iota(jnp.int32, (rc, LANES), 1)
    for r0 in range(0, blk, rc):
        qa = qi + r0
        mask = jnp.logical_or(jnp.logical_and(jnp.logical_and(kj < blk, kj >= qa), nb > 0),
                              jnp.logical_and(kj >= blk, kj - blk <= qa))
        st = jnp.zeros((rc, LANES), F32)
        for h in range(ATT_HEADS):
            s = jnp.where(mask, s_scr[h, r0:r0 + rc, :], NEG_INF)
            m = jnp.max(s, axis=-1, keepdims=True)
            p = jnp.exp(s - m)
            l = jnp.sum(p, axis=-1, keepdims=True)
            p_scr[h, r0:r0 + rc, :] = p.astype(BF16)
            st = jnp.where(lane == h, m, st)
            st = jnp.where(lane == ATT_HEADS + h, l, st)
        st_ref[r0:r0 + rc, :] = st
    for h, sl in enumerate(heads):
        num_ref[:, sl] = (jnp.dot(p_scr[h, :, 0:blk], vp_ref[:, sl], preferred_element_type=F32)
                          + jnp.dot(p_scr[h, :, blk:2 * blk], vc_ref[:, sl], preferred_element_type=F32))


def attn_prompt(qkv):
    _, b, d, l, w = qkv.shape
    nblk = l // ATT_BLOCK

    def spec(kind, prev):
        if prev:
            return pl.BlockSpec((None, None, None, ATT_BLOCK, w),
                                lambda bi, r, n: (kind, bi, r, jnp.maximum(n - 1, 0), 0))
        return pl.BlockSpec((None, None, None, ATT_BLOCK, w), lambda bi, r, n: (kind, bi, r, n, 0))

    return pl.pallas_call(
        _attn_kernel, grid=(b, d, nblk),
        in_specs=[spec(0, False), spec(1, False), spec(2, False), spec(1, True), spec(2, True)],
        out_specs=[pl.BlockSpec((None, None, ATT_BLOCK, w), lambda bi, r, n: (bi, r, n, 0)),
                   pl.BlockSpec((None, None, ATT_BLOCK, LANES), lambda bi, r, n: (bi, r, n, 0))],
        out_shape=[jax.ShapeDtypeStruct((b, d, l, w), F32), jax.ShapeDtypeStruct((b, d, l, LANES), F32)],
        scratch_shapes=[pltpu.VMEM((ATT_HEADS, ATT_BLOCK, 2 * ATT_BLOCK), F32),
                        pltpu.VMEM((ATT_HEADS, ATT_BLOCK, 2 * ATT_BLOCK), BF16)],
        compiler_params=_cparams(3), name=f"attn_prompt_d{d}")(qkv, qkv, qkv, qkv, qkv)


def _col(mat, h, n_lanes=LANES):
    return jnp.broadcast_to(mat[:, h:h + 1], (mat.shape[0], n_lanes))


def _ssd_kernel(xbc_ref, z_ref, dt_ref, cw_ref, cb_ref, dtb_ref, alog_ref, dsk_ref, nw_ref,
                y_ref, hout_ref,
                cbuf, xs_s, xw_s, eac_s, cd_s, y_s, ht_s, *, d_inner, n_groups):
    c = pl.program_id(1)
    nc = pl.num_programs(1)
    q = SSM_CHUNK
    n_st = SSM_STATE
    gw = d_inner // n_groups
    conv_dim = xbc_ref.shape[1]
    top = 8

    @pl.when(c == 0)
    def _():
        cbuf[0:top, :] = jnp.zeros((top, conv_dim), F32)
        ht_s[...] = jnp.zeros(ht_s.shape, F32)

    cbuf[top:top + q, :] = xbc_ref[...].astype(F32)
    cwid = 512
    for j in range(conv_dim // cwid):
        cs = slice(j * cwid, (j + 1) * cwid)
        acc = cb_ref[:, cs] + cbuf[top - 3:top - 3 + q, cs] * cw_ref[0:1, cs]
        for i in range(1, SSM_CONV):
            acc = acc + cbuf[top - 3 + i:top - 3 + i + q, cs] * cw_ref[i:i + 1, cs]
        xs_s[:, cs] = _silu(acc)
    cbuf[top - 3:top, :] = cbuf[top + q - 3:top + q, :]

    dt = jax.nn.softplus(dt_ref[...] + dtb_ref[...])
    a_row = -jnp.exp(alog_ref[...])
    a = dt * a_row
    ri = lax.broadcasted_iota(jnp.int32, (q, q), 0)
    ci = lax.broadcasted_iota(jnp.int32, (q, q), 1)
    causal = ri >= ci
    tril = jnp.where(causal, 1.0, 0.0).astype(F32)
    acum = jnp.dot(tril, a, preferred_element_type=F32, precision=lax.Precision.HIGHEST)
    acum_t = acum.T
    lane = lax.broadcasted_iota(jnp.int32, (q, LANES), 1)
    lo = lane < SSM_HEAD_DIM

    hpg = gw // SSM_HEAD_DIM
    for g in range(n_groups):
        bc = xs_s[:, d_inner + g * n_st:d_inner + (g + 1) * n_st].astype(BF16)
        cc = xs_s[:, d_inner + n_groups * n_st + g * n_st:d_inner + n_groups * n_st + (g + 1) * n_st].astype(BF16)
        cb = lax.dot_general(cc, bc, (((1,), (1,)), ((), ())), preferred_element_type=F32)
        for jp in range(hpg // 2):
            h0 = g * hpg + 2 * jp
            ls = slice(g * gw + jp * LANES, g * gw + (jp + 1) * LANES)
            ac0 = _col(acum, h0)
            ac1 = _col(acum, h0 + 1)
            acum_e = jnp.where(lo, ac0, ac1)
            dt_e = jnp.where(lo, _col(dt, h0), _col(dt, h0 + 1))
            xdt = xs_s[:, ls] * dt_e
            acl_e = acum_e[q - 1:q, :]
            xw_s[:, ls] = (xdt * jnp.exp(acl_e - acum_e)).astype(BF16)
            eac_s[:, ls] = jnp.exp(acum_e)
            cd_s[:, ls] = jnp.exp(acl_e)
            xdt_b = xdt.astype(BF16)
            zero = jnp.zeros_like(xdt_b)
            m0 = (jnp.exp(jnp.where(causal, ac0 - acum_t[h0:h0 + 1, :], NEG_INF)) * cb).astype(BF16)
            m1 = (jnp.exp(jnp.where(causal, ac1 - acum_t[h0 + 1:h0 + 2, :], NEG_INF)) * cb).astype(BF16)
            y_s[:, ls] = (jnp.dot(m0, jnp.where(lo, xdt_b, zero), preferred_element_type=F32)
                          + jnp.dot(m1, jnp.where(lo, zero, xdt_b), preferred_element_type=F32))
        gs = slice(g * gw, (g + 1) * gw)
        h_prev = ht_s[g]
        y_off = jnp.dot(cc, h_prev.astype(BF16), preferred_element_type=F32) * eac_s[:, gs]
        y_s[:, gs] = y_s[:, gs] + y_off
        st = lax.dot_general(bc, xw_s[:, gs], (((0,), (0,)), ((), ())), preferred_element_type=F32)
        ht_s[g] = cd_s[:, gs] * h_prev + st

    for g in range(n_groups):
        gs = slice(g * gw, (g + 1) * gw)
        y = y_s[:, gs] + dsk_ref[:, gs] * xs_s[:, gs]
        yf = y * _silu(z_ref[:, gs].astype(F32))
        yf = yf * lax.rsqrt(jnp.mean(yf * yf, axis=-1, keepdims=True) + SSM_NORM_EPS)
        y_ref[:, gs] = (yf * nw_ref[:, gs]).astype(y_ref.dtype)

    @pl.when(c == nc - 1)
    def _():
        for g in range(n_groups):
            hout_ref[g * gw:(g + 1) * gw, :] = ht_s[g].T


def ssd_prompt(xbc, z, dt_raw, conv_w, conv_b, dtb, alog, dsk_e, nw, batch, seq):
    n, conv_dim = xbc.shape
    d_inner = z.shape[1]
    n_groups = SSM_GROUPS
    gw = d_inner // n_groups
    q = SSM_CHUNK
    cps = seq // q
    row = lambda w: pl.BlockSpec((1, w), lambda b, c: (0, 0))
    tok = lambda w: pl.BlockSpec((q, w), lambda b, c: (b * cps + c, 0))
    return pl.pallas_call(
        functools.partial(_ssd_kernel, d_inner=d_inner, n_groups=n_groups),
        grid=(batch, cps),
        in_specs=[tok(conv_dim), tok(d_inner), tok(LANES),
                  pl.BlockSpec((SSM_CONV, conv_dim), lambda b, c: (0, 0)), row(conv_dim),
                  row(LANES), row(LANES), row(d_inner), row(d_inner)],
        out_specs=[tok(d_inner), pl.BlockSpec((None, d_inner, SSM_STATE), lambda b, c: (b, 0, 0))],
        out_shape=[jax.ShapeDtypeStruct((n, d_inner), BF16),
                   jax.ShapeDtypeStruct((batch, d_inner, SSM_STATE), F32)],
        scratch_shapes=[pltpu.VMEM((8 + q, conv_dim), F32),
                        pltpu.VMEM((q, conv_dim), F32),
                        pltpu.VMEM((q, d_inner), BF16),
                        pltpu.VMEM((q, d_inner), F32),
                        pltpu.VMEM((1, d_inner), F32),
                        pltpu.VMEM((q, d_inner), F32),
                        pltpu.VMEM((n_groups, SSM_STATE, gw), F32)],
        compiler_params=_cparams(2), name="ssd_prompt")(xbc, z, dt_raw, conv_w, conv_b, dtb, alog, dsk_e, nw)


TPAD = 8


def _ssd_step_kernel(xbc_ref, z_ref, dt_ref, cst_ref, h0_ref, cw_ref, cb_ref, dtb_ref, alog_ref,
                     dsk_ref, nw_ref, exp_ref, y_ref, hout_ref,
                     cbuf, xs_s, f_s, *, d_inner, n_groups, n_tok):
    b = pl.program_id(0)
    n_st = SSM_STATE
    gw = d_inner // n_groups
    conv_dim = xbc_ref.shape[1]
    top = 8
    hist = SSM_CONV - 1

    @pl.when(b == 0)
    def _():
        cbuf[0:top, :] = jnp.zeros((top, conv_dim), F32)

    cbuf[top - hist:top, :] = cst_ref[...]
    cbuf[top:top + TPAD, :] = xbc_ref[...]
    acc = cb_ref[...] + cbuf[top - hist:top - hist + TPAD, :] * cw_ref[0:1, :]
    for i in range(1, SSM_CONV):
        acc = acc + cbuf[top - hist + i:top - hist + i + TPAD, :] * cw_ref[i:i + 1, :]
    xs_s[...] = _silu(acc)

    rid = lax.broadcasted_iota(jnp.int32, (TPAD, LANES), 0)
    dt = jax.nn.softplus(dt_ref[...] + dtb_ref[...])
    da = dt * (-jnp.exp(alog_ref[...]))
    cum = da
    for k in range(1, n_tok):
        cum = cum + jnp.where(rid >= k, pltpu.roll(da, k, 0), 0.0)
    facs = [dt, jnp.exp(cum)]
    for t in range(n_tok):
        facs.append(jnp.where(rid <= t, jnp.exp(cum[t:t + 1, :] - cum), 0.0))
    hi = lax.Precision.HIGHEST
    f_s[...] = jnp.dot(jnp.concatenate(facs, axis=0), exp_ref[...], preferred_element_type=F32, precision=hi)

    nt = (((1,), (1,)), ((), ()))
    row8 = lax.broadcasted_iota(jnp.int32, (TPAD, gw), 0)
    for g in range(n_groups):
        gs = slice(g * gw, (g + 1) * gw)
        xdt = xs_s[:, gs] * f_s[0:TPAD, gs]
        b_f = xs_s[:, d_inner + g * n_st:d_inner + (g + 1) * n_st]
        c_f = xs_s[:, d_inner + (n_groups + g) * n_st:d_inner + (n_groups + g + 1) * n_st]
        b16 = b_f.astype(BF16)
        c16 = c_f.astype(BF16)
        h0g = h0_ref[gs, :]
        y = f_s[TPAD:2 * TPAD, gs] * lax.dot_general(c16, h0g.astype(BF16), nt, preferred_element_type=F32)
        bc = lax.dot_general(b16, c16, nt, preferred_element_type=F32)
        for t in range(n_tok):
            term = bc[:, t:t + 1] * f_s[(2 + t) * TPAD:(3 + t) * TPAD, gs] * xdt
            y = y + jnp.where(row8 == t, jnp.sum(term, axis=0, keepdims=True), 0.0)
        yv = y + dsk_ref[:, gs] * xs_s[:, gs]
        yf = yv * _silu(z_ref[:, gs])
        yf = yf * lax.rsqrt(jnp.mean(yf * yf, axis=-1, keepdims=True) + SSM_NORM_EPS)
        y_ref[:, gs] = yf * nw_ref[:, gs]

        dend = f_s[(1 + n_tok) * TPAD:(2 + n_tok) * TPAD, gs]
        pend = f_s[TPAD + n_tok - 1:TPAD + n_tok, gs]
        lhs = jnp.where(row8 == n_tok, pend, dend * xdt)
        rhs = jnp.concatenate([jnp.where(rid < n_tok, b_f, 0.0), jnp.where(rid == n_tok, 1.0, 0.0)], axis=1)
        res = lax.dot_general(lhs, rhs, (((0,), (0,)), ((), ())), preferred_element_type=F32, precision=hi)
        hout_ref[gs, :] = res[:, n_st:] * h0g + res[:, :n_st]


def ssd_sample(xbc, z, dt_raw, conv_state, h0, conv_w, conv_b, dtb, alog, dsk_e, nw, expand, n_tok):
    bsz, _, conv_dim = xbc.shape
    d_inner = z.shape[2]
    row = lambda w: pl.BlockSpec((1, w), lambda b: (0, 0))
    tok = lambda w: pl.BlockSpec((None, TPAD, w), lambda b: (b, 0, 0))
    st = pl.BlockSpec((None, d_inner, SSM_STATE), lambda b: (b, 0, 0))
    return pl.pallas_call(
        functools.partial(_ssd_step_kernel, d_inner=d_inner, n_groups=SSM_GROUPS, n_tok=n_tok),
        grid=(bsz,),
        in_specs=[tok(conv_dim), tok(d_inner), tok(LANES),
                  pl.BlockSpec((None, SSM_CONV - 1, conv_dim), lambda b: (b, 0, 0)), st,
                  pl.BlockSpec((SSM_CONV, conv_dim), lambda b: (0, 0)), row(conv_dim),
                  row(LANES), row(LANES), row(d_inner), row(d_inner),
                  pl.BlockSpec((LANES, d_inner), lambda b: (0, 0))],
        out_specs=[tok(d_inner), st],
        out_shape=[jax.ShapeDtypeStruct((bsz, TPAD, d_inner), F32),
                   jax.ShapeDtypeStruct((bsz, d_inner, SSM_STATE), F32)],
        scratch_shapes=[pltpu.VMEM((8 + TPAD, conv_dim), F32),
                        pltpu.VMEM((TPAD, conv_dim), F32),
                        pltpu.VMEM(((2 + n_tok) * TPAD, d_inner), F32)],
        compiler_params=_cparams(1), name="ssd_sample")(
            xbc, z, dt_raw, conv_state, h0, conv_w, conv_b, dtb, alog, dsk_e, nw, expand)


HEADS_PER_STEP = 4


def _cache_attn_kernel(new_ref, c0_ref, c1_ref, c2_ref, o_ref, n0_ref, n1_ref, n2_ref, stage, tts, *, n_tok):
    first = jnp.logical_and(pl.program_id(0) == 0, pl.program_id(1) == 0)

    @pl.when(first)
    def _():
        stage[...] = jnp.zeros(stage.shape, F32)

    crefs = (c0_ref, c1_ref, c2_ref)
    orefs = (n0_ref, n1_ref, n2_ref)
    hd = ATT_HEAD_DIM
    nt = (((1,), (1,)), ((), ()))
    lane = lax.broadcasted_iota(jnp.int32, (hd, LANES), 1)
    t_new = lax.broadcasted_iota(jnp.int32, (TPAD, TPAD), 0)
    u_new = lax.broadcasted_iota(jnp.int32, (TPAD, TPAD), 1)

    for head in range(HEADS_PER_STEP):
        hs = slice(head * hd, (head + 1) * hd)
        parts = []
        for g, d in enumerate(ATT_DILATIONS):
            w = crefs[g].shape[-1]
            q = new_ref[g, 0, :, hs].astype(BF16)
            kn = new_ref[g, 1, :, hs].astype(BF16)
            vn = new_ref[g, 2, :, hs].astype(BF16)
            t_id = lax.broadcasted_iota(jnp.int32, (TPAD, w), 0)
            r_id = lax.broadcasted_iota(jnp.int32, (TPAD, w), 1)
            s = jnp.dot(q, crefs[g][0, head].astype(BF16), preferred_element_type=F32)
            s = jnp.where((r_id >= t_id) if d == 1 else ((r_id & (d - 1)) == t_id), s, NEG_INF)
            sn = lax.dot_general(q, kn, nt, preferred_element_type=F32)
            sn = jnp.where((u_new <= t_new) if d == 1 else (u_new == t_new), sn, NEG_INF)
            m = jnp.maximum(jnp.max(s, axis=1, keepdims=True), jnp.max(sn, axis=1, keepdims=True))
            p = jnp.exp(s - m)
            pn = jnp.exp(sn - m)
            l = jnp.sum(p, axis=1, keepdims=True) + jnp.sum(pn, axis=1, keepdims=True)
            num = (lax.dot_general(p.astype(BF16), crefs[g][1, head].astype(BF16), nt,
                                   preferred_element_type=F32)
                   + jnp.dot(pn.astype(BF16), vn, preferred_element_type=F32))
            parts.append((num, m, l))
        mx = jnp.maximum(jnp.maximum(parts[0][1], parts[1][1]), parts[2][1])
        wg = [jnp.exp(p_[1] - mx) for p_ in parts]
        num = wg[0] * parts[0][0]
        for g in range(1, 3):
            num = num + wg[g] * parts[g][0]
        den = wg[0] * parts[0][2] + wg[1] * parts[1][2] + wg[2] * parts[2][2]
        o_ref[:, hs] = num / den

    for pair in range(HEADS_PER_STEP // 2):
        ps = slice(pair * LANES, (pair + 1) * LANES)
        for g in range(3):
            for kv in range(2):
                stage[0:TPAD, :] = new_ref[g, 1 + kv, :, ps]
                tts[2 * g + kv] = stage[...].T
        for hh in range(2):
            head = pair * 2 + hh
            hs = slice(hh * hd, (hh + 1) * hd)
            for g in range(3):
                n_col = crefs[g].shape[-1] // LANES
                for kv in range(2):
                    cur = pltpu.roll(crefs[g][kv, head, :, 0:LANES], LANES - n_tok, 1)
                    for j in range(n_col):
                        if j + 1 < n_col:
                            nxt = pltpu.roll(crefs[g][kv, head, :, (j + 1) * LANES:(j + 2) * LANES],
                                             LANES - n_tok, 1)
                        else:
                            nxt = pltpu.roll(tts[2 * g + kv, hs, :], LANES - n_tok, 1)
                        orefs[g][kv, head, :, j * LANES:(j + 1) * LANES] = jnp.where(
                            lane < LANES - n_tok, cur, nxt)
                        cur = nxt


def cache_attn(new, caches, n_tok):
    bsz = new.shape[2]
    hps = HEADS_PER_STEP
    cspec = lambda c: pl.BlockSpec((None, 2, hps, ATT_HEAD_DIM, c.shape[-1]), lambda b, h: (b, 0, h, 0, 0))
    return pl.pallas_call(
        functools.partial(_cache_attn_kernel, n_tok=n_tok), grid=(bsz, ATT_HEADS // hps),
        in_specs=[pl.BlockSpec((3, 3, None, TPAD, hps * ATT_HEAD_DIM), lambda b, h: (0, 0, b, 0, h))]
        + [cspec(c) for c in caches],
        out_specs=[pl.BlockSpec((None, TPAD, hps * ATT_HEAD_DIM), lambda b, h: (b, 0, h))]
        + [cspec(c) for c in caches],
        out_shape=[jax.ShapeDtypeStruct((bsz, TPAD, ATT_WIDTH), F32)]
        + [jax.ShapeDtypeStruct(c.shape, c.dtype) for c in caches],
        scratch_shapes=[pltpu.VMEM((LANES, LANES), F32), pltpu.VMEM((6, LANES, LANES), F32)],
        compiler_params=_cparams(2), name="cache_attn")(new, *caches)


def _router(logits):
    lanef = lax.broadcasted_iota(jnp.int32, logits.shape, 1).astype(F32)
    big = 1e9
    lc = jnp.where(lanef < MOE_GROUPS, logits, NEG_INF)
    mc = jnp.max(lc, axis=-1, keepdims=True)
    g_sel = jnp.min(jnp.where(lc == mc, lanef, big), axis=-1, keepdims=True)
    p_sel = 1.0 / jnp.sum(jnp.exp(lc - mc), axis=-1, keepdims=True)
    base = MOE_GROUPS + MOE_PER_GROUP * g_sel
    lf = jnp.where(jnp.logical_and(lanef >= base, lanef < base + MOE_PER_GROUP), logits, NEG_INF)
    v1 = jnp.max(lf, axis=-1, keepdims=True)
    i1 = jnp.min(jnp.where(lf == v1, lanef, big), axis=-1, keepdims=True)
    lf2 = jnp.where(lanef == i1, NEG_INF, lf)
    v2 = jnp.max(lf2, axis=-1, keepdims=True)
    i2 = jnp.min(jnp.where(lf2 == v2, lanef, big), axis=-1, keepdims=True)
    e2 = jnp.exp(v2 - v1)
    den = 1.0 + e2
    w1 = (1.0 / den) * p_sel
    w2 = (e2 / den) * p_sel
    out = jnp.where(lanef == 0.0, i1 - MOE_GROUPS, 0.0)
    out = jnp.where(lanef == 1.0, i2 - MOE_GROUPS, out)
    out = jnp.where(lanef == 2.0, w1, out)
    return jnp.where(lanef == 3.0, w2, out)


def _outproj_kernel(*refs, dils, tm):
    n_g = len(dils)
    if n_g:
        x_ref, g_ref, ys_ref = refs[:3]
        att_refs = refs[3:3 + 2 * n_g]
        rest = refs[3 + 2 * n_g:]
    else:
        x_ref, g_ref, ys_ref, attn_ref = refs[:4]
        rest = refs[4:]
    wa_ref, ws_ref, wo_ref, n2_ref, wr_ref, br_ref, x2_ref, xn_ref, cmb_ref = rest[:9]
    scr = rest[9:]

    if n_g:
        n_ch = ATT_WIDTH // LANES
        nums, stats = [], []
        for g, d in enumerate(dils):
            num_ref, st_ref = att_refs[2 * g], att_refs[2 * g + 1]
            if d == 1:
                nums.append([num_ref[0, :, ch * LANES:(ch + 1) * LANES] for ch in range(n_ch)])
                stats.append(st_ref[0])
            else:
                ns, ss = scr[2 * g], scr[2 * g + 1]
                for r in range(d):
                    for ch in range(n_ch):
                        ns[ch, pl.ds(r, tm // d, stride=d), :] = num_ref[r, :, ch * LANES:(ch + 1) * LANES]
                    ss[pl.ds(r, tm // d, stride=d), :] = st_ref[r]
                nums.append([ns[ch] for ch in range(n_ch)])
                stats.append(ss[...])
        lane = lax.broadcasted_iota(jnp.int32, (tm, LANES), 1)
        lo = lane < ATT_HEAD_DIM
        chunks = []
        for ch in range(ATT_WIDTH // LANES):
            wts, dens = [], []
            for h in (2 * ch, 2 * ch + 1):
                ms = [s[:, h:h + 1] for s in stats]
                ls = [s[:, ATT_HEADS + h:ATT_HEADS + h + 1] for s in stats]
                mx = ms[0]
                for m in ms[1:]:
                    mx = jnp.maximum(mx, m)
                w = [jnp.exp(m - mx) for m in ms]
                den = w[0] * ls[0]
                for g in range(1, n_g):
                    den = den + w[g] * ls[g]
                wts.append(w)
                dens.append(den)
            num = jnp.where(lo, wts[0][0], wts[1][0]) * nums[0][ch]
            for g in range(1, n_g):
                num = num + jnp.where(lo, wts[0][g], wts[1][g]) * nums[g][ch]
            chunks.append((num / jnp.where(lo, dens[0], dens[1])).astype(BF16))
        attn = jnp.concatenate(chunks, axis=1)
    else:
        attn = attn_ref[...].astype(BF16)

    dm = x_ref.shape[1]
    a = jnp.dot(attn, wa_ref[...], preferred_element_type=F32)
    s = jnp.dot(ys_ref[...].astype(BF16), ws_ref[...], preferred_element_type=F32)
    gates = g_ref[...].astype(F32)
    mixed = _sigmoid(gates[:, :dm]) * a + _sigmoid(gates[:, dm:]) * s
    x2 = x_ref[...] + jnp.dot(mixed.astype(BF16), wo_ref[...], preferred_element_type=F32)
    x2_ref[...] = x2
    xn = x2 * lax.rsqrt(jnp.mean(x2 * x2, axis=-1, keepdims=True) + RMS_EPS) * n2_ref[...]
    _to_rows(xn_ref, xn)
    logits = jnp.dot(xn.astype(BF16), wr_ref[...], preferred_element_type=F32) + br_ref[...]
    cmb_ref[...] = _router(logits)


def outproj(x, gates, yssm, att, dils, batch, seq, weights, tm):
    n, dm = x.shape
    wa, ws, wo, n2, wr, br = weights
    tpb = seq // tm
    tok = lambda w: pl.BlockSpec((tm, w), lambda i: (i, 0))
    full = lambda a: pl.BlockSpec(a.shape, lambda i: (0,) * a.ndim)
    in_specs = [tok(dm), tok(gates.shape[1]), tok(yssm.shape[1])]
    args = [x, gates, yssm]
    scratch = []
    if dils:
        for (num, st), d in zip(att, dils):
            for arr in (num, st):
                in_specs.append(pl.BlockSpec((None, d, tm // d, arr.shape[-1]),
                                             lambda i: (i // tpb, 0, i % tpb, 0)))
                args.append(arr)
                scratch.append(pltpu.VMEM((tm, LANES) if arr.shape[-1] == LANES
                                          else (arr.shape[-1] // LANES, tm, LANES), F32))
    else:
        in_specs.append(tok(att.shape[1]))
        args.append(att)
    in_specs += [full(wa), full(ws), full(wo), full(n2), full(wr), full(br)]
    args += [wa, ws, wo, n2, wr, br]
    return pl.pallas_call(
        functools.partial(_outproj_kernel, dils=tuple(dils), tm=tm), grid=(n // tm,),
        in_specs=in_specs,
        out_specs=[tok(dm), pl.BlockSpec((tm * (dm // LANES), LANES), lambda i: (i, 0)), tok(LANES)],
        out_shape=[jax.ShapeDtypeStruct((n, dm), F32), jax.ShapeDtypeStruct((n * (dm // LANES), LANES), F32),
                   jax.ShapeDtypeStruct((n, LANES), F32)],
        scratch_shapes=scratch,
        compiler_params=_cparams(1), name="outproj")(*args)


MOE_TILE = 256


def moe_plan(e1, e2, n_exp, tile):
    n = e1.shape[0]
    flat = jnp.stack([e1, e2], axis=1).reshape(-1)
    onehot = (flat[:, None] == jnp.arange(n_exp, dtype=jnp.int32)[None, :]).astype(jnp.int32)
    rank = jnp.sum((jnp.cumsum(onehot, axis=0) - onehot) * onehot, axis=1)
    counts = jnp.sum(onehot, axis=0)
    padded = (counts + tile - 1) // tile * tile
    ends = jnp.cumsum(padded)
    slot = (ends - padded)[flat] + rank
    n_tiles = (2 * n) // tile + n_exp
    row_token = jnp.zeros((n_tiles * tile,), jnp.int32).at[slot].set(jnp.arange(2 * n, dtype=jnp.int32) // 2)
    tile_start = jnp.arange(n_tiles, dtype=jnp.int32) * tile
    tile_expert = jnp.minimum(jnp.searchsorted(ends, tile_start, side="right"), n_exp - 1).astype(jnp.int32)
    return slot.reshape(n, 2), row_token.reshape(n_tiles, 1, tile), tile_expert


def _to_rows(ref, val):
    n, ch = val.shape[0], val.shape[1] // LANES
    for c in range(ch):
        ref[pl.ds(c, n, stride=ch), :] = val[:, c * LANES:(c + 1) * LANES]


def _from_rows(ref, start, n, ch):
    return [ref[pl.ds(start * ch + c, n, stride=ch), :] for c in range(ch)]


def _gather_rows(idx_cur, idx_next, src_hbm, buf, sem, ch):
    i = pl.program_id(0)
    n_idx = buf.shape[1] // ch
    slot = i % 2

    def row_copy(idx_ref, j, s):
        src = src_hbm.at[pl.ds(pl.multiple_of(idx_ref[0, j] * ch, ch), ch), :]
        return pltpu.make_async_copy(src, buf.at[s, pl.ds(j * ch, ch), :], sem.at[s])

    def wait_rows(s):
        pltpu.make_async_copy(src_hbm.at[pl.ds(0, n_idx * ch), :], buf.at[s], sem.at[s]).wait()

    @pl.when(i == 0)
    def _():
        def body(j, c):
            row_copy(idx_cur, j, 0).start()
            return c
        lax.fori_loop(0, n_idx, body, 0)

    wait_rows(slot)
    for j in range(n_idx):
        row_copy(idx_next, j, 1 - slot).start()
    return slot, lambda: wait_rows(1 - slot)


def _moe_group_kernel(te_ref, tokc_ref, tokn_ref, x_hbm, wg_ref, wu_ref, wd_ref, o_ref, buf, sem):
    del te_ref
    ch = wg_ref.shape[0] // LANES
    tile = buf.shape[1] // ch
    slot, drain = _gather_rows(tokc_ref, tokn_ref, x_hbm, buf, sem, ch)
    x = jnp.concatenate(_from_rows(buf.at[slot], 0, tile, ch), axis=1).astype(BF16)
    he = _silu(jnp.dot(x, wg_ref[...], preferred_element_type=F32)) * jnp.dot(
        x, wu_ref[...], preferred_element_type=F32)
    _to_rows(o_ref, jnp.dot(he.astype(BF16), wd_ref[...], preferred_element_type=F32))
    pl.when(pl.program_id(0) == pl.num_programs(0) - 1)(drain)


def moe_grouped(x_rows, row_token, tile_expert, wg, wu, wd):
    n_tiles, _, tile = row_token.shape
    dm, ff = wg.shape[1], wg.shape[2]
    ch = dm // LANES
    tok = lambda f: pl.BlockSpec((None, 1, tile), f, memory_space=pltpu.SMEM)
    grid_spec = pltpu.PrefetchScalarGridSpec(
        num_scalar_prefetch=1, grid=(n_tiles,),
        in_specs=[tok(lambda i, te: (i, 0, 0)),
                  tok(lambda i, te: (jnp.minimum(i + 1, n_tiles - 1), 0, 0)),
                  pl.BlockSpec(memory_space=pl.ANY),
                  pl.BlockSpec((None, dm, ff), lambda i, te: (te[i], 0, 0)),
                  pl.BlockSpec((None, dm, ff), lambda i, te: (te[i], 0, 0)),
                  pl.BlockSpec((None, ff, dm), lambda i, te: (te[i], 0, 0))],
        out_specs=pl.BlockSpec((tile * ch, LANES), lambda i, te: (i, 0)),
        scratch_shapes=[pltpu.VMEM((2, tile * ch, LANES), F32), pltpu.SemaphoreType.DMA((2,))])
    return pl.pallas_call(
        _moe_group_kernel, grid_spec=grid_spec,
        out_shape=jax.ShapeDtypeStruct((n_tiles * tile * ch, LANES), F32),
        compiler_params=_cparams(1), name="moe_grouped")(tile_expert, row_token, row_token, x_rows, wg, wu, wd)


def _moe_combine_kernel(sc_ref, sn_ref, ys_hbm, x2_ref, rt_ref, nf_ref, o_ref, buf, sem, *, final_norm):
    tm, dm = x2_ref.shape
    ch = dm // LANES
    slot, drain = _gather_rows(sc_ref, sn_ref, ys_hbm, buf, sem, ch)
    y1 = _from_rows(buf.at[slot], 0, tm, ch)
    y2 = _from_rows(buf.at[slot], tm, tm, ch)
    rt = rt_ref[...]
    w1, w2 = rt[:, 2:3], rt[:, 3:4]
    xf = jnp.concatenate([x2_ref[:, c * LANES:(c + 1) * LANES] + (w1 * y1[c] + w2 * y2[c]) for c in range(ch)],
                         axis=1)
    if final_norm:
        xf = xf * lax.rsqrt(jnp.mean(xf * xf, axis=-1, keepdims=True) + RMS_EPS) * nf_ref[...]
    o_ref[...] = xf
    pl.when(pl.program_id(0) == pl.num_programs(0) - 1)(drain)


def moe_combine(ys_rows, slots, x2, route, nf, tm, final_norm):
    n, dm = x2.shape
    nt = n // tm
    ch = dm // LANES
    srows = slots.reshape(nt, tm, 2).transpose(0, 2, 1).reshape(nt, 1, 2 * tm)
    sspec = lambda f: pl.BlockSpec((None, 1, 2 * tm), f, memory_space=pltpu.SMEM)
    return pl.pallas_call(
        functools.partial(_moe_combine_kernel, final_norm=final_norm), grid=(nt,),
        in_specs=[sspec(lambda i: (i, 0, 0)), sspec(lambda i: (jnp.minimum(i + 1, nt - 1), 0, 0)),
                  pl.BlockSpec(memory_space=pl.ANY),
                  pl.BlockSpec((tm, dm), lambda i: (i, 0)),
                  pl.BlockSpec((tm, LANES), lambda i: (i, 0)),
                  pl.BlockSpec((1, dm), lambda i: (0, 0))],
        out_specs=pl.BlockSpec((tm, dm), lambda i: (i, 0)),
        out_shape=jax.ShapeDtypeStruct((n, dm), F32),
        scratch_shapes=[pltpu.VMEM((2, 2 * tm * ch, LANES), F32), pltpu.SemaphoreType.DMA((2,))],
        compiler_params=_cparams(1), name="moe_combine")(srows, srows, ys_rows, x2, route, nf)


def moe_final(xn, route, x2, wg, wu, wd, nf, final_norm):
    e1 = route[:, 0].astype(jnp.int32)
    e2 = route[:, 1].astype(jnp.int32)
    slots, row_token, tile_expert = moe_plan(e1, e2, wg.shape[0], MOE_TILE)
    ys = moe_grouped(xn, row_token, tile_expert, wg, wu, wd)
    return moe_combine(ys, slots, x2, route, nf, min(MOE_TILE, x2.shape[0]), final_norm)


def _pad_lanes(v, width=LANES):
    return jnp.pad(v, ((0, 0), (0, width - v.shape[1])))


def _natural_rows(arr, n_rows):
    b, d, l, w = arr.shape
    tail = arr[:, :, l - n_rows // d:, :]
    return jnp.swapaxes(tail, 1, 2).reshape(b, n_rows, w)


def kernel(x_prompt, x_sample, cache_kv_w128, cache_kv_w512, cache_kv_w2048, state_ssm, state_conv,
           norm1, w_in, conv_w, conv_b, dt_bias, a_log, d_skip, ssm_norm, w_att_out, w_ssm_out, w_o,
           norm2, w_router_coarse, b_router_coarse, w_router_fine, b_router_fine,
           w_exp_gate, w_exp_up, w_exp_down, norm_f):
    bp, sp, dm = x_prompt.shape
    bs, ts, _ = x_sample.shape
    depth = w_in.shape[0]
    n_heads = dt_bias.shape[1]
    d_inner = n_heads * SSM_HEAD_DIM
    conv_dim = conv_w.shape[2]
    n_grp = len(ATT_DILATIONS)
    off_z = n_grp * 3 * ATT_WIDTH
    off_xbc = off_z + d_inner
    off_dt = off_xbc + conv_dim
    off_gate = off_dt + n_heads
    caches = (cache_kv_w128, cache_kv_w512, cache_kv_w2048)
    for g in range(n_grp):
        assert caches[g].shape[2] == ATT_WINDOWS[g] and sp % (ATT_DILATIONS[g] * ATT_BLOCK) == 0
    assert ts <= min(ATT_DILATIONS[1:]) and ts < TPAD and sp % SSM_CHUNK == 0

    tabs_p = rope_tables(jnp.arange(sp, dtype=F32))
    tabs_s = rope_tables(jnp.tile(jnp.arange(ts, dtype=F32) + PAST_LEN, bs))
    expand = (jnp.arange(LANES)[:, None] == (jnp.arange(d_inner) // SSM_HEAD_DIM)[None, :]).astype(F32)

    xp = x_prompt.reshape(bp * sp, dm)
    xs = x_sample.reshape(bs * ts, dm)
    outs = {k: [] for k in ("kvp0", "kvp1", "kvp2", "ssm_p", "conv_p", "kvs0", "kvs1", "kvs2", "ssm_s", "conv_s")}
    n_s = bs * ts
    for layer in range(depth):
        w = w_in[layer].astype(BF16)
        w_qkv = [w[:, g * 3 * ATT_WIDTH:(g + 1) * 3 * ATT_WIDTH] for g in range(n_grp)]
        w_z, w_xbc, w_gate = w[:, off_z:off_xbc], w[:, off_xbc:off_dt], w[:, off_gate:]
        w_dt = _pad_lanes(w[:, off_dt:off_gate])
        dtb = _pad_lanes(dt_bias[layer][None])
        alog = _pad_lanes(a_log[layer][None])
        dsk_e = jnp.repeat(d_skip[layer], SSM_HEAD_DIM)[None]
        nw = ssm_norm[layer][None]
        cw, cb = conv_w[layer], conv_b[layer][None]
        w_router = _pad_lanes(jnp.concatenate([w_router_coarse[layer], w_router_fine[layer]], axis=1)).astype(BF16)
        b_router = _pad_lanes(jnp.concatenate([b_router_coarse[layer], b_router_fine[layer]])[None])
        wts = (w_att_out[layer].astype(BF16), w_ssm_out[layer].astype(BF16), w_o[layer].astype(BF16),
               norm2[layer][None], w_router, b_router)
        wg, wu, wd = (w_exp_gate[layer].astype(BF16), w_exp_up[layer].astype(BF16),
                      w_exp_down[layer].astype(BF16))

        xn = rmsnorm_bf16(xp, norm1[layer], 1024)
        att = []
        for g, d in enumerate(ATT_DILATIONS):
            qkv = qkv_proj(xn, w_qkv[g], tabs_p, bp, sp, d, 1024, BF16, f"qkv_prompt_d{d}")
            att.append(attn_prompt(qkv))
            wnd = min(ATT_WINDOWS[g], sp)
            kv = jnp.stack([_natural_rows(qkv[1], wnd), _natural_rows(qkv[2], wnd)], axis=2)
            outs[f"kvp{g}"].append(kv.astype(F32).reshape(bp, wnd, 2, ATT_HEADS, ATT_HEAD_DIM))
        z = matmul(xn, w_z, BF16, 2048, 1024, "proj_z")
        xbc = matmul(xn, w_xbc, BF16, 2048, 1024, "proj_xbc")
        gates = matmul(xn, w_gate, BF16, 2048, 1024, "proj_gates")
        dt_raw = matmul(xn, w_dt, F32, 2048, LANES, "proj_dt")
        y_ssm, h_p = ssd_prompt(xbc, z, dt_raw, cw, cb, dtb, alog, dsk_e, nw, bp, sp)
        outs["ssm_p"].append(h_p.reshape(bp, n_heads, SSM_HEAD_DIM, SSM_STATE))
        outs["conv_p"].append(xbc.reshape(bp, sp, conv_dim)[:, sp - (SSM_CONV - 1):].astype(F32))
        x2, xn2, cmb = outproj(xp, gates, y_ssm, att, ATT_DILATIONS, bp, sp, wts, 256)
        xp = moe_final(xn2, cmb, x2, wg, wu, wd, norm_f[None], layer == depth - 1)

        xn = rmsnorm_bf16(xs, norm1[layer], n_s)
        pad_t = lambda a: jnp.pad(a.reshape(bs, ts, a.shape[-1]), ((0, 0), (0, TPAD - ts), (0, 0)))
        new = jnp.stack([qkv_proj(xn, w_qkv[g], tabs_s, 1, n_s, 1, n_s, F32, f"qkv_sample_{g}")
                         .reshape(3, bs, ts, ATT_WIDTH) for g in range(n_grp)])
        new = jnp.pad(new, ((0, 0), (0, 0), (0, 0), (0, TPAD - ts), (0, 0)))
        cl = [jnp.transpose(caches[g][layer], (0, 2, 3, 4, 1)) for g in range(n_grp)]
        res = cache_attn(new, cl, ts)
        attn_s = res[0][:, :ts]
        for g in range(n_grp):
            outs[f"kvs{g}"].append(jnp.transpose(res[1 + g], (0, 4, 1, 2, 3)))
        z = matmul(xn, w_z, F32, n_s, 512, "proj_z_s")
        xbc = matmul(xn, w_xbc, F32, n_s, 512, "proj_xbc_s")
        gates = matmul(xn, w_gate, F32, n_s, 512, "proj_gates_s")
        dt_raw = matmul(xn, w_dt, F32, n_s, LANES, "proj_dt_s")
        y_s, h_s = ssd_sample(pad_t(xbc), pad_t(z), pad_t(dt_raw), state_conv[layer],
                              state_ssm[layer].reshape(bs, d_inner, SSM_STATE), cw, cb, dtb, alog, dsk_e, nw,
                              expand, ts)
        outs["ssm_s"].append(h_s.reshape(bs, n_heads, SSM_HEAD_DIM, SSM_STATE))
        hist = jnp.concatenate([state_conv[layer], xbc.reshape(bs, ts, conv_dim)], axis=1)
        outs["conv_s"].append(hist[:, ts:])
        y_s = y_s[:, :ts].reshape(n_s, d_inner)
        x2, xn2, cmb = outproj(xs, gates, y_s, attn_s.reshape(n_s, ATT_WIDTH), (), 1, n_s, wts, min(256, n_s))
        xs = moe_final(xn2, cmb, x2, wg, wu, wd, norm_f[None], layer == depth - 1)

    st = lambda k: jnp.stack(outs[k])
    return (xp.reshape(bp, sp, dm), xs.reshape(bs, ts, dm),
            st("kvp0"), st("kvp1"), st("kvp2"), st("ssm_p"), st("conv_p"),
            st("kvs0"), st("kvs1"), st("kvs2"), st("ssm_s"), st("conv_s"))
```

```python
import functools
import math

import jax
import jax.numpy as jnp
from jax import lax
from jax.experimental import pallas as pl
from jax.experimental.pallas import tpu as pltpu

F32 = jnp.float32
BF16 = jnp.bfloat16

PAST_LEN = 8192
ATT_WINDOWS = (128, 512, 2048)
ATT_DILATIONS = (1, 4, 16)
ATT_HEADS = 8
ATT_HEAD_DIM = 64
ATT_WIDTH = ATT_HEADS * ATT_HEAD_DIM
ATT_SCALE = ATT_HEAD_DIM ** -0.5
ROT_DIM = ATT_HEAD_DIM // 4
ROPE_THETA = 500000.0
ATT_BLOCK = 128

SSM_HEAD_DIM = 64
SSM_STATE = 128
SSM_GROUPS = 4
SSM_CONV = 4
SSM_CHUNK = 128
MOE_GROUPS = 4
MOE_PER_GROUP = 4
MOE_EXPERTS = MOE_GROUPS * MOE_PER_GROUP
RMS_EPS = 1e-6
SSM_NORM_EPS = 1e-5

LANES = 128
VMEM_LIMIT = 56 * 1024 * 1024
NEG_INF = float("-inf")


def _cparams(n_axes):
    return pltpu.CompilerParams(dimension_semantics=("arbitrary",) * n_axes,
                                vmem_limit_bytes=VMEM_LIMIT)


def _sigmoid(x):
    return 1.0 / (1.0 + jnp.exp(-x))


def _silu(x):
    return x * _sigmoid(x)


def _norm_kernel(x_ref, g_ref, o_ref):
    x = x_ref[...]
    y = x * lax.rsqrt(jnp.mean(x * x, axis=-1, keepdims=True) + RMS_EPS)
    o_ref[...] = (y * g_ref[...]).astype(o_ref.dtype)


def rmsnorm_bf16(x, g, tm):
    n, d = x.shape
    return pl.pallas_call(
        _norm_kernel, grid=(n // tm,),
        in_specs=[pl.BlockSpec((tm, d), lambda i: (i, 0)), pl.BlockSpec((1, d), lambda i: (0, 0))],
        out_specs=pl.BlockSpec((tm, d), lambda i: (i, 0)),
        out_shape=jax.ShapeDtypeStruct((n, d), BF16),
        compiler_params=_cparams(1), name="rmsnorm")(x, g.reshape(1, d))


def _mm_kernel(x_ref, w_ref, o_ref):
    o_ref[...] = jnp.dot(x_ref[...], w_ref[...], preferred_element_type=F32).astype(o_ref.dtype)


def matmul(x, w, out_dtype, tm, tn, name):
    m, k = x.shape
    n = w.shape[1]
    return pl.pallas_call(
        _mm_kernel, grid=(m // tm, n // tn),
        in_specs=[pl.BlockSpec((tm, k), lambda i, j: (i, 0)), pl.BlockSpec((k, tn), lambda i, j: (0, j))],
        out_specs=pl.BlockSpec((tm, tn), lambda i, j: (i, j)),
        out_shape=jax.ShapeDtypeStruct((m, n), out_dtype),
        compiler_params=_cparams(2), name=name)(x, w)


QKV_CHUNK = 256


def _qkv_kernel(x_ref, w_ref, cos_ref, sa_ref, sb_ref, o_ref, acc_ref, *, d):
    sc = jnp.where(pl.program_id(1) == 0, ATT_SCALE, 1.0).astype(F32)
    tm = x_ref.shape[0]
    ck = min(QKV_CHUNK, tm)
    for c0 in range(0, tm, ck):
        rows = slice(c0, c0 + ck)
        acc = jnp.dot(x_ref[rows, :], w_ref[...], preferred_element_type=F32)
        c = cos_ref[rows, :] * sc
        sa = sa_ref[rows, :] * sc
        sb = sb_ref[rows, :] * sc
        for ch in range(ATT_WIDTH // LANES):
            cs = slice(ch * LANES, (ch + 1) * LANES)
            t = acc[:, cs]
            r = t * c + pltpu.roll(t, LANES - ROT_DIM // 2, 1) * sa + pltpu.roll(t, ROT_DIM // 2, 1) * sb
            if d == 1:
                o_ref[0, rows, cs] = r.astype(o_ref.dtype)
            else:
                acc_ref[ch, rows, :] = r
                for res in range(d):
                    o_ref[res, c0 // d:(c0 + ck) // d, cs] = acc_ref[
                        ch, pl.ds(c0 + res, ck // d, stride=d), :].astype(o_ref.dtype)


def qkv_proj(xn, w, tabs, batch, seq, d, tm, out_dtype, name):
    n, dm = xn.shape
    tpb = seq // tm
    cos, sa, sb = tabs
    tab_spec = pl.BlockSpec((None, tm, LANES), lambda i, j: (jnp.where(j == 2, 1, 0), i % tpb, 0))
    return pl.pallas_call(
        functools.partial(_qkv_kernel, d=d), grid=(n // tm, 3),
        in_specs=[pl.BlockSpec((tm, dm), lambda i, j: (i, 0)),
                  pl.BlockSpec((dm, ATT_WIDTH), lambda i, j: (0, j)),
                  tab_spec, tab_spec, tab_spec],
        out_specs=pl.BlockSpec((None, None, d, tm // d, ATT_WIDTH),
                               lambda i, j: (j, i // tpb, 0, i % tpb, 0)),
        out_shape=jax.ShapeDtypeStruct((3, batch, d, seq // d, ATT_WIDTH), out_dtype),
        scratch_shapes=[pltpu.VMEM((ATT_WIDTH // LANES, tm, LANES), F32)],
        compiler_params=_cparams(2), name=name)(xn, w, cos, sa, sb)


def rope_tables(pos):
    half = ROT_DIM // 2
    inv = ROPE_THETA ** (-jnp.arange(half, dtype=F32) / half)
    ang = pos[:, None] * inv[None, :]
    cos = jnp.cos(ang)
    sin = jnp.sin(ang)
    s = pos.shape[0]
    ones = jnp.ones((s, ATT_HEAD_DIM - ROT_DIM), F32)
    zeros = jnp.zeros((s, ATT_HEAD_DIM - ROT_DIM), F32)
    zh = jnp.zeros((s, half), F32)
    c = jnp.concatenate([cos, cos, ones], axis=1)
    sa = jnp.concatenate([-sin, zh, zeros], axis=1)
    sb = jnp.concatenate([zh, sin, zeros], axis=1)
    rep = LANES // ATT_HEAD_DIM
    c, sa, sb = (jnp.tile(t, (1, rep)) for t in (c, sa, sb))
    return (jnp.stack([c, jnp.ones_like(c)]), jnp.stack([sa, jnp.zeros_like(sa)]),
            jnp.stack([sb, jnp.zeros_like(sb)]))


ATT_ROW_CHUNK = 32
ATT_SUB = 2


def _attn_kernel(q_ref, kc_ref, vc_ref, kp_ref, vp_ref, num_ref, st_ref, s_scr, p_scr):
    nb = pl.program_id(2)
    blk = ATT_BLOCK
    nt = (((1,), (1,)), ((), ()))
    heads = [slice(h * ATT_HEAD_DIM, (h + 1) * ATT_HEAD_DIM) for h in range(ATT_HEADS)]
    subs = [slice(sb * blk, (sb + 1) * blk) for sb in range(ATT_SUB)]

    def prev(ref_p, ref_c, sb, sl):
        return ref_p[:, sl] if sb == 0 else ref_c[subs[sb - 1], sl]

    for sb, rows in enumerate(subs):
        for h, sl in enumerate(heads):
            q = q_ref[rows, sl]
            s_scr[sb, h, :, 0:blk] = lax.dot_general(q, prev(kp_ref, kc_ref, sb, sl), nt,
                                                     preferred_element_type=F32)
            s_scr[sb, h, :, blk:2 * blk] = lax.dot_general(q, kc_ref[rows, sl], nt, preferred_element_type=F32)
    rc = ATT_ROW_CHUNK
    qi = lax.broadcasted_iota(jnp.int32, (rc, 2 * blk), 0)
    kj = lax.broadcasted_iota(jnp.int32, (rc, 2 * blk), 1)
    lane = lax.broadcasted_iota(jnp.int32, (rc, LANES), 1)
    for sb, rows in enumerate(subs):
        for r0 in range(0, blk, rc):
            qa = qi + r0
            in_prev = jnp.logical_and(kj < blk, kj >= qa)
            if sb == 0:
                in_prev = jnp.logical_and(in_prev, nb > 0)
            mask = jnp.logical_or(in_prev, jnp.logical_and(kj >= blk, kj - blk <= qa))
            st = jnp.zeros((rc, LANES), F32)
            for h in range(ATT_HEADS):
                s = jnp.where(mask, s_scr[sb, h, r0:r0 + rc, :], NEG_INF)
                m = jnp.max(s, axis=-1, keepdims=True)
                p = jnp.exp(s - m)
                l = jnp.sum(p, axis=-1, keepdims=True)
                p_scr[sb, h, r0:r0 + rc, :] = p.astype(BF16)
                st = jnp.where(lane == h, m, st)
                st = jnp.where(lane == ATT_HEADS + h, l, st)
            st_ref[sb * blk + r0:sb * blk + r0 + rc, :] = st
    for sb, rows in enumerate(subs):
        for h, sl in enumerate(heads):
            num_ref[rows, sl] = (
                jnp.dot(p_scr[sb, h, :, 0:blk], prev(vp_ref, vc_ref, sb, sl), preferred_element_type=F32)
                + jnp.dot(p_scr[sb, h, :, blk:2 * blk], vc_ref[rows, sl], preferred_element_type=F32))


def attn_prompt(qkv):
    _, b, d, l, w = qkv.shape
    step = ATT_SUB * ATT_BLOCK
    nstep = l // step

    def spec(kind, prev):
        if prev:
            return pl.BlockSpec((None, None, None, ATT_BLOCK, w),
                                lambda bi, r, n: (kind, bi, r, jnp.maximum(ATT_SUB * n - 1, 0), 0))
        return pl.BlockSpec((None, None, None, step, w), lambda bi, r, n: (kind, bi, r, n, 0))

    return pl.pallas_call(
        _attn_kernel, grid=(b, d, nstep),
        in_specs=[spec(0, False), spec(1, False), spec(2, False), spec(1, True), spec(2, True)],
        out_specs=[pl.BlockSpec((None, None, step, w), lambda bi, r, n: (bi, r, n, 0)),
                   pl.BlockSpec((None, None, step, LANES), lambda bi, r, n: (bi, r, n, 0))],
        out_shape=[jax.ShapeDtypeStruct((b, d, l, w), F32), jax.ShapeDtypeStruct((b, d, l, LANES), F32)],
        scratch_shapes=[pltpu.VMEM((ATT_SUB, ATT_HEADS, ATT_BLOCK, 2 * ATT_BLOCK), F32),
                        pltpu.VMEM((ATT_SUB, ATT_HEADS, ATT_BLOCK, 2 * ATT_BLOCK), BF16)],
        compiler_params=_cparams(3), name=f"attn_prompt_d{d}")(qkv, qkv, qkv, qkv, qkv)


def _col(mat, h, n_lanes=LANES):
    return jnp.broadcast_to(mat[:, h:h + 1], (mat.shape[0], n_lanes))


def _ssd_kernel(xbc_ref, z_ref, dt_ref, cw_ref, cb_ref, dtb_ref, alog_ref, dsk_ref, nw_ref,
                y_ref, hout_ref,
                cbuf, xs_s, xw_s, eac_s, cd_s, y_s, ht_s, *, d_inner, n_groups):
    c = pl.program_id(1)
    nc = pl.num_programs(1)
    q = SSM_CHUNK
    n_st = SSM_STATE
    gw = d_inner // n_groups
    conv_dim = xbc_ref.shape[1]
    top = 8

    @pl.when(c == 0)
    def _():
        cbuf[0:top, :] = jnp.zeros((top, conv_dim), F32)
        ht_s[...] = jnp.zeros(ht_s.shape, F32)

    cbuf[top:top + q, :] = xbc_ref[...].astype(F32)
    cwid = 512
    for j in range(conv_dim // cwid):
        cs = slice(j * cwid, (j + 1) * cwid)
        acc = cb_ref[:, cs] + cbuf[top - 3:top - 3 + q, cs] * cw_ref[0:1, cs]
        for i in range(1, SSM_CONV):
            acc = acc + cbuf[top - 3 + i:top - 3 + i + q, cs] * cw_ref[i:i + 1, cs]
        xs_s[:, cs] = _silu(acc)
    cbuf[top - 3:top, :] = cbuf[top + q - 3:top + q, :]

    dt = jax.nn.softplus(dt_ref[...] + dtb_ref[...])
    a_row = -jnp.exp(alog_ref[...])
    a = dt * a_row
    ri = lax.broadcasted_iota(jnp.int32, (q, q), 0)
    ci = lax.broadcasted_iota(jnp.int32, (q, q), 1)
    causal = ri >= ci
    tril = jnp.where(causal, 1.0, 0.0).astype(F32)
    acum = jnp.dot(tril, a, preferred_element_type=F32, precision=lax.Precision.HIGHEST)
    acum_t = acum.T
    lane = lax.broadcasted_iota(jnp.int32, (q, LANES), 1)
    lo = lane < SSM_HEAD_DIM

    hpg = gw // SSM_HEAD_DIM
    for g in range(n_groups):
        bc = xs_s[:, d_inner + g * n_st:d_inner + (g + 1) * n_st].astype(BF16)
        cc = xs_s[:, d_inner + n_groups * n_st + g * n_st:d_inner + n_groups * n_st + (g + 1) * n_st].astype(BF16)
        cb = lax.dot_general(cc, bc, (((1,), (1,)), ((), ())), preferred_element_type=F32)
        for jp in range(hpg // 2):
            h0 = g * hpg + 2 * jp
            ls = slice(g * gw + jp * LANES, g * gw + (jp + 1) * LANES)
            ac0 = _col(acum, h0)
            ac1 = _col(acum, h0 + 1)
            acum_e = jnp.where(lo, ac0, ac1)
            dt_e = jnp.where(lo, _col(dt, h0), _col(dt, h0 + 1))
            xdt = xs_s[:, ls] * dt_e
            acl_e = acum_e[q - 1:q, :]
            xw_s[:, ls] = (xdt * jnp.exp(acl_e - acum_e)).astype(BF16)
            eac_s[:, ls] = jnp.exp(acum_e)
            cd_s[:, ls] = jnp.exp(acl_e)
            xdt_b = xdt.astype(BF16)
            zero = jnp.zeros_like(xdt_b)
            m0 = (jnp.exp(jnp.where(causal, ac0 - acum_t[h0:h0 + 1, :], NEG_INF)) * cb).astype(BF16)
            m1 = (jnp.exp(jnp.where(causal, ac1 - acum_t[h0 + 1:h0 + 2, :], NEG_INF)) * cb).astype(BF16)
            y_s[:, ls] = (jnp.dot(m0, jnp.where(lo, xdt_b, zero), preferred_element_type=F32)
                          + jnp.dot(m1, jnp.where(lo, zero, xdt_b), preferred_element_type=F32))
        gs = slice(g * gw, (g + 1) * gw)
        h_prev = ht_s[g]
        y_off = jnp.dot(cc, h_prev.astype(BF16), preferred_element_type=F32) * eac_s[:, gs]
        y_s[:, gs] = y_s[:, gs] + y_off
        st = lax.dot_general(bc, xw_s[:, gs], (((0,), (0,)), ((), ())), preferred_element_type=F32)
        ht_s[g] = cd_s[:, gs] * h_prev + st

    for g in range(n_groups):
        gs = slice(g * gw, (g + 1) * gw)
        y = y_s[:, gs] + dsk_ref[:, gs] * xs_s[:, gs]
        yf = y * _silu(z_ref[:, gs].astype(F32))
        yf = yf * lax.rsqrt(jnp.mean(yf * yf, axis=-1, keepdims=True) + SSM_NORM_EPS)
        y_ref[:, gs] = (yf * nw_ref[:, gs]).astype(y_ref.dtype)

    @pl.when(c == nc - 1)
    def _():
        for g in range(n_groups):
            hout_ref[g * gw:(g + 1) * gw, :] = ht_s[g].T


def ssd_prompt(xbc, z, dt_raw, conv_w, conv_b, dtb, alog, dsk_e, nw, batch, seq):
    n, conv_dim = xbc.shape
    d_inner = z.shape[1]
    n_groups = SSM_GROUPS
    gw = d_inner // n_groups
    q = SSM_CHUNK
    cps = seq // q
    row = lambda w: pl.BlockSpec((1, w), lambda b, c: (0, 0))
    tok = lambda w: pl.BlockSpec((q, w), lambda b, c: (b * cps + c, 0))
    return pl.pallas_call(
        functools.partial(_ssd_kernel, d_inner=d_inner, n_groups=n_groups),
        grid=(batch, cps),
        in_specs=[tok(conv_dim), tok(d_inner), tok(LANES),
                  pl.BlockSpec((SSM_CONV, conv_dim), lambda b, c: (0, 0)), row(conv_dim),
                  row(LANES), row(LANES), row(d_inner), row(d_inner)],
        out_specs=[tok(d_inner), pl.BlockSpec((None, d_inner, SSM_STATE), lambda b, c: (b, 0, 0))],
        out_shape=[jax.ShapeDtypeStruct((n, d_inner), BF16),
                   jax.ShapeDtypeStruct((batch, d_inner, SSM_STATE), F32)],
        scratch_shapes=[pltpu.VMEM((8 + q, conv_dim), F32),
                        pltpu.VMEM((q, conv_dim), F32),
                        pltpu.VMEM((q, d_inner), BF16),
                        pltpu.VMEM((q, d_inner), F32),
                        pltpu.VMEM((1, d_inner), F32),
                        pltpu.VMEM((q, d_inner), F32),
                        pltpu.VMEM((n_groups, SSM_STATE, gw), F32)],
        compiler_params=_cparams(2), name="ssd_prompt")(xbc, z, dt_raw, conv_w, conv_b, dtb, alog, dsk_e, nw)


TPAD = 8


def _ssd_step_kernel(xbc_ref, z_ref, dt_ref, cst_ref, h0_ref, cw_ref, cb_ref, dtb_ref, alog_ref,
                     dsk_ref, nw_ref, exp_ref, y_ref, hout_ref,
                     cbuf, xs_s, f_s, *, d_inner, n_groups, n_tok):
    b = pl.program_id(0)
    n_st = SSM_STATE
    gw = d_inner // n_groups
    conv_dim = xbc_ref.shape[1]
    top = 8
    hist = SSM_CONV - 1

    @pl.when(b == 0)
    def _():
        cbuf[0:top, :] = jnp.zeros((top, conv_dim), F32)

    cbuf[top - hist:top, :] = cst_ref[...]
    cbuf[top:top + TPAD, :] = xbc_ref[...]
    acc = cb_ref[...] + cbuf[top - hist:top - hist + TPAD, :] * cw_ref[0:1, :]
    for i in range(1, SSM_CONV):
        acc = acc + cbuf[top - hist + i:top - hist + i + TPAD, :] * cw_ref[i:i + 1, :]
    xs_s[...] = _silu(acc)

    rid = lax.broadcasted_iota(jnp.int32, (TPAD, LANES), 0)
    dt = jax.nn.softplus(dt_ref[...] + dtb_ref[...])
    da = dt * (-jnp.exp(alog_ref[...]))
    cum = da
    for k in range(1, n_tok):
        cum = cum + jnp.where(rid >= k, pltpu.roll(da, k, 0), 0.0)
    facs = [dt, jnp.exp(cum)]
    for t in range(n_tok):
        facs.append(jnp.where(rid <= t, jnp.exp(cum[t:t + 1, :] - cum), 0.0))
    hi = lax.Precision.HIGHEST
    f_s[...] = jnp.dot(jnp.concatenate(facs, axis=0), exp_ref[...], preferred_element_type=F32, precision=hi)

    nt = (((1,), (1,)), ((), ()))
    row8 = lax.broadcasted_iota(jnp.int32, (TPAD, gw), 0)
    for g in range(n_groups):
        gs = slice(g * gw, (g + 1) * gw)
        xdt = xs_s[:, gs] * f_s[0:TPAD, gs]
        b_f = xs_s[:, d_inner + g * n_st:d_inner + (g + 1) * n_st]
        c_f = xs_s[:, d_inner + (n_groups + g) * n_st:d_inner + (n_groups + g + 1) * n_st]
        b16 = b_f.astype(BF16)
        c16 = c_f.astype(BF16)
        h0g = h0_ref[gs, :]
        y = f_s[TPAD:2 * TPAD, gs] * lax.dot_general(c16, h0g.astype(BF16), nt, preferred_element_type=F32)
        bc = lax.dot_general(b16, c16, nt, preferred_element_type=F32)
        for t in range(n_tok):
            term = bc[:, t:t + 1] * f_s[(2 + t) * TPAD:(3 + t) * TPAD, gs] * xdt
            y = y + jnp.where(row8 == t, jnp.sum(term, axis=0, keepdims=True), 0.0)
        yv = y + dsk_ref[:, gs] * xs_s[:, gs]
        yf = yv * _silu(z_ref[:, gs])
        yf = yf * lax.rsqrt(jnp.mean(yf * yf, axis=-1, keepdims=True) + SSM_NORM_EPS)
        y_ref[:, gs] = yf * nw_ref[:, gs]

        dend = f_s[(1 + n_tok) * TPAD:(2 + n_tok) * TPAD, gs]
        pend = f_s[TPAD + n_tok - 1:TPAD + n_tok, gs]
        lhs = jnp.where(row8 == n_tok, pend, dend * xdt)
        rhs = jnp.concatenate([jnp.where(rid < n_tok, b_f, 0.0), jnp.where(rid == n_tok, 1.0, 0.0)], axis=1)
        res = lax.dot_general(lhs, rhs, (((0,), (0,)), ((), ())), preferred_element_type=F32, precision=hi)
        hout_ref[gs, :] = res[:, n_st:] * h0g + res[:, :n_st]


def ssd_sample(xbc, z, dt_raw, conv_state, h0, conv_w, conv_b, dtb, alog, dsk_e, nw, expand, n_tok):
    bsz, _, conv_dim = xbc.shape
    d_inner = z.shape[2]
    row = lambda w: pl.BlockSpec((1, w), lambda b: (0, 0))
    tok = lambda w: pl.BlockSpec((None, TPAD, w), lambda b: (b, 0, 0))
    st = pl.BlockSpec((None, d_inner, SSM_STATE), lambda b: (b, 0, 0))
    return pl.pallas_call(
        functools.partial(_ssd_step_kernel, d_inner=d_inner, n_groups=SSM_GROUPS, n_tok=n_tok),
        grid=(bsz,),
        in_specs=[tok(conv_dim), tok(d_inner), tok(LANES),
                  pl.BlockSpec((None, SSM_CONV - 1, conv_dim), lambda b: (b, 0, 0)), st,
                  pl.BlockSpec((SSM_CONV, conv_dim), lambda b: (0, 0)), row(conv_dim),
                  row(LANES), row(LANES), row(d_inner), row(d_inner),
                  pl.BlockSpec((LANES, d_inner), lambda b: (0, 0))],
        out_specs=[tok(d_inner), st],
        out_shape=[jax.ShapeDtypeStruct((bsz, TPAD, d_inner), F32),
                   jax.ShapeDtypeStruct((bsz, d_inner, SSM_STATE), F32)],
        scratch_shapes=[pltpu.VMEM((8 + TPAD, conv_dim), F32),
                        pltpu.VMEM((TPAD, conv_dim), F32),
                        pltpu.VMEM(((2 + n_tok) * TPAD, d_inner), F32)],
        compiler_params=_cparams(1), name="ssd_sample")(
            xbc, z, dt_raw, conv_state, h0, conv_w, conv_b, dtb, alog, dsk_e, nw, expand)


HEADS_PER_STEP = 4


def _cache_attn_kernel(new_ref, c0_ref, c1_ref, c2_ref, o_ref, n0_ref, n1_ref, n2_ref, stage, tts, *, n_tok):
    first = jnp.logical_and(pl.program_id(0) == 0, pl.program_id(1) == 0)

    @pl.when(first)
    def _():
        stage[...] = jnp.zeros(stage.shape, F32)

    crefs = (c0_ref, c1_ref, c2_ref)
    orefs = (n0_ref, n1_ref, n2_ref)
    hd = ATT_HEAD_DIM
    nt = (((1,), (1,)), ((), ()))
    lane = lax.broadcasted_iota(jnp.int32, (hd, LANES), 1)
    t_new = lax.broadcasted_iota(jnp.int32, (TPAD, TPAD), 0)
    u_new = lax.broadcasted_iota(jnp.int32, (TPAD, TPAD), 1)

    for head in range(HEADS_PER_STEP):
        hs = slice(head * hd, (head + 1) * hd)
        parts = []
        for g, d in enumerate(ATT_DILATIONS):
            w = crefs[g].shape[-1]
            q = new_ref[g, 0, :, hs].astype(BF16)
            kn = new_ref[g, 1, :, hs].astype(BF16)
            vn = new_ref[g, 2, :, hs].astype(BF16)
            t_id = lax.broadcasted_iota(jnp.int32, (TPAD, w), 0)
            r_id = lax.broadcasted_iota(jnp.int32, (TPAD, w), 1)
            s = jnp.dot(q, crefs[g][0, head].astype(BF16), preferred_element_type=F32)
            s = jnp.where((r_id >= t_id) if d == 1 else ((r_id & (d - 1)) == t_id), s, NEG_INF)
            sn = lax.dot_general(q, kn, nt, preferred_element_type=F32)
            sn = jnp.where((u_new <= t_new) if d == 1 else (u_new == t_new), sn, NEG_INF)
            m = jnp.maximum(jnp.max(s, axis=1, keepdims=True), jnp.max(sn, axis=1, keepdims=True))
            p = jnp.exp(s - m)
            pn = jnp.exp(sn - m)
            l = jnp.sum(p, axis=1, keepdims=True) + jnp.sum(pn, axis=1, keepdims=True)
            num = (lax.dot_general(p.astype(BF16), crefs[g][1, head].astype(BF16), nt,
                                   preferred_element_type=F32)
                   + jnp.dot(pn.astype(BF16), vn, preferred_element_type=F32))
            parts.append((num, m, l))
        mx = jnp.maximum(jnp.maximum(parts[0][1], parts[1][1]), parts[2][1])
        wg = [jnp.exp(p_[1] - mx) for p_ in parts]
        num = wg[0] * parts[0][0]
        for g in range(1, 3):
            num = num + wg[g] * parts[g][0]
        den = wg[0] * parts[0][2] + wg[1] * parts[1][2] + wg[2] * parts[2][2]
        o_ref[:, hs] = num / den

    for pair in range(HEADS_PER_STEP // 2):
        ps = slice(pair * LANES, (pair + 1) * LANES)
        for g in range(3):
            for kv in range(2):
                stage[0:TPAD, :] = new_ref[g, 1 + kv, :, ps]
                tts[2 * g + kv] = stage[...].T
        for hh in range(2):
            head = pair * 2 + hh
            hs = slice(hh * hd, (hh + 1) * hd)
            for g in range(3):
                n_col = crefs[g].shape[-1] // LANES
                for kv in range(2):
                    cur = pltpu.roll(crefs[g][kv, head, :, 0:LANES], LANES - n_tok, 1)
                    for j in range(n_col):
                        if j + 1 < n_col:
                            nxt = pltpu.roll(crefs[g][kv, head, :, (j + 1) * LANES:(j + 2) * LANES],
                                             LANES - n_tok, 1)
                        else:
                            nxt = pltpu.roll(tts[2 * g + kv, hs, :], LANES - n_tok, 1)
                        orefs[g][kv, head, :, j * LANES:(j + 1) * LANES] = jnp.where(
                            lane < LANES - n_tok, cur, nxt)
                        cur = nxt


def cache_attn(new, caches, n_tok):
    bsz = new.shape[2]
    hps = HEADS_PER_STEP
    cspec = lambda c: pl.BlockSpec((None, 2, hps, ATT_HEAD_DIM, c.shape[-1]), lambda b, h: (b, 0, h, 0, 0))
    return pl.pallas_call(
        functools.partial(_cache_attn_kernel, n_tok=n_tok), grid=(bsz, ATT_HEADS // hps),
        in_specs=[pl.BlockSpec((3, 3, None, TPAD, hps * ATT_HEAD_DIM), lambda b, h: (0, 0, b, 0, h))]
        + [cspec(c) for c in caches],
        out_specs=[pl.BlockSpec((None, TPAD, hps * ATT_HEAD_DIM), lambda b, h: (b, 0, h))]
        + [cspec(c) for c in caches],
        out_shape=[jax.ShapeDtypeStruct((bsz, TPAD, ATT_WIDTH), F32)]
        + [jax.ShapeDtypeStruct(c.shape, c.dtype) for c in caches],
        scratch_shapes=[pltpu.VMEM((LANES, LANES), F32), pltpu.VMEM((6, LANES, LANES), F32)],
        compiler_params=_cparams(2), name="cache_attn")(new, *caches)


def _router(logits):
    lanef = lax.broadcasted_iota(jnp.int32, logits.shape, 1).astype(F32)
    big = 1e9
    lc = jnp.where(lanef < MOE_GROUPS, logits, NEG_INF)
    mc = jnp.max(lc, axis=-1, keepdims=True)
    g_sel = jnp.min(jnp.where(lc == mc, lanef, big), axis=-1, keepdims=True)
    p_sel = 1.0 / jnp.sum(jnp.exp(lc - mc), axis=-1, keepdims=True)
    base = MOE_GROUPS + MOE_PER_GROUP * g_sel
    lf = jnp.where(jnp.logical_and(lanef >= base, lanef < base + MOE_PER_GROUP), logits, NEG_INF)
    v1 = jnp.max(lf, axis=-1, keepdims=True)
    i1 = jnp.min(jnp.where(lf == v1, lanef, big), axis=-1, keepdims=True)
    lf2 = jnp.where(lanef == i1, NEG_INF, lf)
    v2 = jnp.max(lf2, axis=-1, keepdims=True)
    i2 = jnp.min(jnp.where(lf2 == v2, lanef, big), axis=-1, keepdims=True)
    e2 = jnp.exp(v2 - v1)
    den = 1.0 + e2
    w1 = (1.0 / den) * p_sel
    w2 = (e2 / den) * p_sel
    out = jnp.where(lanef == 0.0, i1 - MOE_GROUPS, 0.0)
    out = jnp.where(lanef == 1.0, i2 - MOE_GROUPS, out)
    out = jnp.where(lanef == 2.0, w1, out)
    return jnp.where(lanef == 3.0, w2, out)


def _outproj_kernel(*refs, dils, tm):
    n_g = len(dils)
    if n_g:
        x_ref, g_ref, ys_ref = refs[:3]
        att_refs = refs[3:3 + 2 * n_g]
        rest = refs[3 + 2 * n_g:]
    else:
        x_ref, g_ref, ys_ref, attn_ref = refs[:4]
        rest = refs[4:]
    wa_ref, ws_ref, wo_ref, n2_ref, wr_ref, br_ref, x2_ref, xn_ref, cmb_ref = rest[:9]
    scr = rest[9:]

    if n_g:
        n_ch = ATT_WIDTH // LANES
        nums, stats = [], []
        for g, d in enumerate(dils):
            num_ref, st_ref = att_refs[2 * g], att_refs[2 * g + 1]
            if d == 1:
                nums.append([num_ref[0, :, ch * LANES:(ch + 1) * LANES] for ch in range(n_ch)])
                stats.append(st_ref[0])
            else:
                ns, ss = scr[2 * g], scr[2 * g + 1]
                for r in range(d):
                    for ch in range(n_ch):
                        ns[ch, pl.ds(r, tm // d, stride=d), :] = num_ref[r, :, ch * LANES:(ch + 1) * LANES]
                    ss[pl.ds(r, tm // d, stride=d), :] = st_ref[r]
                nums.append([ns[ch] for ch in range(n_ch)])
                stats.append(ss[...])
        lane = lax.broadcasted_iota(jnp.int32, (tm, LANES), 1)
        lo = lane < ATT_HEAD_DIM
        chunks = []
        for ch in range(ATT_WIDTH // LANES):
            wts, dens = [], []
            for h in (2 * ch, 2 * ch + 1):
                ms = [s[:, h:h + 1] for s in stats]
                ls = [s[:, ATT_HEADS + h:ATT_HEADS + h + 1] for s in stats]
                mx = ms[0]
                for m in ms[1:]:
                    mx = jnp.maximum(mx, m)
                w = [jnp.exp(m - mx) for m in ms]
                den = w[0] * ls[0]
                for g in range(1, n_g):
                    den = den + w[g] * ls[g]
                wts.append(w)
                dens.append(den)
            num = jnp.where(lo, wts[0][0], wts[1][0]) * nums[0][ch]
            for g in range(1, n_g):
                num = num + jnp.where(lo, wts[0][g], wts[1][g]) * nums[g][ch]
            chunks.append((num / jnp.where(lo, dens[0], dens[1])).astype(BF16))
        attn = jnp.concatenate(chunks, axis=1)
    else:
        attn = attn_ref[...].astype(BF16)

    dm = x_ref.shape[1]
    a = jnp.dot(attn, wa_ref[...], preferred_element_type=F32)
    s = jnp.dot(ys_ref[...].astype(BF16), ws_ref[...], preferred_element_type=F32)
    gates = g_ref[...].astype(F32)
    mixed = _sigmoid(gates[:, :dm]) * a + _sigmoid(gates[:, dm:]) * s
    x2 = x_ref[...] + jnp.dot(mixed.astype(BF16), wo_ref[...], preferred_element_type=F32)
    x2_ref[...] = x2
    xn = x2 * lax.rsqrt(jnp.mean(x2 * x2, axis=-1, keepdims=True) + RMS_EPS) * n2_ref[...]
    _to_rows(xn_ref, xn)
    logits = jnp.dot(xn.astype(BF16), wr_ref[...], preferred_element_type=F32) + br_ref[...]
    cmb_ref[...] = _router(logits)


def outproj(x, gates, yssm, att, dils, batch, seq, weights, tm):
    n, dm = x.shape
    wa, ws, wo, n2, wr, br = weights
    tpb = seq // tm
    tok = lambda w: pl.BlockSpec((tm, w), lambda i: (i, 0))
    full = lambda a: pl.BlockSpec(a.shape, lambda i: (0,) * a.ndim)
    in_specs = [tok(dm), tok(gates.shape[1]), tok(yssm.shape[1])]
    args = [x, gates, yssm]
    scratch = []
    if dils:
        for (num, st), d in zip(att, dils):
            for arr in (num, st):
                in_specs.append(pl.BlockSpec((None, d, tm // d, arr.shape[-1]),
                                             lambda i: (i // tpb, 0, i % tpb, 0)))
                args.append(arr)
                scratch.append(pltpu.VMEM((tm, LANES) if arr.shape[-1] == LANES
                                          else (arr.shape[-1] // LANES, tm, LANES), F32))
    else:
        in_specs.append(tok(att.shape[1]))
        args.append(att)
    in_specs += [full(wa), full(ws), full(wo), full(n2), full(wr), full(br)]
    args += [wa, ws, wo, n2, wr, br]
    return pl.pallas_call(
        functools.partial(_outproj_kernel, dils=tuple(dils), tm=tm), grid=(n // tm,),
        in_specs=in_specs,
        out_specs=[tok(dm), pl.BlockSpec((tm * (dm // LANES), LANES), lambda i: (i, 0)), tok(LANES)],
        out_shape=[jax.ShapeDtypeStruct((n, dm), F32), jax.ShapeDtypeStruct((n * (dm // LANES), LANES), F32),
                   jax.ShapeDtypeStruct((n, LANES), F32)],
        scratch_shapes=scratch,
        compiler_params=_cparams(1), name="outproj")(*args)


MOE_TILE = 256


def moe_plan(e1, e2, n_exp, tile):
    n = e1.shape[0]
    flat = jnp.stack([e1, e2], axis=1).reshape(-1)
    onehot = (flat[:, None] == jnp.arange(n_exp, dtype=jnp.int32)[None, :]).astype(jnp.int32)
    rank = jnp.sum((jnp.cumsum(onehot, axis=0) - onehot) * onehot, axis=1)
    counts = jnp.sum(onehot, axis=0)
    padded = (counts + tile - 1) // tile * tile
    ends = jnp.cumsum(padded)
    slot = (ends - padded)[flat] + rank
    n_tiles = (2 * n) // tile + n_exp
    row_token = (jnp.arange(n_tiles * tile, dtype=jnp.int32) % n).at[slot].set(
        jnp.arange(2 * n, dtype=jnp.int32) // 2)
    tile_start = jnp.arange(n_tiles, dtype=jnp.int32) * tile
    tile_expert = jnp.minimum(jnp.searchsorted(ends, tile_start, side="right"), n_exp - 1).astype(jnp.int32)
    tile_live = (tile_start < ends[n_exp - 1]).astype(jnp.int32)
    return slot.reshape(n, 2), row_token.reshape(n_tiles, 1, tile), tile_expert, tile_live


def _to_rows(ref, val):
    n, ch = val.shape[0], val.shape[1] // LANES
    for c in range(ch):
        ref[pl.ds(c, n, stride=ch), :] = val[:, c * LANES:(c + 1) * LANES]


def _from_rows(ref, start, n, ch):
    return [ref[pl.ds(start * ch + c, n, stride=ch), :] for c in range(ch)]


def _gather_rows(idx_cur, idx_next, src_hbm, buf, sem, ch):
    i = pl.program_id(0)
    n_idx = buf.shape[1] // ch
    slot = i % 2

    def row_copy(idx_ref, j, s):
        src = src_hbm.at[pl.ds(pl.multiple_of(idx_ref[0, j] * ch, ch), ch), :]
        return pltpu.make_async_copy(src, buf.at[s, pl.ds(j * ch, ch), :], sem.at[s])

    def wait_rows(s):
        pltpu.make_async_copy(src_hbm.at[pl.ds(0, n_idx * ch), :], buf.at[s], sem.at[s]).wait()

    @pl.when(i == 0)
    def _():
        def body(j, c):
            row_copy(idx_cur, j, 0).start()
            return c
        lax.fori_loop(0, n_idx, body, 0)

    wait_rows(slot)
    for j in range(n_idx):
        row_copy(idx_next, j, 1 - slot).start()
    return slot, lambda: wait_rows(1 - slot)


def _moe_group_kernel(te_ref, live_ref, tokc_ref, tokn_ref, x_hbm, wg_ref, wu_ref, wd_ref, o_ref, buf, sem):
    del te_ref
    i = pl.program_id(0)
    ch = wg_ref.shape[0] // LANES
    tile = buf.shape[1] // ch
    slot, drain = _gather_rows(tokc_ref, tokn_ref, x_hbm, buf, sem, ch)

    @pl.when(live_ref[i] > 0)
    def _():
        x = jnp.concatenate(_from_rows(buf.at[slot], 0, tile, ch), axis=1).astype(BF16)
        he = _silu(jnp.dot(x, wg_ref[...], preferred_element_type=F32)) * jnp.dot(
            x, wu_ref[...], preferred_element_type=F32)
        _to_rows(o_ref, jnp.dot(he.astype(BF16), wd_ref[...], preferred_element_type=F32))

    @pl.when(live_ref[i] == 0)
    def _():
        o_ref[...] = jnp.zeros(o_ref.shape, F32)

    pl.when(i == pl.num_programs(0) - 1)(drain)


def moe_grouped(x_rows, row_token, tile_expert, tile_live, wg, wu, wd):
    n_tiles, _, tile = row_token.shape
    dm, ff = wg.shape[1], wg.shape[2]
    ch = dm // LANES
    tok = lambda f: pl.BlockSpec((None, 1, tile), f, memory_space=pltpu.SMEM)
    grid_spec = pltpu.PrefetchScalarGridSpec(
        num_scalar_prefetch=2, grid=(n_tiles,),
        in_specs=[tok(lambda i, te, tl: (i, 0, 0)),
                  tok(lambda i, te, tl: (jnp.minimum(i + 1, n_tiles - 1), 0, 0)),
                  pl.BlockSpec(memory_space=pl.ANY),
                  pl.BlockSpec((None, dm, ff), lambda i, te, tl: (te[i], 0, 0)),
                  pl.BlockSpec((None, dm, ff), lambda i, te, tl: (te[i], 0, 0)),
                  pl.BlockSpec((None, ff, dm), lambda i, te, tl: (te[i], 0, 0))],
        out_specs=pl.BlockSpec((tile * ch, LANES), lambda i, te, tl: (i, 0)),
        scratch_shapes=[pltpu.VMEM((2, tile * ch, LANES), F32), pltpu.SemaphoreType.DMA((2,))])
    return pl.pallas_call(
        _moe_group_kernel, grid_spec=grid_spec,
        out_shape=jax.ShapeDtypeStruct((n_tiles * tile * ch, LANES), F32),
        compiler_params=_cparams(1), name="moe_grouped")(
            tile_expert, tile_live, row_token, row_token, x_rows, wg, wu, wd)


def _moe_combine_kernel(sc_ref, sn_ref, ys_hbm, x2_ref, rt_ref, nf_ref, o_ref, buf, sem, *, final_norm):
    tm, dm = x2_ref.shape
    ch = dm // LANES
    slot, drain = _gather_rows(sc_ref, sn_ref, ys_hbm, buf, sem, ch)
    y1 = _from_rows(buf.at[slot], 0, tm, ch)
    y2 = _from_rows(buf.at[slot], tm, tm, ch)
    rt = rt_ref[...]
    w1, w2 = rt[:, 2:3], rt[:, 3:4]
    xf = jnp.concatenate([x2_ref[:, c * LANES:(c + 1) * LANES] + (w1 * y1[c] + w2 * y2[c]) for c in range(ch)],
                         axis=1)
    if final_norm:
        xf = xf * lax.rsqrt(jnp.mean(xf * xf, axis=-1, keepdims=True) + RMS_EPS) * nf_ref[...]
    o_ref[...] = xf
    pl.when(pl.program_id(0) == pl.num_programs(0) - 1)(drain)


def moe_combine(ys_rows, slots, x2, route, nf, tm, final_norm):
    n, dm = x2.shape
    nt = n // tm
    ch = dm // LANES
    srows = slots.reshape(nt, tm, 2).transpose(0, 2, 1).reshape(nt, 1, 2 * tm)
    sspec = lambda f: pl.BlockSpec((None, 1, 2 * tm), f, memory_space=pltpu.SMEM)
    return pl.pallas_call(
        functools.partial(_moe_combine_kernel, final_norm=final_norm), grid=(nt,),
        in_specs=[sspec(lambda i: (i, 0, 0)), sspec(lambda i: (jnp.minimum(i + 1, nt - 1), 0, 0)),
                  pl.BlockSpec(memory_space=pl.ANY),
                  pl.BlockSpec((tm, dm), lambda i: (i, 0)),
                  pl.BlockSpec((tm, LANES), lambda i: (i, 0)),
                  pl.BlockSpec((1, dm), lambda i: (0, 0))],
        out_specs=pl.BlockSpec((tm, dm), lambda i: (i, 0)),
        out_shape=jax.ShapeDtypeStruct((n, dm), F32),
        scratch_shapes=[pltpu.VMEM((2, 2 * tm * ch, LANES), F32), pltpu.SemaphoreType.DMA((2,))],
        compiler_params=_cparams(1), name="moe_combine")(srows, srows, ys_rows, x2, route, nf)


def moe_final(xn, route, x2, wg, wu, wd, nf, final_norm):
    e1 = route[:, 0].astype(jnp.int32)
    e2 = route[:, 1].astype(jnp.int32)
    slots, row_token, tile_expert, tile_live = moe_plan(e1, e2, wg.shape[0], MOE_TILE)
    ys = moe_grouped(xn, row_token, tile_expert, tile_live, wg, wu, wd)
    return moe_combine(ys, slots, x2, route, nf, min(MOE_TILE, x2.shape[0]), final_norm)


def _pad_lanes(v, width=LANES):
    return jnp.pad(v, ((0, 0), (0, width - v.shape[1])))


def _natural_rows(arr, n_rows):
    b, d, l, w = arr.shape
    tail = arr[:, :, l - n_rows // d:, :]
    return jnp.swapaxes(tail, 1, 2).reshape(b, n_rows, w)


def kernel(x_prompt, x_sample, cache_kv_w128, cache_kv_w512, cache_kv_w2048, state_ssm, state_conv,
           norm1, w_in, conv_w, conv_b, dt_bias, a_log, d_skip, ssm_norm, w_att_out, w_ssm_out, w_o,
           norm2, w_router_coarse, b_router_coarse, w_router_fine, b_router_fine,
           w_exp_gate, w_exp_up, w_exp_down, norm_f):
    bp, sp, dm = x_prompt.shape
    bs, ts, _ = x_sample.shape
    depth = w_in.shape[0]
    n_heads = dt_bias.shape[1]
    d_inner = n_heads * SSM_HEAD_DIM
    conv_dim = conv_w.shape[2]
    n_grp = len(ATT_DILATIONS)
    off_z = n_grp * 3 * ATT_WIDTH
    off_xbc = off_z + d_inner
    off_dt = off_xbc + conv_dim
    off_gate = off_dt + n_heads
    caches = (cache_kv_w128, cache_kv_w512, cache_kv_w2048)
    for g in range(n_grp):
        assert caches[g].shape[2] == ATT_WINDOWS[g] and sp % (ATT_DILATIONS[g] * ATT_BLOCK) == 0
    assert ts <= min(ATT_DILATIONS[1:]) and ts < TPAD and sp % SSM_CHUNK == 0

    tabs_p = rope_tables(jnp.arange(sp, dtype=F32))
    tabs_s = rope_tables(jnp.tile(jnp.arange(ts, dtype=F32) + PAST_LEN, bs))
    expand = (jnp.arange(LANES)[:, None] == (jnp.arange(d_inner) // SSM_HEAD_DIM)[None, :]).astype(F32)

    xp = x_prompt.reshape(bp * sp, dm)
    xs = x_sample.reshape(bs * ts, dm)
    outs = {k: [] for k in ("kvp0", "kvp1", "kvp2", "ssm_p", "conv_p", "kvs0", "kvs1", "kvs2", "ssm_s", "conv_s")}
    n_s = bs * ts
    for layer in range(depth):
        w = w_in[layer].astype(BF16)
        w_qkv = [w[:, g * 3 * ATT_WIDTH:(g + 1) * 3 * ATT_WIDTH] for g in range(n_grp)]
        w_z, w_xbc, w_gate = w[:, off_z:off_xbc], w[:, off_xbc:off_dt], w[:, off_gate:]
        w_dt = _pad_lanes(w[:, off_dt:off_gate])
        dtb = _pad_lanes(dt_bias[layer][None])
        alog = _pad_lanes(a_log[layer][None])
        dsk_e = jnp.repeat(d_skip[layer], SSM_HEAD_DIM)[None]
        nw = ssm_norm[layer][None]
        cw, cb = conv_w[layer], conv_b[layer][None]
        w_router = _pad_lanes(jnp.concatenate([w_router_coarse[layer], w_router_fine[layer]], axis=1)).astype(BF16)
        b_router = _pad_lanes(jnp.concatenate([b_router_coarse[layer], b_router_fine[layer]])[None])
        wts = (w_att_out[layer].astype(BF16), w_ssm_out[layer].astype(BF16), w_o[layer].astype(BF16),
               norm2[layer][None], w_router, b_router)
        wg, wu, wd = (w_exp_gate[layer].astype(BF16), w_exp_up[layer].astype(BF16),
                      w_exp_down[layer].astype(BF16))

        xn = rmsnorm_bf16(xp, norm1[layer], 1024)
        att = []
        for g, d in enumerate(ATT_DILATIONS):
            qkv = qkv_proj(xn, w_qkv[g], tabs_p, bp, sp, d, 1024, BF16, f"qkv_prompt_d{d}")
            att.append(attn_prompt(qkv))
            wnd = min(ATT_WINDOWS[g], sp)
            kv = jnp.stack([_natural_rows(qkv[1], wnd), _natural_rows(qkv[2], wnd)], axis=2)
            outs[f"kvp{g}"].append(kv.astype(F32).reshape(bp, wnd, 2, ATT_HEADS, ATT_HEAD_DIM))
        z = matmul(xn, w_z, BF16, 2048, 1024, "proj_z")
        xbc = matmul(xn, w_xbc, BF16, 2048, 1024, "proj_xbc")
        gates = matmul(xn, w_gate, BF16, 2048, 1024, "proj_gates")
        dt_raw = matmul(xn, w_dt, F32, 2048, LANES, "proj_dt")
        y_ssm, h_p = ssd_prompt(xbc, z, dt_raw, cw, cb, dtb, alog, dsk_e, nw, bp, sp)
        outs["ssm_p"].append(h_p.reshape(bp, n_heads, SSM_HEAD_DIM, SSM_STATE))
        outs["conv_p"].append(xbc.reshape(bp, sp, conv_dim)[:, sp - (SSM_CONV - 1):].astype(F32))
        x2, xn2, cmb = outproj(xp, gates, y_ssm, att, ATT_DILATIONS, bp, sp, wts, 256)
        xp = moe_final(xn2, cmb, x2, wg, wu, wd, norm_f[None], layer == depth - 1)

        xn = rmsnorm_bf16(xs, norm1[layer], n_s)
        pad_t = lambda a: jnp.pad(a.reshape(bs, ts, a.shape[-1]), ((0, 0), (0, TPAD - ts), (0, 0)))
        new = jnp.stack([qkv_proj(xn, w_qkv[g], tabs_s, 1, n_s, 1, n_s, F32, f"qkv_sample_{g}")
                         .reshape(3, bs, ts, ATT_WIDTH) for g in range(n_grp)])
        new = jnp.pad(new, ((0, 0), (0, 0), (0, 0), (0, TPAD - ts), (0, 0)))
        cl = [jnp.transpose(caches[g][layer], (0, 2, 3, 4, 1)) for g in range(n_grp)]
        res = cache_attn(new, cl, ts)
        attn_s = res[0][:, :ts]
        for g in range(n_grp):
            outs[f"kvs{g}"].append(jnp.transpose(res[1 + g], (0, 4, 1, 2, 3)))
        z = matmul(xn, w_z, F32, n_s, 512, "proj_z_s")
        xbc = matmul(xn, w_xbc, F32, n_s, 512, "proj_xbc_s")
        gates = matmul(xn, w_gate, F32, n_s, 512, "proj_gates_s")
        dt_raw = matmul(xn, w_dt, F32, n_s, LANES, "proj_dt_s")
        y_s, h_s = ssd_sample(pad_t(xbc), pad_t(z), pad_t(dt_raw), state_conv[layer],
                              state_ssm[layer].reshape(bs, d_inner, SSM_STATE), cw, cb, dtb, alog, dsk_e, nw,
                              expand, ts)
        outs["ssm_s"].append(h_s.reshape(bs, n_heads, SSM_HEAD_DIM, SSM_STATE))
        hist = jnp.concatenate([state_conv[layer], xbc.reshape(bs, ts, conv_dim)], axis=1)
        outs["conv_s"].append(hist[:, ts:])
        y_s = y_s[:, :ts].reshape(n_s, d_inner)
        x2, xn2, cmb = outproj(xs, gates, y_s, attn_s.reshape(n_s, ATT_WIDTH), (), 1, n_s, wts, min(256, n_s))
        xs = moe_final(xn2, cmb, x2, wg, wu, wd, norm_f[None], layer == depth - 1)

    st = lambda k: jnp.stack(outs[k])
    return (xp.reshape(bp, sp, dm), xs.reshape(bs, ts, dm),
            st("kvp0"), st("kvp1"), st("kvp2"), st("ssm_p"), st("conv_p"),
            st("kvs0"), st("kvs1"), st("kvs2"), st("ssm_s"), st("conv_s"))
```

```python
import functools
import math

import jax
import jax.numpy as jnp
from jax import lax
from jax.experimental import pallas as pl
from jax.experimental.pallas import tpu as pltpu

F32 = jnp.float32
BF16 = jnp.bfloat16

PAST_LEN = 8192
ATT_WINDOWS = (128, 512, 2048)
ATT_DILATIONS = (1, 4, 16)
ATT_HEADS = 8
ATT_HEAD_DIM = 64
ATT_WIDTH = ATT_HEADS * ATT_HEAD_DIM
ATT_SCALE = ATT_HEAD_DIM ** -0.5
ROT_DIM = ATT_HEAD_DIM // 4
ROPE_THETA = 500000.0
ATT_BLOCK = 128

SSM_HEAD_DIM = 64
SSM_STATE = 128
SSM_GROUPS = 4
SSM_CONV = 4
SSM_CHUNK = 128
MOE_GROUPS = 4
MOE_PER_GROUP = 4
MOE_EXPERTS = MOE_GROUPS * MOE_PER_GROUP
RMS_EPS = 1e-6
SSM_NORM_EPS = 1e-5

LANES = 128
VMEM_LIMIT = 56 * 1024 * 1024
NEG_INF = float("-inf")


def _cparams(n_axes):
    return pltpu.CompilerParams(dimension_semantics=("arbitrary",) * n_axes,
                                vmem_limit_bytes=VMEM_LIMIT)


def _sigmoid(x):
    return 1.0 / (1.0 + jnp.exp(-x))


def _silu(x):
    return x * _sigmoid(x)


def _norm_kernel(x_ref, g_ref, o_ref):
    x = x_ref[...]
    y = x * lax.rsqrt(jnp.mean(x * x, axis=-1, keepdims=True) + RMS_EPS)
    o_ref[...] = (y * g_ref[...]).astype(o_ref.dtype)


def rmsnorm_bf16(x, g, tm):
    n, d = x.shape
    return pl.pallas_call(
        _norm_kernel, grid=(n // tm,),
        in_specs=[pl.BlockSpec((tm, d), lambda i: (i, 0)), pl.BlockSpec((1, d), lambda i: (0, 0))],
        out_specs=pl.BlockSpec((tm, d), lambda i: (i, 0)),
        out_shape=jax.ShapeDtypeStruct((n, d), BF16),
        compiler_params=_cparams(1), name="rmsnorm")(x, g.reshape(1, d))


def _mm_kernel(x_ref, w_ref, o_ref):
    o_ref[...] = jnp.dot(x_ref[...], w_ref[...], preferred_element_type=F32).astype(o_ref.dtype)


def matmul(x, w, out_dtype, tm, tn, name):
    m, k = x.shape
    n = w.shape[1]
    return pl.pallas_call(
        _mm_kernel, grid=(m // tm, n // tn),
        in_specs=[pl.BlockSpec((tm, k), lambda i, j: (i, 0)), pl.BlockSpec((k, tn), lambda i, j: (0, j))],
        out_specs=pl.BlockSpec((tm, tn), lambda i, j: (i, j)),
        out_shape=jax.ShapeDtypeStruct((m, n), out_dtype),
        compiler_params=_cparams(2), name=name)(x, w)


QKV_CHUNK = 256


def _qkv_kernel(x_ref, w_ref, cos_ref, sa_ref, sb_ref, o_ref, acc_ref, *, d):
    sc = jnp.where(pl.program_id(1) == 0, ATT_SCALE, 1.0).astype(F32)
    tm = x_ref.shape[0]
    ck = min(QKV_CHUNK, tm)
    for c0 in range(0, tm, ck):
        rows = slice(c0, c0 + ck)
        acc = jnp.dot(x_ref[rows, :], w_ref[...], preferred_element_type=F32)
        c = cos_ref[rows, :] * sc
        sa = sa_ref[rows, :] * sc
        sb = sb_ref[rows, :] * sc
        for ch in range(ATT_WIDTH // LANES):
            cs = slice(ch * LANES, (ch + 1) * LANES)
            t = acc[:, cs]
            r = t * c + pltpu.roll(t, LANES - ROT_DIM // 2, 1) * sa + pltpu.roll(t, ROT_DIM // 2, 1) * sb
            if d == 1:
                o_ref[0, rows, cs] = r.astype(o_ref.dtype)
            else:
                acc_ref[ch, rows, :] = r
                for res in range(d):
                    o_ref[res, c0 // d:(c0 + ck) // d, cs] = acc_ref[
                        ch, pl.ds(c0 + res, ck // d, stride=d), :].astype(o_ref.dtype)


def qkv_proj(xn, w, tabs, batch, seq, d, tm, out_dtype, name):
    n, dm = xn.shape
    tpb = seq // tm
    cos, sa, sb = tabs
    tab_spec = pl.BlockSpec((None, tm, LANES), lambda i, j: (jnp.where(j == 2, 1, 0), i % tpb, 0))
    return pl.pallas_call(
        functools.partial(_qkv_kernel, d=d), grid=(n // tm, 3),
        in_specs=[pl.BlockSpec((tm, dm), lambda i, j: (i, 0)),
                  pl.BlockSpec((dm, ATT_WIDTH), lambda i, j: (0, j)),
                  tab_spec, tab_spec, tab_spec],
        out_specs=pl.BlockSpec((None, None, d, tm // d, ATT_WIDTH),
                               lambda i, j: (j, i // tpb, 0, i % tpb, 0)),
        out_shape=jax.ShapeDtypeStruct((3, batch, d, seq // d, ATT_WIDTH), out_dtype),
        scratch_shapes=[pltpu.VMEM((ATT_WIDTH // LANES, tm, LANES), F32)],
        compiler_params=_cparams(2), name=name)(xn, w, cos, sa, sb)


def rope_tables(pos):
    half = ROT_DIM // 2
    inv = ROPE_THETA ** (-jnp.arange(half, dtype=F32) / half)
    ang = pos[:, None] * inv[None, :]
    cos = jnp.cos(ang)
    sin = jnp.sin(ang)
    s = pos.shape[0]
    ones = jnp.ones((s, ATT_HEAD_DIM - ROT_DIM), F32)
    zeros = jnp.zeros((s, ATT_HEAD_DIM - ROT_DIM), F32)
    zh = jnp.zeros((s, half), F32)
    c = jnp.concatenate([cos, cos, ones], axis=1)
    sa = jnp.concatenate([-sin, zh, zeros], axis=1)
    sb = jnp.concatenate([zh, sin, zeros], axis=1)
    rep = LANES // ATT_HEAD_DIM
    c, sa, sb = (jnp.tile(t, (1, rep)) for t in (c, sa, sb))
    return (jnp.stack([c, jnp.ones_like(c)]), jnp.stack([sa, jnp.zeros_like(sa)]),
            jnp.stack([sb, jnp.zeros_like(sb)]))


ATT_ROW_CHUNK = 32
ATT_SUB = 2


def _attn_kernel(q_ref, kc_ref, vc_ref, kp_ref, vp_ref, num_ref, st_ref, s_scr, p_scr):
    nb = pl.program_id(2)
    blk = ATT_BLOCK
    nt = (((1,), (1,)), ((), ()))
    heads = [slice(h * ATT_HEAD_DIM, (h + 1) * ATT_HEAD_DIM) for h in range(ATT_HEADS)]
    subs = [slice(sb * blk, (sb + 1) * blk) for sb in range(ATT_SUB)]

    def prev(ref_p, ref_c, sb, sl):
        return ref_p[:, sl] if sb == 0 else ref_c[subs[sb - 1], sl]

    for sb, rows in enumerate(subs):
        for h, sl in enumerate(heads):
            q = q_ref[rows, sl]
            s_scr[sb, h, :, 0:blk] = lax.dot_general(q, prev(kp_ref, kc_ref, sb, sl), nt,
                                                     preferred_element_type=F32)
            s_scr[sb, h, :, blk:2 * blk] = lax.dot_general(q, kc_ref[rows, sl], nt, preferred_element_type=F32)
    rc = ATT_ROW_CHUNK
    qi = lax.broadcasted_iota(jnp.int32, (rc, 2 * blk), 0)
    kj = lax.broadcasted_iota(jnp.int32, (rc, 2 * blk), 1)
    lane = lax.broadcasted_iota(jnp.int32, (rc, LANES), 1)
    for sb, rows in enumerate(subs):
        for r0 in range(0, blk, rc):
            qa = qi + r0
            in_prev = jnp.logical_and(kj < blk, kj >= qa)
            if sb == 0:
                in_prev = jnp.logical_and(in_prev, nb > 0)
            mask = jnp.logical_or(in_prev, jnp.logical_and(kj >= blk, kj - blk <= qa))
            st = jnp.zeros((rc, LANES), F32)
            for h in range(ATT_HEADS):
                s = jnp.where(mask, s_scr[sb, h, r0:r0 + rc, :], NEG_INF)
                m = jnp.max(s, axis=-1, keepdims=True)
                p = jnp.exp(s - m)
                l = jnp.sum(p, axis=-1, keepdims=True)
                p_scr[sb, h, r0:r0 + rc, :] = p.astype(BF16)
                st = jnp.where(lane == h, m, st)
                st = jnp.where(lane == ATT_HEADS + h, l, st)
            st_ref[sb * blk + r0:sb * blk + r0 + rc, :] = st
    for sb, rows in enumerate(subs):
        for h, sl in enumerate(heads):
            num_ref[rows, sl] = (
                jnp.dot(p_scr[sb, h, :, 0:blk], prev(vp_ref, vc_ref, sb, sl), preferred_element_type=F32)
                + jnp.dot(p_scr[sb, h, :, blk:2 * blk], vc_ref[rows, sl], preferred_element_type=F32))


def attn_prompt(qkv):
    _, b, d, l, w = qkv.shape
    step = ATT_SUB * ATT_BLOCK
    nstep = l // step

    def spec(kind, prev):
        if prev:
            return pl.BlockSpec((None, None, None, ATT_BLOCK, w),
                                lambda bi, r, n: (kind, bi, r, jnp.maximum(ATT_SUB * n - 1, 0), 0))
        return pl.BlockSpec((None, None, None, step, w), lambda bi, r, n: (kind, bi, r, n, 0))

    return pl.pallas_call(
        _attn_kernel, grid=(b, d, nstep),
        in_specs=[spec(0, False), spec(1, False), spec(2, False), spec(1, True), spec(2, True)],
        out_specs=[pl.BlockSpec((None, None, step, w), lambda bi, r, n: (bi, r, n, 0)),
                   pl.BlockSpec((None, None, step, LANES), lambda bi, r, n: (bi, r, n, 0))],
        out_shape=[jax.ShapeDtypeStruct((b, d, l, w), F32), jax.ShapeDtypeStruct((b, d, l, LANES), F32)],
        scratch_shapes=[pltpu.VMEM((ATT_SUB, ATT_HEADS, ATT_BLOCK, 2 * ATT_BLOCK), F32),
                        pltpu.VMEM((ATT_SUB, ATT_HEADS, ATT_BLOCK, 2 * ATT_BLOCK), BF16)],
        compiler_params=_cparams(3), name=f"attn_prompt_d{d}")(qkv, qkv, qkv, qkv, qkv)


def _col(mat, h, n_lanes=LANES):
    return jnp.broadcast_to(mat[:, h:h + 1], (mat.shape[0], n_lanes))


def _ssd_kernel(xbc_ref, z_ref, dt_ref, cw_ref, cb_ref, dtb_ref, alog_ref, dsk_ref, nw_ref,
                y_ref, hout_ref,
                cbuf, xs_s, xw_s, eac_s, cd_s, y_s, ht_s, *, d_inner, n_groups):
    c = pl.program_id(1)
    nc = pl.num_programs(1)
    q = SSM_CHUNK
    n_st = SSM_STATE
    gw = d_inner // n_groups
    conv_dim = xbc_ref.shape[1]
    top = 8

    @pl.when(c == 0)
    def _():
        cbuf[0:top, :] = jnp.zeros((top, conv_dim), F32)
        ht_s[...] = jnp.zeros(ht_s.shape, F32)

    cbuf[top:top + q, :] = xbc_ref[...].astype(F32)
    cwid = 512
    for j in range(conv_dim // cwid):
        cs = slice(j * cwid, (j + 1) * cwid)
        acc = cb_ref[:, cs] + cbuf[top - 3:top - 3 + q, cs] * cw_ref[0:1, cs]
        for i in range(1, SSM_CONV):
            acc = acc + cbuf[top - 3 + i:top - 3 + i + q, cs] * cw_ref[i:i + 1, cs]
        xs_s[:, cs] = _silu(acc)
    cbuf[top - 3:top, :] = cbuf[top + q - 3:top + q, :]

    dt = jax.nn.softplus(dt_ref[...] + dtb_ref[...])
    a_row = -jnp.exp(alog_ref[...])
    a = dt * a_row
    ri = lax.broadcasted_iota(jnp.int32, (q, q), 0)
    ci = lax.broadcasted_iota(jnp.int32, (q, q), 1)
    causal = ri >= ci
    tril = jnp.where(causal, 1.0, 0.0).astype(F32)
    acum = jnp.dot(tril, a, preferred_element_type=F32, precision=lax.Precision.HIGHEST)
    acum_t = acum.T
    lane = lax.broadcasted_iota(jnp.int32, (q, LANES), 1)
    lo = lane < SSM_HEAD_DIM

    hpg = gw // SSM_HEAD_DIM
    for g in range(n_groups):
        bc = xs_s[:, d_inner + g * n_st:d_inner + (g + 1) * n_st].astype(BF16)
        cc = xs_s[:, d_inner + n_groups * n_st + g * n_st:d_inner + n_groups * n_st + (g + 1) * n_st].astype(BF16)
        cb = lax.dot_general(cc, bc, (((1,), (1,)), ((), ())), preferred_element_type=F32)
        for jp in range(hpg // 2):
            h0 = g * hpg + 2 * jp
            ls = slice(g * gw + jp * LANES, g * gw + (jp + 1) * LANES)
            ac0 = _col(acum, h0)
            ac1 = _col(acum, h0 + 1)
            acum_e = jnp.where(lo, ac0, ac1)
            dt_e = jnp.where(lo, _col(dt, h0), _col(dt, h0 + 1))
            xdt = xs_s[:, ls] * dt_e
            acl_e = acum_e[q - 1:q, :]
            xw_s[:, ls] = (xdt * jnp.exp(acl_e - acum_e)).astype(BF16)
            eac_s[:, ls] = jnp.exp(acum_e)
            cd_s[:, ls] = jnp.exp(acl_e)
            xdt_b = xdt.astype(BF16)
            zero = jnp.zeros_like(xdt_b)
            m0 = (jnp.exp(jnp.where(causal, ac0 - acum_t[h0:h0 + 1, :], NEG_INF)) * cb).astype(BF16)
            m1 = (jnp.exp(jnp.where(causal, ac1 - acum_t[h0 + 1:h0 + 2, :], NEG_INF)) * cb).astype(BF16)
            y_s[:, ls] = (jnp.dot(m0, jnp.where(lo, xdt_b, zero), preferred_element_type=F32)
                          + jnp.dot(m1, jnp.where(lo, zero, xdt_b), preferred_element_type=F32))
        gs = slice(g * gw, (g + 1) * gw)
        h_prev = ht_s[g]
        y_off = jnp.dot(cc, h_prev.astype(BF16), preferred_element_type=F32) * eac_s[:, gs]
        y_s[:, gs] = y_s[:, gs] + y_off
        st = lax.dot_general(bc, xw_s[:, gs], (((0,), (0,)), ((), ())), preferred_element_type=F32)
        ht_s[g] = cd_s[:, gs] * h_prev + st

    for g in range(n_groups):
        gs = slice(g * gw, (g + 1) * gw)
        y = y_s[:, gs] + dsk_ref[:, gs] * xs_s[:, gs]
        yf = y * _silu(z_ref[:, gs].astype(F32))
        yf = yf * lax.rsqrt(jnp.mean(yf * yf, axis=-1, keepdims=True) + SSM_NORM_EPS)
        y_ref[:, gs] = (yf * nw_ref[:, gs]).astype(y_ref.dtype)

    @pl.when(c == nc - 1)
    def _():
        for g in range(n_groups):
            hout_ref[g * gw:(g + 1) * gw, :] = ht_s[g].T


def ssd_prompt(xbc, z, dt_raw, conv_w, conv_b, dtb, alog, dsk_e, nw, batch, seq):
    n, conv_dim = xbc.shape
    d_inner = z.shape[1]
    n_groups = SSM_GROUPS
    gw = d_inner // n_groups
    q = SSM_CHUNK
    cps = seq // q
    row = lambda w: pl.BlockSpec((1, w), lambda b, c: (0, 0))
    tok = lambda w: pl.BlockSpec((q, w), lambda b, c: (b * cps + c, 0))
    return pl.pallas_call(
        functools.partial(_ssd_kernel, d_inner=d_inner, n_groups=n_groups),
        grid=(batch, cps),
        in_specs=[tok(conv_dim), tok(d_inner), tok(LANES),
                  pl.BlockSpec((SSM_CONV, conv_dim), lambda b, c: (0, 0)), row(conv_dim),
                  row(LANES), row(LANES), row(d_inner), row(d_inner)],
        out_specs=[tok(d_inner), pl.BlockSpec((None, d_inner, SSM_STATE), lambda b, c: (b, 0, 0))],
        out_shape=[jax.ShapeDtypeStruct((n, d_inner), BF16),
                   jax.ShapeDtypeStruct((batch, d_inner, SSM_STATE), F32)],
        scratch_shapes=[pltpu.VMEM((8 + q, conv_dim), F32),
                        pltpu.VMEM((q, conv_dim), F32),
                        pltpu.VMEM((q, d_inner), BF16),
                        pltpu.VMEM((q, d_inner), F32),
                        pltpu.VMEM((1, d_inner), F32),
                        pltpu.VMEM((q, d_inner), F32),
                        pltpu.VMEM((n_groups, SSM_STATE, gw), F32)],
        compiler_params=_cparams(2), name="ssd_prompt")(xbc, z, dt_raw, conv_w, conv_b, dtb, alog, dsk_e, nw)


TPAD = 8


def _ssd_step_kernel(xbc_ref, z_ref, dt_ref, cst_ref, h0_ref, cw_ref, cb_ref, dtb_ref, alog_ref,
                     dsk_ref, nw_ref, exp_ref, y_ref, hout_ref,
                     cbuf, xs_s, f_s, *, d_inner, n_groups, n_tok):
    b = pl.program_id(0)
    n_st = SSM_STATE
    gw = d_inner // n_groups
    conv_dim = xbc_ref.shape[1]
    top = 8
    hist = SSM_CONV - 1

    @pl.when(b == 0)
    def _():
        cbuf[0:top, :] = jnp.zeros((top, conv_dim), F32)

    cbuf[top - hist:top, :] = cst_ref[...]
    cbuf[top:top + TPAD, :] = xbc_ref[...]
    acc = cb_ref[...] + cbuf[top - hist:top - hist + TPAD, :] * cw_ref[0:1, :]
    for i in range(1, SSM_CONV):
        acc = acc + cbuf[top - hist + i:top - hist + i + TPAD, :] * cw_ref[i:i + 1, :]
    xs_s[...] = _silu(acc)

    rid = lax.broadcasted_iota(jnp.int32, (TPAD, LANES), 0)
    dt = jax.nn.softplus(dt_ref[...] + dtb_ref[...])
    da = dt * (-jnp.exp(alog_ref[...]))
    cum = da
    for k in range(1, n_tok):
        cum = cum + jnp.where(rid >= k, pltpu.roll(da, k, 0), 0.0)
    facs = [dt, jnp.exp(cum)]
    for t in range(n_tok):
        facs.append(jnp.where(rid <= t, jnp.exp(cum[t:t + 1, :] - cum), 0.0))
    hi = lax.Precision.HIGHEST
    f_s[...] = jnp.dot(jnp.concatenate(facs, axis=0), exp_ref[...], preferred_element_type=F32, precision=hi)

    nt = (((1,), (1,)), ((), ()))
    row8 = lax.broadcasted_iota(jnp.int32, (TPAD, gw), 0)
    for g in range(n_groups):
        gs = slice(g * gw, (g + 1) * gw)
        xdt = xs_s[:, gs] * f_s[0:TPAD, gs]
        b_f = xs_s[:, d_inner + g * n_st:d_inner + (g + 1) * n_st]
        c_f = xs_s[:, d_inner + (n_groups + g) * n_st:d_inner + (n_groups + g + 1) * n_st]
        b16 = b_f.astype(BF16)
        c16 = c_f.astype(BF16)
        h0g = h0_ref[gs, :]
        y = f_s[TPAD:2 * TPAD, gs] * lax.dot_general(c16, h0g.astype(BF16), nt, preferred_element_type=F32)
        bc = lax.dot_general(b16, c16, nt, preferred_element_type=F32)
        for t in range(n_tok):
            term = bc[:, t:t + 1] * f_s[(2 + t) * TPAD:(3 + t) * TPAD, gs] * xdt
            y = y + jnp.where(row8 == t, jnp.sum(term, axis=0, keepdims=True), 0.0)
        yv = y + dsk_ref[:, gs] * xs_s[:, gs]
        yf = yv * _silu(z_ref[:, gs])
        yf = yf * lax.rsqrt(jnp.mean(yf * yf, axis=-1, keepdims=True) + SSM_NORM_EPS)
        y_ref[:, gs] = yf * nw_ref[:, gs]

        dend = f_s[(1 + n_tok) * TPAD:(2 + n_tok) * TPAD, gs]
        pend = f_s[TPAD + n_tok - 1:TPAD + n_tok, gs]
        lhs = jnp.where(row8 == n_tok, pend, dend * xdt)
        rhs = jnp.concatenate([jnp.where(rid < n_tok, b_f, 0.0), jnp.where(rid == n_tok, 1.0, 0.0)], axis=1)
        res = lax.dot_general(lhs, rhs, (((0,), (0,)), ((), ())), preferred_element_type=F32, precision=hi)
        hout_ref[gs, :] = res[:, n_st:] * h0g + res[:, :n_st]


def ssd_sample(xbc, z, dt_raw, conv_state, h0, conv_w, conv_b, dtb, alog, dsk_e, nw, expand, n_tok):
    bsz, _, conv_dim = xbc.shape
    d_inner = z.shape[2]
    row = lambda w: pl.BlockSpec((1, w), lambda b: (0, 0))
    tok = lambda w: pl.BlockSpec((None, TPAD, w), lambda b: (b, 0, 0))
    st = pl.BlockSpec((None, d_inner, SSM_STATE), lambda b: (b, 0, 0))
    return pl.pallas_call(
        functools.partial(_ssd_step_kernel, d_inner=d_inner, n_groups=SSM_GROUPS, n_tok=n_tok),
        grid=(bsz,),
        in_specs=[tok(conv_dim), tok(d_inner), tok(LANES),
                  pl.BlockSpec((None, SSM_CONV - 1, conv_dim), lambda b: (b, 0, 0)), st,
                  pl.BlockSpec((SSM_CONV, conv_dim), lambda b: (0, 0)), row(conv_dim),
                  row(LANES), row(LANES), row(d_inner), row(d_inner),
                  pl.BlockSpec((LANES, d_inner), lambda b: (0, 0))],
        out_specs=[tok(d_inner), st],
        out_shape=[jax.ShapeDtypeStruct((bsz, TPAD, d_inner), F32),
                   jax.ShapeDtypeStruct((bsz, d_inner, SSM_STATE), F32)],
        scratch_shapes=[pltpu.VMEM((8 + TPAD, conv_dim), F32),
                        pltpu.VMEM((TPAD, conv_dim), F32),
                        pltpu.VMEM(((2 + n_tok) * TPAD, d_inner), F32)],
        compiler_params=_cparams(1), name="ssd_sample")(
            xbc, z, dt_raw, conv_state, h0, conv_w, conv_b, dtb, alog, dsk_e, nw, expand)


HEADS_PER_STEP = 4


def _cache_attn_kernel(new_ref, c0_ref, c1_ref, c2_ref, o_ref, n0_ref, n1_ref, n2_ref, stage, tts, *, n_tok):
    first = jnp.logical_and(pl.program_id(0) == 0, pl.program_id(1) == 0)

    @pl.when(first)
    def _():
        stage[...] = jnp.zeros(stage.shape, F32)

    crefs = (c0_ref, c1_ref, c2_ref)
    orefs = (n0_ref, n1_ref, n2_ref)
    hd = ATT_HEAD_DIM
    nt = (((1,), (1,)), ((), ()))
    lane = lax.broadcasted_iota(jnp.int32, (hd, LANES), 1)
    t_new = lax.broadcasted_iota(jnp.int32, (TPAD, TPAD), 0)
    u_new = lax.broadcasted_iota(jnp.int32, (TPAD, TPAD), 1)

    for head in range(HEADS_PER_STEP):
        hs = slice(head * hd, (head + 1) * hd)
        parts = []
        for g, d in enumerate(ATT_DILATIONS):
            w = crefs[g].shape[-1]
            q = new_ref[g, 0, :, hs].astype(BF16)
            kn = new_ref[g, 1, :, hs].astype(BF16)
            vn = new_ref[g, 2, :, hs].astype(BF16)
            t_id = lax.broadcasted_iota(jnp.int32, (TPAD, w), 0)
            r_id = lax.broadcasted_iota(jnp.int32, (TPAD, w), 1)
            s = jnp.dot(q, crefs[g][0, head].astype(BF16), preferred_element_type=F32)
            s = jnp.where((r_id >= t_id) if d == 1 else ((r_id & (d - 1)) == t_id), s, NEG_INF)
            sn = lax.dot_general(q, kn, nt, preferred_element_type=F32)
            sn = jnp.where((u_new <= t_new) if d == 1 else (u_new == t_new), sn, NEG_INF)
            m = jnp.maximum(jnp.max(s, axis=1, keepdims=True), jnp.max(sn, axis=1, keepdims=True))
            p = jnp.exp(s - m)
            pn = jnp.exp(sn - m)
            l = jnp.sum(p, axis=1, keepdims=True) + jnp.sum(pn, axis=1, keepdims=True)
            num = (lax.dot_general(p.astype(BF16), crefs[g][1, head].astype(BF16), nt,
                                   preferred_element_type=F32)
                   + jnp.dot(pn.astype(BF16), vn, preferred_element_type=F32))
            parts.append((num, m, l))
        mx = jnp.maximum(jnp.maximum(parts[0][1], parts[1][1]), parts[2][1])
        wg = [jnp.exp(p_[1] - mx) for p_ in parts]
        num = wg[0] * parts[0][0]
        for g in range(1, 3):
            num = num + wg[g] * parts[g][0]
        den = wg[0] * parts[0][2] + wg[1] * parts[1][2] + wg[2] * parts[2][2]
        o_ref[:, hs] = num / den

    for pair in range(HEADS_PER_STEP // 2):
        ps = slice(pair * LANES, (pair + 1) * LANES)
        for g in range(3):
            for kv in range(2):
                stage[0:TPAD, :] = new_ref[g, 1 + kv, :, ps]
                tts[2 * g + kv] = stage[...].T
        for hh in range(2):
            head = pair * 2 + hh
            hs = slice(hh * hd, (hh + 1) * hd)
            for g in range(3):
                n_col = crefs[g].shape[-1] // LANES
                for kv in range(2):
                    cur = pltpu.roll(crefs[g][kv, head, :, 0:LANES], LANES - n_tok, 1)
                    for j in range(n_col):
                        if j + 1 < n_col:
                            nxt = pltpu.roll(crefs[g][kv, head, :, (j + 1) * LANES:(j + 2) * LANES],
                                             LANES - n_tok, 1)
                        else:
                            nxt = pltpu.roll(tts[2 * g + kv, hs, :], LANES - n_tok, 1)
                        orefs[g][kv, head, :, j * LANES:(j + 1) * LANES] = jnp.where(
                            lane < LANES - n_tok, cur, nxt)
                        cur = nxt


def cache_attn(new, caches, n_tok):
    bsz = new.shape[2]
    hps = HEADS_PER_STEP
    cspec = lambda c: pl.BlockSpec((None, 2, hps, ATT_HEAD_DIM, c.shape[-1]), lambda b, h: (b, 0, h, 0, 0))
    return pl.pallas_call(
        functools.partial(_cache_attn_kernel, n_tok=n_tok), grid=(bsz, ATT_HEADS // hps),
        in_specs=[pl.BlockSpec((3, 3, None, TPAD, hps * ATT_HEAD_DIM), lambda b, h: (0, 0, b, 0, h))]
        + [cspec(c) for c in caches],
        out_specs=[pl.BlockSpec((None, TPAD, hps * ATT_HEAD_DIM), lambda b, h: (b, 0, h))]
        + [cspec(c) for c in caches],
        out_shape=[jax.ShapeDtypeStruct((bsz, TPAD, ATT_WIDTH), F32)]
        + [jax.ShapeDtypeStruct(c.shape, c.dtype) for c in caches],
        scratch_shapes=[pltpu.VMEM((LANES, LANES), F32), pltpu.VMEM((6, LANES, LANES), F32)],
        compiler_params=_cparams(2), name="cache_attn")(new, *caches)


def _router(logits):
    lanef = lax.broadcasted_iota(jnp.int32, logits.shape, 1).astype(F32)
    big = 1e9
    lc = jnp.where(lanef < MOE_GROUPS, logits, NEG_INF)
    mc = jnp.max(lc, axis=-1, keepdims=True)
    g_sel = jnp.min(jnp.where(lc == mc, lanef, big), axis=-1, keepdims=True)
    p_sel = 1.0 / jnp.sum(jnp.exp(lc - mc), axis=-1, keepdims=True)
    base = MOE_GROUPS + MOE_PER_GROUP * g_sel
    lf = jnp.where(jnp.logical_and(lanef >= base, lanef < base + MOE_PER_GROUP), logits, NEG_INF)
    v1 = jnp.max(lf, axis=-1, keepdims=True)
    i1 = jnp.min(jnp.where(lf == v1, lanef, big), axis=-1, keepdims=True)
    lf2 = jnp.where(lanef == i1, NEG_INF, lf)
    v2 = jnp.max(lf2, axis=-1, keepdims=True)
    i2 = jnp.min(jnp.where(lf2 == v2, lanef, big), axis=-1, keepdims=True)
    e2 = jnp.exp(v2 - v1)
    den = 1.0 + e2
    w1 = (1.0 / den) * p_sel
    w2 = (e2 / den) * p_sel
    out = jnp.where(lanef == 0.0, i1 - MOE_GROUPS, 0.0)
    out = jnp.where(lanef == 1.0, i2 - MOE_GROUPS, out)
    out = jnp.where(lanef == 2.0, w1, out)
    return jnp.where(lanef == 3.0, w2, out)


OUTPROJ_CHUNK = 256


def _outproj_kernel(*refs, dils, tm):
    n_g = len(dils)
    if n_g:
        x_ref, xn1_ref, ys_ref = refs[:3]
        att_refs = refs[3:3 + 2 * n_g]
        rest = refs[3 + 2 * n_g:]
    else:
        x_ref, xn1_ref, ys_ref, attn_ref = refs[:4]
        rest = refs[4:]
    wa_ref, ws_ref, wo_ref, wg_ref, n2_ref, wr_ref, br_ref, x2_ref, xn_ref, cmb_ref = rest[:10]
    scr = rest[10:]

    dm = x_ref.shape[1]
    n_ch = ATT_WIDTH // LANES
    rc = min(OUTPROJ_CHUNK, tm)
    lane = lax.broadcasted_iota(jnp.int32, (rc, LANES), 1)
    lo = lane < ATT_HEAD_DIM
    for c0 in range(0, tm, rc):
        rows = slice(c0, c0 + rc)
        if n_g:
            nums, stats = [], []
            for g, d in enumerate(dils):
                num_ref, st_ref = att_refs[2 * g], att_refs[2 * g + 1]
                if d == 1:
                    nums.append([num_ref[0, rows, ch * LANES:(ch + 1) * LANES] for ch in range(n_ch)])
                    stats.append(st_ref[0, rows, :])
                else:
                    ns, ss = scr[2 * g], scr[2 * g + 1]
                    rr = slice(c0 // d, (c0 + rc) // d)
                    for r in range(d):
                        for ch in range(n_ch):
                            ns[ch, pl.ds(c0 + r, rc // d, stride=d), :] = num_ref[r, rr, ch * LANES:(ch + 1) * LANES]
                        ss[pl.ds(c0 + r, rc // d, stride=d), :] = st_ref[r, rr, :]
                    nums.append([ns[ch, rows, :] for ch in range(n_ch)])
                    stats.append(ss[rows, :])
            chunks = []
            for ch in range(n_ch):
                wts, dens = [], []
                for h in (2 * ch, 2 * ch + 1):
                    ms = [s[:, h:h + 1] for s in stats]
                    ls = [s[:, ATT_HEADS + h:ATT_HEADS + h + 1] for s in stats]
                    mx = ms[0]
                    for m in ms[1:]:
                        mx = jnp.maximum(mx, m)
                    w = [jnp.exp(m - mx) for m in ms]
                    den = w[0] * ls[0]
                    for g in range(1, n_g):
                        den = den + w[g] * ls[g]
                    wts.append(w)
                    dens.append(den)
                num = jnp.where(lo, wts[0][0], wts[1][0]) * nums[0][ch]
                for g in range(1, n_g):
                    num = num + jnp.where(lo, wts[0][g], wts[1][g]) * nums[g][ch]
                chunks.append((num / jnp.where(lo, dens[0], dens[1])).astype(BF16))
            attn = jnp.concatenate(chunks, axis=1)
        else:
            attn = attn_ref[rows, :].astype(BF16)

        a = jnp.dot(attn, wa_ref[...], preferred_element_type=F32)
        s = jnp.dot(ys_ref[rows, :].astype(BF16), ws_ref[...], preferred_element_type=F32)
        gates = jnp.dot(xn1_ref[rows, :], wg_ref[...], preferred_element_type=F32)
        mixed = _sigmoid(gates[:, :dm]) * a + _sigmoid(gates[:, dm:]) * s
        x2 = x_ref[rows, :] + jnp.dot(mixed.astype(BF16), wo_ref[...], preferred_element_type=F32)
        x2_ref[rows, :] = x2
        xn = x2 * lax.rsqrt(jnp.mean(x2 * x2, axis=-1, keepdims=True) + RMS_EPS) * n2_ref[...]
        _to_rows(xn_ref.at[pl.ds(c0 * (dm // LANES), rc * (dm // LANES)), :], xn)
        logits = jnp.dot(xn.astype(BF16), wr_ref[...], preferred_element_type=F32) + br_ref[...]
        cmb_ref[rows, :] = _router(logits)


def outproj(x, xn1, yssm, att, dils, batch, seq, weights, tm):
    n, dm = x.shape
    wa, ws, wo, wgate, n2, wr, br = weights
    tpb = seq // tm
    tok = lambda w: pl.BlockSpec((tm, w), lambda i: (i, 0))
    full = lambda a: pl.BlockSpec(a.shape, lambda i: (0,) * a.ndim)
    in_specs = [tok(dm), tok(dm), tok(yssm.shape[1])]
    args = [x, xn1, yssm]
    scratch = []
    if dils:
        for (num, st), d in zip(att, dils):
            for arr in (num, st):
                in_specs.append(pl.BlockSpec((None, d, tm // d, arr.shape[-1]),
                                             lambda i: (i // tpb, 0, i % tpb, 0)))
                args.append(arr)
                scratch.append(pltpu.VMEM((tm, LANES) if arr.shape[-1] == LANES
                                          else (arr.shape[-1] // LANES, tm, LANES), F32))
    else:
        in_specs.append(tok(att.shape[1]))
        args.append(att)
    in_specs += [full(wa), full(ws), full(wo), full(wgate), full(n2), full(wr), full(br)]
    args += [wa, ws, wo, wgate, n2, wr, br]
    return pl.pallas_call(
        functools.partial(_outproj_kernel, dils=tuple(dils), tm=tm), grid=(n // tm,),
        in_specs=in_specs,
        out_specs=[tok(dm), pl.BlockSpec((tm * (dm // LANES), LANES), lambda i: (i, 0)), tok(LANES)],
        out_shape=[jax.ShapeDtypeStruct((n, dm), F32), jax.ShapeDtypeStruct((n * (dm // LANES), LANES), F32),
                   jax.ShapeDtypeStruct((n, LANES), F32)],
        scratch_shapes=scratch,
        compiler_params=_cparams(1), name="outproj")(*args)


MOE_TILE = 256


def moe_plan(e1, e2, n_exp, tile):
    n = e1.shape[0]
    flat = jnp.stack([e1, e2], axis=1).reshape(-1)
    onehot = (flat[:, None] == jnp.arange(n_exp, dtype=jnp.int32)[None, :]).astype(jnp.int32)
    rank = jnp.sum((jnp.cumsum(onehot, axis=0) - onehot) * onehot, axis=1)
    counts = jnp.sum(onehot, axis=0)
    padded = (counts + tile - 1) // tile * tile
    ends = jnp.cumsum(padded)
    slot = (ends - padded)[flat] + rank
    n_tiles = (2 * n) // tile + n_exp
    tile_start = jnp.arange(n_tiles, dtype=jnp.int32) * tile
    tile_expert = jnp.minimum(jnp.searchsorted(ends, tile_start, side="right"), n_exp - 1).astype(jnp.int32)
    tile_live = (tile_start < ends[n_exp - 1]).astype(jnp.int32)
    _, order = lax.sort((flat, jnp.arange(2 * n, dtype=jnp.int32)), num_keys=1, is_stable=True)
    s_id = jnp.arange(n_tiles * tile, dtype=jnp.int32)
    e_s = jnp.repeat(tile_expert, tile)
    k = s_id - (ends - padded)[e_s]
    src = jnp.clip((jnp.cumsum(counts) - counts)[e_s] + k, 0, 2 * n - 1)
    row_token = jnp.where(k < counts[e_s], order[src] // 2, s_id % n)
    return slot.reshape(n, 2), row_token.reshape(n_tiles, 1, tile), tile_expert, tile_live


def _to_rows(ref, val):
    n, ch = val.shape[0], val.shape[1] // LANES
    for c in range(ch):
        ref[pl.ds(c, n, stride=ch), :] = val[:, c * LANES:(c + 1) * LANES]


def _from_rows(ref, start, n, ch):
    return [ref[pl.ds(start * ch + c, n, stride=ch), :] for c in range(ch)]


def _gather_rows(idx_cur, idx_next, src_hbm, buf, sem, ch):
    i = pl.program_id(0)
    n_idx = buf.shape[1] // ch
    slot = i % 2

    def row_copy(idx_ref, j, s):
        src = src_hbm.at[pl.ds(pl.multiple_of(idx_ref[0, j] * ch, ch), ch), :]
        return pltpu.make_async_copy(src, buf.at[s, pl.ds(j * ch, ch), :], sem.at[s])

    def wait_rows(s):
        pltpu.make_async_copy(src_hbm.at[pl.ds(0, n_idx * ch), :], buf.at[s], sem.at[s]).wait()

    @pl.when(i == 0)
    def _():
        def body(j, c):
            row_copy(idx_cur, j, 0).start()
            return c
        lax.fori_loop(0, n_idx, body, 0)

    wait_rows(slot)
    for j in range(n_idx):
        row_copy(idx_next, j, 1 - slot).start()
    return slot, lambda: wait_rows(1 - slot)


def _moe_group_kernel(te_ref, live_ref, tokc_ref, tokn_ref, x_hbm, wg_ref, wu_ref, wd_ref, o_ref, buf, sem):
    del te_ref
    i = pl.program_id(0)
    ch = wg_ref.shape[0] // LANES
    tile = buf.shape[1] // ch
    slot, drain = _gather_rows(tokc_ref, tokn_ref, x_hbm, buf, sem, ch)

    @pl.when(live_ref[i] > 0)
    def _():
        x = jnp.concatenate(_from_rows(buf.at[slot], 0, tile, ch), axis=1).astype(BF16)
        he = _silu(jnp.dot(x, wg_ref[...], preferred_element_type=F32)) * jnp.dot(
            x, wu_ref[...], preferred_element_type=F32)
        _to_rows(o_ref, jnp.dot(he.astype(BF16), wd_ref[...], preferred_element_type=F32))

    @pl.when(live_ref[i] == 0)
    def _():
        o_ref[...] = jnp.zeros(o_ref.shape, F32)

    pl.when(i == pl.num_programs(0) - 1)(drain)


def moe_grouped(x_rows, row_token, tile_expert, tile_live, wg, wu, wd):
    n_tiles, _, tile = row_token.shape
    dm, ff = wg.shape[1], wg.shape[2]
    ch = dm // LANES
    tok = lambda f: pl.BlockSpec((None, 1, tile), f, memory_space=pltpu.SMEM)
    grid_spec = pltpu.PrefetchScalarGridSpec(
        num_scalar_prefetch=2, grid=(n_tiles,),
        in_specs=[tok(lambda i, te, tl: (i, 0, 0)),
                  tok(lambda i, te, tl: (jnp.minimum(i + 1, n_tiles - 1), 0, 0)),
                  pl.BlockSpec(memory_space=pl.ANY),
                  pl.BlockSpec((None, dm, ff), lambda i, te, tl: (te[i], 0, 0)),
                  pl.BlockSpec((None, dm, ff), lambda i, te, tl: (te[i], 0, 0)),
                  pl.BlockSpec((None, ff, dm), lambda i, te, tl: (te[i], 0, 0))],
        out_specs=pl.BlockSpec((tile * ch, LANES), lambda i, te, tl: (i, 0)),
        scratch_shapes=[pltpu.VMEM((2, tile * ch, LANES), F32), pltpu.SemaphoreType.DMA((2,))])
    return pl.pallas_call(
        _moe_group_kernel, grid_spec=grid_spec,
        out_shape=jax.ShapeDtypeStruct((n_tiles * tile * ch, LANES), F32),
        compiler_params=_cparams(1), name="moe_grouped")(
            tile_expert, tile_live, row_token, row_token, x_rows, wg, wu, wd)


def _moe_combine_kernel(sc_ref, sn_ref, ys_hbm, x2_ref, rt_ref, nf_ref, o_ref, buf, sem, *, final_norm):
    tm, dm = x2_ref.shape
    ch = dm // LANES
    slot, drain = _gather_rows(sc_ref, sn_ref, ys_hbm, buf, sem, ch)
    y1 = _from_rows(buf.at[slot], 0, tm, ch)
    y2 = _from_rows(buf.at[slot], tm, tm, ch)
    rt = rt_ref[...]
    w1, w2 = rt[:, 2:3], rt[:, 3:4]
    xf = jnp.concatenate([x2_ref[:, c * LANES:(c + 1) * LANES] + (w1 * y1[c] + w2 * y2[c]) for c in range(ch)],
                         axis=1)
    if final_norm:
        xf = xf * lax.rsqrt(jnp.mean(xf * xf, axis=-1, keepdims=True) + RMS_EPS) * nf_ref[...]
    o_ref[...] = xf
    pl.when(pl.program_id(0) == pl.num_programs(0) - 1)(drain)


def moe_combine(ys_rows, slots, x2, route, nf, tm, final_norm):
    n, dm = x2.shape
    nt = n // tm
    ch = dm // LANES
    srows = slots.reshape(nt, tm, 2).transpose(0, 2, 1).reshape(nt, 1, 2 * tm)
    sspec = lambda f: pl.BlockSpec((None, 1, 2 * tm), f, memory_space=pltpu.SMEM)
    return pl.pallas_call(
        functools.partial(_moe_combine_kernel, final_norm=final_norm), grid=(nt,),
        in_specs=[sspec(lambda i: (i, 0, 0)), sspec(lambda i: (jnp.minimum(i + 1, nt - 1), 0, 0)),
                  pl.BlockSpec(memory_space=pl.ANY),
                  pl.BlockSpec((tm, dm), lambda i: (i, 0)),
                  pl.BlockSpec((tm, LANES), lambda i: (i, 0)),
                  pl.BlockSpec((1, dm), lambda i: (0, 0))],
        out_specs=pl.BlockSpec((tm, dm), lambda i: (i, 0)),
        out_shape=jax.ShapeDtypeStruct((n, dm), F32),
        scratch_shapes=[pltpu.VMEM((2, 2 * tm * ch, LANES), F32), pltpu.SemaphoreType.DMA((2,))],
        compiler_params=_cparams(1), name="moe_combine")(srows, srows, ys_rows, x2, route, nf)


def moe_final(xn, route, x2, wg, wu, wd, nf, final_norm):
    e1 = route[:, 0].astype(jnp.int32)
    e2 = route[:, 1].astype(jnp.int32)
    slots, row_token, tile_expert, tile_live = moe_plan(e1, e2, wg.shape[0], MOE_TILE)
    ys = moe_grouped(xn, row_token, tile_expert, tile_live, wg, wu, wd)
    return moe_combine(ys, slots, x2, route, nf, min(MOE_TILE, x2.shape[0]), final_norm)


def _pad_lanes(v, width=LANES):
    return jnp.pad(v, ((0, 0), (0, width - v.shape[1])))


def _natural_rows(arr, n_rows):
    b, d, l, w = arr.shape
    tail = arr[:, :, l - n_rows // d:, :]
    return jnp.swapaxes(tail, 1, 2).reshape(b, n_rows, w)


def kernel(x_prompt, x_sample, cache_kv_w128, cache_kv_w512, cache_kv_w2048, state_ssm, state_conv,
           norm1, w_in, conv_w, conv_b, dt_bias, a_log, d_skip, ssm_norm, w_att_out, w_ssm_out, w_o,
           norm2, w_router_coarse, b_router_coarse, w_router_fine, b_router_fine,
           w_exp_gate, w_exp_up, w_exp_down, norm_f):
    bp, sp, dm = x_prompt.shape
    bs, ts, _ = x_sample.shape
    depth = w_in.shape[0]
    n_heads = dt_bias.shape[1]
    d_inner = n_heads * SSM_HEAD_DIM
    conv_dim = conv_w.shape[2]
    n_grp = len(ATT_DILATIONS)
    off_z = n_grp * 3 * ATT_WIDTH
    off_xbc = off_z + d_inner
    off_dt = off_xbc + conv_dim
    off_gate = off_dt + n_heads
    caches = (cache_kv_w128, cache_kv_w512, cache_kv_w2048)
    for g in range(n_grp):
        assert caches[g].shape[2] == ATT_WINDOWS[g] and sp % (ATT_DILATIONS[g] * ATT_BLOCK) == 0
    assert ts <= min(ATT_DILATIONS[1:]) and ts < TPAD and sp % SSM_CHUNK == 0

    tabs_p = rope_tables(jnp.arange(sp, dtype=F32))
    tabs_s = rope_tables(jnp.tile(jnp.arange(ts, dtype=F32) + PAST_LEN, bs))
    expand = (jnp.arange(LANES)[:, None] == (jnp.arange(d_inner) // SSM_HEAD_DIM)[None, :]).astype(F32)

    xp = x_prompt.reshape(bp * sp, dm)
    xs = x_sample.reshape(bs * ts, dm)
    outs = {k: [] for k in ("kvp0", "kvp1", "kvp2", "ssm_p", "conv_p", "kvs0", "kvs1", "kvs2", "ssm_s", "conv_s")}
    n_s = bs * ts
    for layer in range(depth):
        w = w_in[layer].astype(BF16)
        w_qkv = [w[:, g * 3 * ATT_WIDTH:(g + 1) * 3 * ATT_WIDTH] for g in range(n_grp)]
        w_z, w_xbc, w_gate = w[:, off_z:off_xbc], w[:, off_xbc:off_dt], w[:, off_gate:]
        w_dt = _pad_lanes(w[:, off_dt:off_gate])
        dtb = _pad_lanes(dt_bias[layer][None])
        alog = _pad_lanes(a_log[layer][None])
        dsk_e = jnp.repeat(d_skip[layer], SSM_HEAD_DIM)[None]
        nw = ssm_norm[layer][None]
        cw, cb = conv_w[layer], conv_b[layer][None]
        w_router = _pad_lanes(jnp.concatenate([w_router_coarse[layer], w_router_fine[layer]], axis=1)).astype(BF16)
        b_router = _pad_lanes(jnp.concatenate([b_router_coarse[layer], b_router_fine[layer]])[None])
        wts = (w_att_out[layer].astype(BF16), w_ssm_out[layer].astype(BF16), w_o[layer].astype(BF16),
               w_gate, norm2[layer][None], w_router, b_router)
        wg, wu, wd = (w_exp_gate[layer].astype(BF16), w_exp_up[layer].astype(BF16),
                      w_exp_down[layer].astype(BF16))

        xn = rmsnorm_bf16(xp, norm1[layer], 1024)
        att = []
        for g, d in enumerate(ATT_DILATIONS):
            qkv = qkv_proj(xn, w_qkv[g], tabs_p, bp, sp, d, 1024, BF16, f"qkv_prompt_d{d}")
            att.append(attn_prompt(qkv))
            wnd = min(ATT_WINDOWS[g], sp)
            kv = jnp.stack([_natural_rows(qkv[1], wnd), _natural_rows(qkv[2], wnd)], axis=2)
            outs[f"kvp{g}"].append(kv.astype(F32).reshape(bp, wnd, 2, ATT_HEADS, ATT_HEAD_DIM))
        z = matmul(xn, w_z, BF16, 2048, 1024, "proj_z")
        xbc = matmul(xn, w_xbc, BF16, 2048, 1024, "proj_xbc")
        dt_raw = matmul(xn, w_dt, F32, 2048, LANES, "proj_dt")
        y_ssm, h_p = ssd_prompt(xbc, z, dt_raw, cw, cb, dtb, alog, dsk_e, nw, bp, sp)
        outs["ssm_p"].append(h_p.reshape(bp, n_heads, SSM_HEAD_DIM, SSM_STATE))
        outs["conv_p"].append(xbc.reshape(bp, sp, conv_dim)[:, sp - (SSM_CONV - 1):].astype(F32))
        x2, xn2, cmb = outproj(xp, xn, y_ssm, att, ATT_DILATIONS, bp, sp, wts, 512)
        xp = moe_final(xn2, cmb, x2, wg, wu, wd, norm_f[None], layer == depth - 1)

        xn = rmsnorm_bf16(xs, norm1[layer], n_s)
        pad_t = lambda a: jnp.pad(a.reshape(bs, ts, a.shape[-1]), ((0, 0), (0, TPAD - ts), (0, 0)))
        new = jnp.stack([qkv_proj(xn, w_qkv[g], tabs_s, 1, n_s, 1, n_s, F32, f"qkv_sample_{g}")
                         .reshape(3, bs, ts, ATT_WIDTH) for g in range(n_grp)])
        new = jnp.pad(new, ((0, 0), (0, 0), (0, 0), (0, TPAD - ts), (0, 0)))
        cl = [jnp.transpose(caches[g][layer], (0, 2, 3, 4, 1)) for g in range(n_grp)]
        res = cache_attn(new, cl, ts)
        attn_s = res[0][:, :ts]
        for g in range(n_grp):
            outs[f"kvs{g}"].append(jnp.transpose(res[1 + g], (0, 4, 1, 2, 3)))
        z = matmul(xn, w_z, F32, n_s, 512, "proj_z_s")
        xbc = matmul(xn, w_xbc, F32, n_s, 512, "proj_xbc_s")
        dt_raw = matmul(xn, w_dt, F32, n_s, LANES, "proj_dt_s")
        y_s, h_s = ssd_sample(pad_t(xbc), pad_t(z), pad_t(dt_raw), state_conv[layer],
                              state_ssm[layer].reshape(bs, d_inner, SSM_STATE), cw, cb, dtb, alog, dsk_e, nw,
                              expand, ts)
        outs["ssm_s"].append(h_s.reshape(bs, n_heads, SSM_HEAD_DIM, SSM_STATE))
        hist = jnp.concatenate([state_conv[layer], xbc.reshape(bs, ts, conv_dim)], axis=1)
        outs["conv_s"].append(hist[:, ts:])
        y_s = y_s[:, :ts].reshape(n_s, d_inner)
        x2, xn2, cmb = outproj(xs, xn, y_s, attn_s.reshape(n_s, ATT_WIDTH), (), 1, n_s, wts, min(512, n_s))
        xs = moe_final(xn2, cmb, x2, wg, wu, wd, norm_f[None], layer == depth - 1)

    st = lambda k: jnp.stack(outs[k])
    return (xp.reshape(bp, sp, dm), xs.reshape(bs, ts, dm),
            st("kvp0"), st("kvp1"), st("kvp2"), st("ssm_p"), st("conv_p"),
            st("kvs0"), st("kvs1"), st("kvs2"), st("ssm_s"), st("conv_s"))
```

```python
import functools
import math

import jax
import jax.numpy as jnp
from jax import lax
from jax.experimental import pallas as pl
from jax.experimental.pallas import tpu as pltpu

F32 = jnp.float32
BF16 = jnp.bfloat16

PAST_LEN = 8192
ATT_WINDOWS = (128, 512, 2048)
ATT_DILATIONS = (1, 4, 16)
ATT_HEADS = 8
ATT_HEAD_DIM = 64
ATT_WIDTH = ATT_HEADS * ATT_HEAD_DIM
ATT_SCALE = ATT_HEAD_DIM ** -0.5
ROT_DIM = ATT_HEAD_DIM // 4
ROPE_THETA = 500000.0
ATT_BLOCK = 128

SSM_HEAD_DIM = 64
SSM_STATE = 128
SSM_GROUPS = 4
SSM_CONV = 4
SSM_CHUNK = 128
MOE_GROUPS = 4
MOE_PER_GROUP = 4
MOE_EXPERTS = MOE_GROUPS * MOE_PER_GROUP
RMS_EPS = 1e-6
SSM_NORM_EPS = 1e-5

LANES = 128
VMEM_LIMIT = 56 * 1024 * 1024
NEG_INF = float("-inf")


def _cparams(n_axes):
    return pltpu.CompilerParams(dimension_semantics=("arbitrary",) * n_axes,
                                vmem_limit_bytes=VMEM_LIMIT)


def _sigmoid(x):
    return 1.0 / (1.0 + jnp.exp(-x))


def _silu(x):
    return x * _sigmoid(x)


def _norm_kernel(x_ref, g_ref, o_ref):
    x = x_ref[...]
    y = x * lax.rsqrt(jnp.mean(x * x, axis=-1, keepdims=True) + RMS_EPS)
    o_ref[...] = (y * g_ref[...]).astype(o_ref.dtype)


def rmsnorm_bf16(x, g, tm):
    n, d = x.shape
    return pl.pallas_call(
        _norm_kernel, grid=(n // tm,),
        in_specs=[pl.BlockSpec((tm, d), lambda i: (i, 0)), pl.BlockSpec((1, d), lambda i: (0, 0))],
        out_specs=pl.BlockSpec((tm, d), lambda i: (i, 0)),
        out_shape=jax.ShapeDtypeStruct((n, d), BF16),
        compiler_params=_cparams(1), name="rmsnorm")(x, g.reshape(1, d))


def _mm_kernel(x_ref, w_ref, o_ref):
    o_ref[...] = jnp.dot(x_ref[...], w_ref[...], preferred_element_type=F32).astype(o_ref.dtype)


def matmul(x, w, out_dtype, tm, tn, name):
    m, k = x.shape
    n = w.shape[1]
    return pl.pallas_call(
        _mm_kernel, grid=(m // tm, n // tn),
        in_specs=[pl.BlockSpec((tm, k), lambda i, j: (i, 0)), pl.BlockSpec((k, tn), lambda i, j: (0, j))],
        out_specs=pl.BlockSpec((tm, tn), lambda i, j: (i, j)),
        out_shape=jax.ShapeDtypeStruct((m, n), out_dtype),
        compiler_params=_cparams(2), name=name)(x, w)


QKV_CHUNK = 256


def _qkv_kernel(x_ref, w_ref, cos_ref, sa_ref, sb_ref, o_ref, acc_ref, *, d):
    sc = jnp.where(pl.program_id(1) == 0, ATT_SCALE, 1.0).astype(F32)
    tm = x_ref.shape[0]
    ck = min(QKV_CHUNK, tm)
    for c0 in range(0, tm, ck):
        rows = slice(c0, c0 + ck)
        acc = jnp.dot(x_ref[rows, :], w_ref[...], preferred_element_type=F32)
        c = cos_ref[rows, :] * sc
        sa = sa_ref[rows, :] * sc
        sb = sb_ref[rows, :] * sc
        for ch in range(ATT_WIDTH // LANES):
            cs = slice(ch * LANES, (ch + 1) * LANES)
            t = acc[:, cs]
            r = t * c + pltpu.roll(t, LANES - ROT_DIM // 2, 1) * sa + pltpu.roll(t, ROT_DIM // 2, 1) * sb
            if d == 1:
                o_ref[0, rows, cs] = r.astype(o_ref.dtype)
            else:
                acc_ref[ch, rows, :] = r
                for res in range(d):
                    o_ref[res, c0 // d:(c0 + ck) // d, cs] = acc_ref[
                        ch, pl.ds(c0 + res, ck // d, stride=d), :].astype(o_ref.dtype)


def qkv_proj(xn, w, tabs, batch, seq, d, tm, out_dtype, name):
    n, dm = xn.shape
    tpb = seq // tm
    cos, sa, sb = tabs
    tab_spec = pl.BlockSpec((None, tm, LANES), lambda i, j: (jnp.where(j == 2, 1, 0), i % tpb, 0))
    return pl.pallas_call(
        functools.partial(_qkv_kernel, d=d), grid=(n // tm, 3),
        in_specs=[pl.BlockSpec((tm, dm), lambda i, j: (i, 0)),
                  pl.BlockSpec((dm, ATT_WIDTH), lambda i, j: (0, j)),
                  tab_spec, tab_spec, tab_spec],
        out_specs=pl.BlockSpec((None, None, d, tm // d, ATT_WIDTH),
                               lambda i, j: (j, i // tpb, 0, i % tpb, 0)),
        out_shape=jax.ShapeDtypeStruct((3, batch, d, seq // d, ATT_WIDTH), out_dtype),
        scratch_shapes=[pltpu.VMEM((ATT_WIDTH // LANES, tm, LANES), F32)],
        compiler_params=_cparams(2), name=name)(xn, w, cos, sa, sb)


def rope_tables(pos):
    half = ROT_DIM // 2
    inv = ROPE_THETA ** (-jnp.arange(half, dtype=F32) / half)
    ang = pos[:, None] * inv[None, :]
    cos = jnp.cos(ang)
    sin = jnp.sin(ang)
    s = pos.shape[0]
    ones = jnp.ones((s, ATT_HEAD_DIM - ROT_DIM), F32)
    zeros = jnp.zeros((s, ATT_HEAD_DIM - ROT_DIM), F32)
    zh = jnp.zeros((s, half), F32)
    c = jnp.concatenate([cos, cos, ones], axis=1)
    sa = jnp.concatenate([-sin, zh, zeros], axis=1)
    sb = jnp.concatenate([zh, sin, zeros], axis=1)
    rep = LANES // ATT_HEAD_DIM
    c, sa, sb = (jnp.tile(t, (1, rep)) for t in (c, sa, sb))
    return (jnp.stack([c, jnp.ones_like(c)]), jnp.stack([sa, jnp.zeros_like(sa)]),
            jnp.stack([sb, jnp.zeros_like(sb)]))


ATT_ROW_CHUNK = 32
ATT_SUB = 2


def _attn_kernel(q_ref, kc_ref, vc_ref, kp_ref, vp_ref, num_ref, st_ref, s_scr, p_scr):
    nb = pl.program_id(2)
    blk = ATT_BLOCK
    nt = (((1,), (1,)), ((), ()))
    heads = [slice(h * ATT_HEAD_DIM, (h + 1) * ATT_HEAD_DIM) for h in range(ATT_HEADS)]
    subs = [slice(sb * blk, (sb + 1) * blk) for sb in range(ATT_SUB)]

    def prev(ref_p, ref_c, sb, sl):
        return ref_p[:, sl] if sb == 0 else ref_c[subs[sb - 1], sl]

    for sb, rows in enumerate(subs):
        for h, sl in enumerate(heads):
            q = q_ref[rows, sl]
            s_scr[sb, h, :, 0:blk] = lax.dot_general(q, prev(kp_ref, kc_ref, sb, sl), nt,
                                                     preferred_element_type=F32)
            s_scr[sb, h, :, blk:2 * blk] = lax.dot_general(q, kc_ref[rows, sl], nt, preferred_element_type=F32)
    rc = ATT_ROW_CHUNK
    qi = lax.broadcasted_iota(jnp.int32, (rc, 2 * blk), 0)
    kj = lax.broadcasted_iota(jnp.int32, (rc, 2 * blk), 1)
    lane = lax.broadcasted_iota(jnp.int32, (rc, LANES), 1)
    for sb, rows in enumerate(subs):
        for r0 in range(0, blk, rc):
            qa = qi + r0
            in_prev = jnp.logical_and(kj < blk, kj >= qa)
            if sb == 0:
                in_prev = jnp.logical_and(in_prev, nb > 0)
            mask = jnp.logical_or(in_prev, jnp.logical_and(kj >= blk, kj - blk <= qa))
            st = jnp.zeros((rc, LANES), F32)
            for h in range(ATT_HEADS):
                s = jnp.where(mask, s_scr[sb, h, r0:r0 + rc, :], NEG_INF)
                m = jnp.max(s, axis=-1, keepdims=True)
                p = jnp.exp(s - m)
                l = jnp.sum(p, axis=-1, keepdims=True)
                p_scr[sb, h, r0:r0 + rc, :] = p.astype(BF16)
                st = jnp.where(lane == h, m, st)
                st = jnp.where(lane == ATT_HEADS + h, l, st)
            st_ref[sb * blk + r0:sb * blk + r0 + rc, :] = st
    for sb, rows in enumerate(subs):
        for h, sl in enumerate(heads):
            num_ref[rows, sl] = (
                jnp.dot(p_scr[sb, h, :, 0:blk], prev(vp_ref, vc_ref, sb, sl), preferred_element_type=F32)
                + jnp.dot(p_scr[sb, h, :, blk:2 * blk], vc_ref[rows, sl], preferred_element_type=F32))


def attn_prompt(qkv):
    _, b, d, l, w = qkv.shape
    step = ATT_SUB * ATT_BLOCK
    nstep = l // step

    def spec(kind, prev):
        if prev:
            return pl.BlockSpec((None, None, None, ATT_BLOCK, w),
                                lambda bi, r, n: (kind, bi, r, jnp.maximum(ATT_SUB * n - 1, 0), 0))
        return pl.BlockSpec((None, None, None, step, w), lambda bi, r, n: (kind, bi, r, n, 0))

    return pl.pallas_call(
        _attn_kernel, grid=(b, d, nstep),
        in_specs=[spec(0, False), spec(1, False), spec(2, False), spec(1, True), spec(2, True)],
        out_specs=[pl.BlockSpec((None, None, step, w), lambda bi, r, n: (bi, r, n, 0)),
                   pl.BlockSpec((None, None, step, LANES), lambda bi, r, n: (bi, r, n, 0))],
        out_shape=[jax.ShapeDtypeStruct((b, d, l, w), F32), jax.ShapeDtypeStruct((b, d, l, LANES), F32)],
        scratch_shapes=[pltpu.VMEM((ATT_SUB, ATT_HEADS, ATT_BLOCK, 2 * ATT_BLOCK), F32),
                        pltpu.VMEM((ATT_SUB, ATT_HEADS, ATT_BLOCK, 2 * ATT_BLOCK), BF16)],
        compiler_params=_cparams(3), name=f"attn_prompt_d{d}")(qkv, qkv, qkv, qkv, qkv)


def _col(mat, h, n_lanes=LANES):
    return jnp.broadcast_to(mat[:, h:h + 1], (mat.shape[0], n_lanes))


def _ssd_kernel(xbc_ref, z_ref, dt_ref, cw_ref, cb_ref, dtb_ref, alog_ref, dsk_ref, nw_ref,
                y_ref, hout_ref,
                cbuf, xs_s, xw_s, eac_s, cd_s, y_s, ht_s, *, d_inner, n_groups):
    c = pl.program_id(1)
    nc = pl.num_programs(1)
    q = SSM_CHUNK
    n_st = SSM_STATE
    gw = d_inner // n_groups
    conv_dim = xbc_ref.shape[1]
    top = 8

    @pl.when(c == 0)
    def _():
        cbuf[0:top, :] = jnp.zeros((top, conv_dim), F32)
        ht_s[...] = jnp.zeros(ht_s.shape, F32)

    cbuf[top:top + q, :] = xbc_ref[...].astype(F32)
    cwid = 512
    for j in range(conv_dim // cwid):
        cs = slice(j * cwid, (j + 1) * cwid)
        acc = cb_ref[:, cs] + cbuf[top - 3:top - 3 + q, cs] * cw_ref[0:1, cs]
        for i in range(1, SSM_CONV):
            acc = acc + cbuf[top - 3 + i:top - 3 + i + q, cs] * cw_ref[i:i + 1, cs]
        xs_s[:, cs] = _silu(acc)
    cbuf[top - 3:top, :] = cbuf[top + q - 3:top + q, :]

    dt = jax.nn.softplus(dt_ref[...] + dtb_ref[...])
    a_row = -jnp.exp(alog_ref[...])
    a = dt * a_row
    ri = lax.broadcasted_iota(jnp.int32, (q, q), 0)
    ci = lax.broadcasted_iota(jnp.int32, (q, q), 1)
    causal = ri >= ci
    tril = jnp.where(causal, 1.0, 0.0).astype(F32)
    acum = jnp.dot(tril, a, preferred_element_type=F32, precision=lax.Precision.HIGHEST)
    acum_t = acum.T
    lane = lax.broadcasted_iota(jnp.int32, (q, LANES), 1)
    lo = lane < SSM_HEAD_DIM

    hpg = gw // SSM_HEAD_DIM
    for g in range(n_groups):
        bc = xs_s[:, d_inner + g * n_st:d_inner + (g + 1) * n_st].astype(BF16)
        cc = xs_s[:, d_inner + n_groups * n_st + g * n_st:d_inner + n_groups * n_st + (g + 1) * n_st].astype(BF16)
        cb = lax.dot_general(cc, bc, (((1,), (1,)), ((), ())), preferred_element_type=F32)
        for jp in range(hpg // 2):
            h0 = g * hpg + 2 * jp
            ls = slice(g * gw + jp * LANES, g * gw + (jp + 1) * LANES)
            ac0 = _col(acum, h0)
            ac1 = _col(acum, h0 + 1)
            acum_e = jnp.where(lo, ac0, ac1)
            dt_e = jnp.where(lo, _col(dt, h0), _col(dt, h0 + 1))
            xdt = xs_s[:, ls] * dt_e
            acl_e = acum_e[q - 1:q, :]
            xw_s[:, ls] = (xdt * jnp.exp(acl_e - acum_e)).astype(BF16)
            eac_s[:, ls] = jnp.exp(acum_e)
            cd_s[:, ls] = jnp.exp(acl_e)
            xdt_b = xdt.astype(BF16)
            zero = jnp.zeros_like(xdt_b)
            m0 = (jnp.exp(jnp.where(causal, ac0 - acum_t[h0:h0 + 1, :], NEG_INF)) * cb).astype(BF16)
            m1 = (jnp.exp(jnp.where(causal, ac1 - acum_t[h0 + 1:h0 + 2, :], NEG_INF)) * cb).astype(BF16)
            y_s[:, ls] = (jnp.dot(m0, jnp.where(lo, xdt_b, zero), preferred_element_type=F32)
                          + jnp.dot(m1, jnp.where(lo, zero, xdt_b), preferred_element_type=F32))
        gs = slice(g * gw, (g + 1) * gw)
        h_prev = ht_s[g]
        y_off = jnp.dot(cc, h_prev.astype(BF16), preferred_element_type=F32) * eac_s[:, gs]
        y_s[:, gs] = y_s[:, gs] + y_off
        st = lax.dot_general(bc, xw_s[:, gs], (((0,), (0,)), ((), ())), preferred_element_type=F32)
        ht_s[g] = cd_s[:, gs] * h_prev + st

    for g in range(n_groups):
        gs = slice(g * gw, (g + 1) * gw)
        y = y_s[:, gs] + dsk_ref[:, gs] * xs_s[:, gs]
        yf = y * _silu(z_ref[:, gs].astype(F32))
        yf = yf * lax.rsqrt(jnp.mean(yf * yf, axis=-1, keepdims=True) + SSM_NORM_EPS)
        y_ref[:, gs] = (yf * nw_ref[:, gs]).astype(y_ref.dtype)

    @pl.when(c == nc - 1)
    def _():
        for g in range(n_groups):
            hout_ref[g * gw:(g + 1) * gw, :] = ht_s[g].T


def ssd_prompt(xbc, z, dt_raw, conv_w, conv_b, dtb, alog, dsk_e, nw, batch, seq):
    n, conv_dim = xbc.shape
    d_inner = z.shape[1]
    n_groups = SSM_GROUPS
    gw = d_inner // n_groups
    q = SSM_CHUNK
    cps = seq // q
    row = lambda w: pl.BlockSpec((1, w), lambda b, c: (0, 0))
    tok = lambda w: pl.BlockSpec((q, w), lambda b, c: (b * cps + c, 0))
    return pl.pallas_call(
        functools.partial(_ssd_kernel, d_inner=d_inner, n_groups=n_groups),
        grid=(batch, cps),
        in_specs=[tok(conv_dim), tok(d_inner), tok(LANES),
                  pl.BlockSpec((SSM_CONV, conv_dim), lambda b, c: (0, 0)), row(conv_dim),
                  row(LANES), row(LANES), row(d_inner), row(d_inner)],
        out_specs=[tok(d_inner), pl.BlockSpec((None, d_inner, SSM_STATE), lambda b, c: (b, 0, 0))],
        out_shape=[jax.ShapeDtypeStruct((n, d_inner), BF16),
                   jax.ShapeDtypeStruct((batch, d_inner, SSM_STATE), F32)],
        scratch_shapes=[pltpu.VMEM((8 + q, conv_dim), F32),
                        pltpu.VMEM((q, conv_dim), F32),
                        pltpu.VMEM((q, d_inner), BF16),
                        pltpu.VMEM((q, d_inner), F32),
                        pltpu.VMEM((1, d_inner), F32),
                        pltpu.VMEM((q, d_inner), F32),
                        pltpu.VMEM((n_groups, SSM_STATE, gw), F32)],
        compiler_params=_cparams(2), name="ssd_prompt")(xbc, z, dt_raw, conv_w, conv_b, dtb, alog, dsk_e, nw)


TPAD = 8


def _ssd_step_kernel(xbc_ref, z_ref, dt_ref, cst_ref, h0_ref, cw_ref, cb_ref, dtb_ref, alog_ref,
                     dsk_ref, nw_ref, exp_ref, y_ref, hout_ref,
                     cbuf, xs_s, f_s, *, d_inner, n_groups, n_tok):
    b = pl.program_id(0)
    n_st = SSM_STATE
    gw = d_inner // n_groups
    conv_dim = xbc_ref.shape[1]
    top = 8
    hist = SSM_CONV - 1

    @pl.when(b == 0)
    def _():
        cbuf[0:top, :] = jnp.zeros((top, conv_dim), F32)

    cbuf[top - hist:top, :] = cst_ref[...]
    cbuf[top:top + TPAD, :] = xbc_ref[...]
    acc = cb_ref[...] + cbuf[top - hist:top - hist + TPAD, :] * cw_ref[0:1, :]
    for i in range(1, SSM_CONV):
        acc = acc + cbuf[top - hist + i:top - hist + i + TPAD, :] * cw_ref[i:i + 1, :]
    xs_s[...] = _silu(acc)

    rid = lax.broadcasted_iota(jnp.int32, (TPAD, LANES), 0)
    dt = jax.nn.softplus(dt_ref[...] + dtb_ref[...])
    da = dt * (-jnp.exp(alog_ref[...]))
    cum = da
    for k in range(1, n_tok):
        cum = cum + jnp.where(rid >= k, pltpu.roll(da, k, 0), 0.0)
    facs = [dt, jnp.exp(cum)]
    for t in range(n_tok):
        facs.append(jnp.where(rid <= t, jnp.exp(cum[t:t + 1, :] - cum), 0.0))
    hi = lax.Precision.HIGHEST
    f_s[...] = jnp.dot(jnp.concatenate(facs, axis=0), exp_ref[...], preferred_element_type=F32, precision=hi)

    nt = (((1,), (1,)), ((), ()))
    row8 = lax.broadcasted_iota(jnp.int32, (TPAD, gw), 0)
    for g in range(n_groups):
        gs = slice(g * gw, (g + 1) * gw)
        xdt = xs_s[:, gs] * f_s[0:TPAD, gs]
        b_f = xs_s[:, d_inner + g * n_st:d_inner + (g + 1) * n_st]
        c_f = xs_s[:, d_inner + (n_groups + g) * n_st:d_inner + (n_groups + g + 1) * n_st]
        b16 = b_f.astype(BF16)
        c16 = c_f.astype(BF16)
        h0g = h0_ref[gs, :]
        y = f_s[TPAD:2 * TPAD, gs] * lax.dot_general(c16, h0g.astype(BF16), nt, preferred_element_type=F32)
        bc = lax.dot_general(b16, c16, nt, preferred_element_type=F32)
        for t in range(n_tok):
            term = bc[:, t:t + 1] * f_s[(2 + t) * TPAD:(3 + t) * TPAD, gs] * xdt
            y = y + jnp.where(row8 == t, jnp.sum(term, axis=0, keepdims=True), 0.0)
        yv = y + dsk_ref[:, gs] * xs_s[:, gs]
        yf = yv * _silu(z_ref[:, gs])
        yf = yf * lax.rsqrt(jnp.mean(yf * yf, axis=-1, keepdims=True) + SSM_NORM_EPS)
        y_ref[:, gs] = yf * nw_ref[:, gs]

        dend = f_s[(1 + n_tok) * TPAD:(2 + n_tok) * TPAD, gs]
        pend = f_s[TPAD + n_tok - 1:TPAD + n_tok, gs]
        lhs = jnp.where(row8 == n_tok, pend, dend * xdt)
        rhs = jnp.concatenate([jnp.where(rid < n_tok, b_f, 0.0), jnp.where(rid == n_tok, 1.0, 0.0)], axis=1)
        res = lax.dot_general(lhs, rhs, (((0,), (0,)), ((), ())), preferred_element_type=F32, precision=hi)
        hout_ref[gs, :] = res[:, n_st:] * h0g + res[:, :n_st]


def ssd_sample(xbc, z, dt_raw, conv_state, h0, conv_w, conv_b, dtb, alog, dsk_e, nw, expand, n_tok):
    bsz, _, conv_dim = xbc.shape
    d_inner = z.shape[2]
    row = lambda w: pl.BlockSpec((1, w), lambda b: (0, 0))
    tok = lambda w: pl.BlockSpec((None, TPAD, w), lambda b: (b, 0, 0))
    st = pl.BlockSpec((None, d_inner, SSM_STATE), lambda b: (b, 0, 0))
    return pl.pallas_call(
        functools.partial(_ssd_step_kernel, d_inner=d_inner, n_groups=SSM_GROUPS, n_tok=n_tok),
        grid=(bsz,),
        in_specs=[tok(conv_dim), tok(d_inner), tok(LANES),
                  pl.BlockSpec((None, SSM_CONV - 1, conv_dim), lambda b: (b, 0, 0)), st,
                  pl.BlockSpec((SSM_CONV, conv_dim), lambda b: (0, 0)), row(conv_dim),
                  row(LANES), row(LANES), row(d_inner), row(d_inner),
                  pl.BlockSpec((LANES, d_inner), lambda b: (0, 0))],
        out_specs=[tok(d_inner), st],
        out_shape=[jax.ShapeDtypeStruct((bsz, TPAD, d_inner), F32),
                   jax.ShapeDtypeStruct((bsz, d_inner, SSM_STATE), F32)],
        scratch_shapes=[pltpu.VMEM((8 + TPAD, conv_dim), F32),
                        pltpu.VMEM((TPAD, conv_dim), F32),
                        pltpu.VMEM(((2 + n_tok) * TPAD, d_inner), F32)],
        compiler_params=_cparams(1), name="ssd_sample")(
            xbc, z, dt_raw, conv_state, h0, conv_w, conv_b, dtb, alog, dsk_e, nw, expand)


HEADS_PER_STEP = 4


def _cache_attn_kernel(new_ref, c0_ref, c1_ref, c2_ref, o_ref, n0_ref, n1_ref, n2_ref, stage, tts, *, n_tok):
    first = jnp.logical_and(pl.program_id(0) == 0, pl.program_id(1) == 0)

    @pl.when(first)
    def _():
        stage[...] = jnp.zeros(stage.shape, F32)

    crefs = (c0_ref, c1_ref, c2_ref)
    orefs = (n0_ref, n1_ref, n2_ref)
    hd = ATT_HEAD_DIM
    nt = (((1,), (1,)), ((), ()))
    lane = lax.broadcasted_iota(jnp.int32, (hd, LANES), 1)
    t_new = lax.broadcasted_iota(jnp.int32, (TPAD, TPAD), 0)
    u_new = lax.broadcasted_iota(jnp.int32, (TPAD, TPAD), 1)

    for head in range(HEADS_PER_STEP):
        hs = slice(head * hd, (head + 1) * hd)
        parts = []
        for g, d in enumerate(ATT_DILATIONS):
            w = crefs[g].shape[-1]
            q = new_ref[g, 0, :, hs].astype(BF16)
            kn = new_ref[g, 1, :, hs].astype(BF16)
            vn = new_ref[g, 2, :, hs].astype(BF16)
            t_id = lax.broadcasted_iota(jnp.int32, (TPAD, w), 0)
            r_id = lax.broadcasted_iota(jnp.int32, (TPAD, w), 1)
            s = jnp.dot(q, crefs[g][0, head].astype(BF16), preferred_element_type=F32)
            s = jnp.where((r_id >= t_id) if d == 1 else ((r_id & (d - 1)) == t_id), s, NEG_INF)
            sn = lax.dot_general(q, kn, nt, preferred_element_type=F32)
            sn = jnp.where((u_new <= t_new) if d == 1 else (u_new == t_new), sn, NEG_INF)
            m = jnp.maximum(jnp.max(s, axis=1, keepdims=True), jnp.max(sn, axis=1, keepdims=True))
            p = jnp.exp(s - m)
            pn = jnp.exp(sn - m)
            l = jnp.sum(p, axis=1, keepdims=True) + jnp.sum(pn, axis=1, keepdims=True)
            num = (lax.dot_general(p.astype(BF16), crefs[g][1, head].astype(BF16), nt,
                                   preferred_element_type=F32)
                   + jnp.dot(pn.astype(BF16), vn, preferred_element_type=F32))
            parts.append((num, m, l))
        mx = jnp.maximum(jnp.maximum(parts[0][1], parts[1][1]), parts[2][1])
        wg = [jnp.exp(p_[1] - mx) for p_ in parts]
        num = wg[0] * parts[0][0]
        for g in range(1, 3):
            num = num + wg[g] * parts[g][0]
        den = wg[0] * parts[0][2] + wg[1] * parts[1][2] + wg[2] * parts[2][2]
        o_ref[:, hs] = num / den

    for pair in range(HEADS_PER_STEP // 2):
        ps = slice(pair * LANES, (pair + 1) * LANES)
        for g in range(3):
            for kv in range(2):
                stage[0:TPAD, :] = new_ref[g, 1 + kv, :, ps]
                tts[2 * g + kv] = stage[...].T
        for hh in range(2):
            head = pair * 2 + hh
            hs = slice(hh * hd, (hh + 1) * hd)
            for g in range(3):
                n_col = crefs[g].shape[-1] // LANES
                for kv in range(2):
                    cur = pltpu.roll(crefs[g][kv, head, :, 0:LANES], LANES - n_tok, 1)
                    for j in range(n_col):
                        if j + 1 < n_col:
                            nxt = pltpu.roll(crefs[g][kv, head, :, (j + 1) * LANES:(j + 2) * LANES],
                                             LANES - n_tok, 1)
                        else:
                            nxt = pltpu.roll(tts[2 * g + kv, hs, :], LANES - n_tok, 1)
                        orefs[g][kv, head, :, j * LANES:(j + 1) * LANES] = jnp.where(
                            lane < LANES - n_tok, cur, nxt)
                        cur = nxt


def cache_attn(new, caches, n_tok):
    bsz = new.shape[2]
    hps = HEADS_PER_STEP
    cspec = lambda c: pl.BlockSpec((None, 2, hps, ATT_HEAD_DIM, c.shape[-1]), lambda b, h: (b, 0, h, 0, 0))
    return pl.pallas_call(
        functools.partial(_cache_attn_kernel, n_tok=n_tok), grid=(bsz, ATT_HEADS // hps),
        in_specs=[pl.BlockSpec((3, 3, None, TPAD, hps * ATT_HEAD_DIM), lambda b, h: (0, 0, b, 0, h))]
        + [cspec(c) for c in caches],
        out_specs=[pl.BlockSpec((None, TPAD, hps * ATT_HEAD_DIM), lambda b, h: (b, 0, h))]
        + [cspec(c) for c in caches],
        out_shape=[jax.ShapeDtypeStruct((bsz, TPAD, ATT_WIDTH), F32)]
        + [jax.ShapeDtypeStruct(c.shape, c.dtype) for c in caches],
        scratch_shapes=[pltpu.VMEM((LANES, LANES), F32), pltpu.VMEM((6, LANES, LANES), F32)],
        compiler_params=_cparams(2), name="cache_attn")(new, *caches)


def _router(logits):
    lanef = lax.broadcasted_iota(jnp.int32, logits.shape, 1).astype(F32)
    big = 1e9
    lc = jnp.where(lanef < MOE_GROUPS, logits, NEG_INF)
    mc = jnp.max(lc, axis=-1, keepdims=True)
    g_sel = jnp.min(jnp.where(lc == mc, lanef, big), axis=-1, keepdims=True)
    p_sel = 1.0 / jnp.sum(jnp.exp(lc - mc), axis=-1, keepdims=True)
    base = MOE_GROUPS + MOE_PER_GROUP * g_sel
    lf = jnp.where(jnp.logical_and(lanef >= base, lanef < base + MOE_PER_GROUP), logits, NEG_INF)
    v1 = jnp.max(lf, axis=-1, keepdims=True)
    i1 = jnp.min(jnp.where(lf == v1, lanef, big), axis=-1, keepdims=True)
    lf2 = jnp.where(lanef == i1, NEG_INF, lf)
    v2 = jnp.max(lf2, axis=-1, keepdims=True)
    i2 = jnp.min(jnp.where(lf2 == v2, lanef, big), axis=-1, keepdims=True)
    e2 = jnp.exp(v2 - v1)
    den = 1.0 + e2
    w1 = (1.0 / den) * p_sel
    w2 = (e2 / den) * p_sel
    out = jnp.where(lanef == 0.0, i1 - MOE_GROUPS, 0.0)
    out = jnp.where(lanef == 1.0, i2 - MOE_GROUPS, out)
    out = jnp.where(lanef == 2.0, w1, out)
    return jnp.where(lanef == 3.0, w2, out)


OUTPROJ_CHUNK = 256


def _outproj_kernel(*refs, dils, tm):
    n_g = len(dils)
    if n_g:
        x_ref, xn1_ref, ys_ref = refs[:3]
        att_refs = refs[3:3 + 2 * n_g]
        rest = refs[3 + 2 * n_g:]
    else:
        x_ref, xn1_ref, ys_ref, attn_ref = refs[:4]
        rest = refs[4:]
    wa_ref, ws_ref, wo_ref, wg_ref, n2_ref, wr_ref, br_ref, x2_ref, xn_ref, cmb_ref = rest[:10]
    scr = rest[10:]

    dm = x_ref.shape[1]
    n_ch = ATT_WIDTH // LANES
    rc = min(OUTPROJ_CHUNK, tm)
    lane = lax.broadcasted_iota(jnp.int32, (rc, LANES), 1)
    lo = lane < ATT_HEAD_DIM
    for c0 in range(0, tm, rc):
        rows = slice(c0, c0 + rc)
        if n_g:
            nums, stats = [], []
            for g, d in enumerate(dils):
                num_ref, st_ref = att_refs[2 * g], att_refs[2 * g + 1]
                if d == 1:
                    nums.append([num_ref[0, rows, ch * LANES:(ch + 1) * LANES] for ch in range(n_ch)])
                    stats.append(st_ref[0, rows, :])
                else:
                    ns, ss = scr[2 * g], scr[2 * g + 1]
                    rr = slice(c0 // d, (c0 + rc) // d)
                    for r in range(d):
                        for ch in range(n_ch):
                            ns[ch, pl.ds(c0 + r, rc // d, stride=d), :] = num_ref[r, rr, ch * LANES:(ch + 1) * LANES]
                        ss[pl.ds(c0 + r, rc // d, stride=d), :] = st_ref[r, rr, :]
                    nums.append([ns[ch, rows, :] for ch in range(n_ch)])
                    stats.append(ss[rows, :])
            chunks = []
            for ch in range(n_ch):
                wts, dens = [], []
                for h in (2 * ch, 2 * ch + 1):
                    ms = [s[:, h:h + 1] for s in stats]
                    ls = [s[:, ATT_HEADS + h:ATT_HEADS + h + 1] for s in stats]
                    mx = ms[0]
                    for m in ms[1:]:
                        mx = jnp.maximum(mx, m)
                    w = [jnp.exp(m - mx) for m in ms]
                    den = w[0] * ls[0]
                    for g in range(1, n_g):
                        den = den + w[g] * ls[g]
                    wts.append(w)
                    dens.append(den)
                num = jnp.where(lo, wts[0][0], wts[1][0]) * nums[0][ch]
                for g in range(1, n_g):
                    num = num + jnp.where(lo, wts[0][g], wts[1][g]) * nums[g][ch]
                chunks.append((num / jnp.where(lo, dens[0], dens[1])).astype(BF16))
            attn = jnp.concatenate(chunks, axis=1)
        else:
            attn = attn_ref[rows, :].astype(BF16)

        a = jnp.dot(attn, wa_ref[...], preferred_element_type=F32)
        s = jnp.dot(ys_ref[rows, :].astype(BF16), ws_ref[...], preferred_element_type=F32)
        gates = jnp.dot(xn1_ref[rows, :], wg_ref[...], preferred_element_type=F32)
        mixed = _sigmoid(gates[:, :dm]) * a + _sigmoid(gates[:, dm:]) * s
        x2 = x_ref[rows, :] + jnp.dot(mixed.astype(BF16), wo_ref[...], preferred_element_type=F32)
        x2_ref[rows, :] = x2
        xn = x2 * lax.rsqrt(jnp.mean(x2 * x2, axis=-1, keepdims=True) + RMS_EPS) * n2_ref[...]
        _to_rows(xn_ref.at[pl.ds(c0 * (dm // LANES), rc * (dm // LANES)), :], xn)
        logits = jnp.dot(xn.astype(BF16), wr_ref[...], preferred_element_type=F32) + br_ref[...]
        cmb_ref[rows, :] = _router(logits)


def outproj(x, xn1, yssm, att, dils, batch, seq, weights, tm):
    n, dm = x.shape
    wa, ws, wo, wgate, n2, wr, br = weights
    tpb = seq // tm
    tok = lambda w: pl.BlockSpec((tm, w), lambda i: (i, 0))
    full = lambda a: pl.BlockSpec(a.shape, lambda i: (0,) * a.ndim)
    in_specs = [tok(dm), tok(dm), tok(yssm.shape[1])]
    args = [x, xn1, yssm]
    scratch = []
    if dils:
        for (num, st), d in zip(att, dils):
            for arr in (num, st):
                in_specs.append(pl.BlockSpec((None, d, tm // d, arr.shape[-1]),
                                             lambda i: (i // tpb, 0, i % tpb, 0)))
                args.append(arr)
                scratch.append(pltpu.VMEM((tm, LANES) if arr.shape[-1] == LANES
                                          else (arr.shape[-1] // LANES, tm, LANES), F32))
    else:
        in_specs.append(tok(att.shape[1]))
        args.append(att)
    in_specs += [full(wa), full(ws), full(wo), full(wgate), full(n2), full(wr), full(br)]
    args += [wa, ws, wo, wgate, n2, wr, br]
    return pl.pallas_call(
        functools.partial(_outproj_kernel, dils=tuple(dils), tm=tm), grid=(n // tm,),
        in_specs=in_specs,
        out_specs=[tok(dm), pl.BlockSpec((tm * (dm // LANES), LANES), lambda i: (i, 0)), tok(LANES)],
        out_shape=[jax.ShapeDtypeStruct((n, dm), F32), jax.ShapeDtypeStruct((n * (dm // LANES), LANES), F32),
                   jax.ShapeDtypeStruct((n, LANES), F32)],
        scratch_shapes=scratch,
        compiler_params=_cparams(1), name="outproj")(*args)


MOE_TILE = 256


def moe_plan(e1, e2, n_exp, tile):
    n = e1.shape[0]
    flat = jnp.stack([e1, e2], axis=1).reshape(-1)
    onehot = (flat[:, None] == jnp.arange(n_exp, dtype=jnp.int32)[None, :]).astype(jnp.int32)
    blk = min(tile, 2 * n)
    oh3 = onehot.reshape(-1, blk, n_exp)
    lower = (jnp.arange(blk)[:, None] > jnp.arange(blk)[None, :]).astype(F32)
    within = jnp.einsum("ij,tjk->tik", lower, oh3.astype(F32)).astype(jnp.int32)
    sums = jnp.sum(oh3, axis=1)
    before = jnp.cumsum(sums, axis=0) - sums
    rank = jnp.sum((within + before[:, None, :]) * oh3, axis=2).reshape(-1)
    counts = jnp.sum(onehot, axis=0)
    padded = (counts + tile - 1) // tile * tile
    ends = jnp.cumsum(padded)
    slot = (ends - padded)[flat] + rank
    n_tiles = (2 * n) // tile + n_exp
    tile_start = jnp.arange(n_tiles, dtype=jnp.int32) * tile
    tile_expert = jnp.minimum(jnp.searchsorted(ends, tile_start, side="right"), n_exp - 1).astype(jnp.int32)
    _, order = lax.sort((flat, jnp.arange(2 * n, dtype=jnp.int32)), num_keys=1, is_stable=True)
    s_id = jnp.arange(n_tiles * tile, dtype=jnp.int32)
    e_s = jnp.repeat(tile_expert, tile)
    k = s_id - (ends - padded)[e_s]
    src = jnp.clip((jnp.cumsum(counts) - counts)[e_s] + k, 0, 2 * n - 1)
    row_token = jnp.where(k < counts[e_s], order[src] // 2, s_id % n)
    return slot.reshape(n, 2), row_token.reshape(n_tiles, 1, tile), tile_expert


def _to_rows(ref, val):
    n, ch = val.shape[0], val.shape[1] // LANES
    for c in range(ch):
        ref[pl.ds(c, n, stride=ch), :] = val[:, c * LANES:(c + 1) * LANES]


def _from_rows(ref, start, n, ch):
    return [ref[pl.ds(start * ch + c, n, stride=ch), :] for c in range(ch)]


def _issue_rows(idx_ref, src_hbm, dst, sem, ch, unroll):
    n_idx = dst.shape[0] // ch

    def row_copy(j):
        src = src_hbm.at[pl.ds(pl.multiple_of(idx_ref[0, j] * ch, ch), ch), :]
        return pltpu.make_async_copy(src, dst.at[pl.ds(j * ch, ch), :], sem)

    if unroll:
        for j in range(n_idx):
            row_copy(j).start()
    else:
        def body(j, c):
            row_copy(j).start()
            return c
        lax.fori_loop(0, n_idx, body, 0)


def _wait_rows(src_hbm, dst, sem):
    pltpu.make_async_copy(src_hbm.at[pl.ds(0, dst.shape[0]), :], dst, sem).wait()


def _moe_group_kernel(te_ref, tok_a, tok_b, tok_an, x_hbm, wg_a, wu_a, wd_a, wg_b, wu_b, wd_b, o_ref,
                      buf_a, buf_b, sem):
    del te_ref
    i = pl.program_id(0)
    ch = wg_a.shape[0] // LANES
    tile = buf_a.shape[0] // ch

    def expert(buf, wg, wu, wd, out):
        x = jnp.concatenate(_from_rows(buf, 0, tile, ch), axis=1).astype(BF16)
        he = _silu(jnp.dot(x, wg[...], preferred_element_type=F32)) * jnp.dot(x, wu[...], preferred_element_type=F32)
        _to_rows(out, jnp.dot(he.astype(BF16), wd[...], preferred_element_type=F32))

    @pl.when(i == 0)
    def _():
        _issue_rows(tok_a, x_hbm, buf_a, sem.at[0], ch, unroll=False)

    _wait_rows(x_hbm, buf_a, sem.at[0])
    _issue_rows(tok_b, x_hbm, buf_b, sem.at[1], ch, unroll=True)
    expert(buf_a, wg_a, wu_a, wd_a, o_ref.at[pl.ds(0, tile * ch), :])
    _wait_rows(x_hbm, buf_b, sem.at[1])
    _issue_rows(tok_an, x_hbm, buf_a, sem.at[0], ch, unroll=True)
    expert(buf_b, wg_b, wu_b, wd_b, o_ref.at[pl.ds(tile * ch, tile * ch), :])

    @pl.when(i == pl.num_programs(0) - 1)
    def _():
        _wait_rows(x_hbm, buf_a, sem.at[0])


def moe_grouped(x_rows, row_token, tile_expert, wg, wu, wd):
    n_tiles, _, tile = row_token.shape
    assert n_tiles % 2 == 0
    dm, ff = wg.shape[1], wg.shape[2]
    ch = dm // LANES
    tok = lambda f: pl.BlockSpec((None, 1, tile), f, memory_space=pltpu.SMEM)
    wspec = lambda shape, off: pl.BlockSpec((None,) + shape, lambda i, te: (te[2 * i + off], 0, 0))
    grid_spec = pltpu.PrefetchScalarGridSpec(
        num_scalar_prefetch=1, grid=(n_tiles // 2,),
        in_specs=[tok(lambda i, te: (2 * i, 0, 0)),
                  tok(lambda i, te: (2 * i + 1, 0, 0)),
                  tok(lambda i, te: (jnp.minimum(2 * i + 2, n_tiles - 1), 0, 0)),
                  pl.BlockSpec(memory_space=pl.ANY),
                  wspec((dm, ff), 0), wspec((dm, ff), 0), wspec((ff, dm), 0),
                  wspec((dm, ff), 1), wspec((dm, ff), 1), wspec((ff, dm), 1)],
        out_specs=pl.BlockSpec((2 * tile * ch, LANES), lambda i, te: (i, 0)),
        scratch_shapes=[pltpu.VMEM((tile * ch, LANES), F32), pltpu.VMEM((tile * ch, LANES), F32),
                        pltpu.SemaphoreType.DMA((2,))])
    return pl.pallas_call(
        _moe_group_kernel, grid_spec=grid_spec,
        out_shape=jax.ShapeDtypeStruct((n_tiles * tile * ch, LANES), F32),
        compiler_params=_cparams(1), name="moe_grouped")(
            tile_expert, row_token, row_token, row_token, x_rows, wg, wu, wd, wg, wu, wd)


def _moe_combine_kernel(s0_ref, s1_ref, s2_ref, ys_hbm, x2_ref, rt_ref, nf_ref, o_ref, buf, sem, *, final_norm):
    tm, dm = x2_ref.shape
    ch = dm // LANES
    i = pl.program_id(0)
    depth = buf.shape[0]
    slot = i % depth

    @pl.when(i == 0)
    def _():
        _issue_rows(s0_ref, ys_hbm, buf.at[0], sem.at[0], ch, unroll=False)
        _issue_rows(s1_ref, ys_hbm, buf.at[1], sem.at[1], ch, unroll=False)

    _wait_rows(ys_hbm, buf.at[slot], sem.at[slot])
    ahead = (i + 2) % depth
    _issue_rows(s2_ref, ys_hbm, buf.at[ahead], sem.at[ahead], ch, unroll=True)
    y1 = _from_rows(buf.at[slot], 0, tm, ch)
    y2 = _from_rows(buf.at[slot], tm, tm, ch)
    rt = rt_ref[...]
    w1, w2 = rt[:, 2:3], rt[:, 3:4]
    xf = jnp.concatenate([x2_ref[:, c * LANES:(c + 1) * LANES] + (w1 * y1[c] + w2 * y2[c]) for c in range(ch)],
                         axis=1)
    if final_norm:
        xf = xf * lax.rsqrt(jnp.mean(xf * xf, axis=-1, keepdims=True) + RMS_EPS) * nf_ref[...]
    o_ref[...] = xf

    @pl.when(i == pl.num_programs(0) - 1)
    def _():
        for s in ((i + 1) % depth, ahead):
            _wait_rows(ys_hbm, buf.at[s], sem.at[s])


def moe_combine(ys_rows, slots, x2, route, nf, tm, final_norm):
    n, dm = x2.shape
    nt = n // tm
    ch = dm // LANES
    srows = slots.reshape(nt, tm, 2).transpose(0, 2, 1).reshape(nt, 1, 2 * tm)
    sspec = lambda f: pl.BlockSpec((None, 1, 2 * tm), f, memory_space=pltpu.SMEM)
    return pl.pallas_call(
        functools.partial(_moe_combine_kernel, final_norm=final_norm), grid=(nt,),
        in_specs=[sspec(lambda i: (i, 0, 0)), sspec(lambda i: (jnp.minimum(i + 1, nt - 1), 0, 0)),
                  sspec(lambda i: (jnp.minimum(i + 2, nt - 1), 0, 0)),
                  pl.BlockSpec(memory_space=pl.ANY),
                  pl.BlockSpec((tm, dm), lambda i: (i, 0)),
                  pl.BlockSpec((tm, LANES), lambda i: (i, 0)),
                  pl.BlockSpec((1, dm), lambda i: (0, 0))],
        out_specs=pl.BlockSpec((tm, dm), lambda i: (i, 0)),
        out_shape=jax.ShapeDtypeStruct((n, dm), F32),
        scratch_shapes=[pltpu.VMEM((3, 2 * tm * ch, LANES), F32), pltpu.SemaphoreType.DMA((3,))],
        compiler_params=_cparams(1), name="moe_combine")(srows, srows, srows, ys_rows, x2, route, nf)


def moe_final(xn, route, x2, wg, wu, wd, nf, final_norm):
    e1 = route[:, 0].astype(jnp.int32)
    e2 = route[:, 1].astype(jnp.int32)
    slots, row_token, tile_expert = moe_plan(e1, e2, wg.shape[0], MOE_TILE)
    ys = moe_grouped(xn, row_token, tile_expert, wg, wu, wd)
    return moe_combine(ys, slots, x2, route, nf, min(MOE_TILE, x2.shape[0]), final_norm)


def _pad_lanes(v, width=LANES):
    return jnp.pad(v, ((0, 0), (0, width - v.shape[1])))


def _natural_rows(arr, n_rows):
    b, d, l, w = arr.shape
    tail = arr[:, :, l - n_rows // d:, :]
    return jnp.swapaxes(tail, 1, 2).reshape(b, n_rows, w)


def kernel(x_prompt, x_sample, cache_kv_w128, cache_kv_w512, cache_kv_w2048, state_ssm, state_conv,
           norm1, w_in, conv_w, conv_b, dt_bias, a_log, d_skip, ssm_norm, w_att_out, w_ssm_out, w_o,
           norm2, w_router_coarse, b_router_coarse, w_router_fine, b_router_fine,
           w_exp_gate, w_exp_up, w_exp_down, norm_f):
    bp, sp, dm = x_prompt.shape
    bs, ts, _ = x_sample.shape
    depth = w_in.shape[0]
    n_heads = dt_bias.shape[1]
    d_inner = n_heads * SSM_HEAD_DIM
    conv_dim = conv_w.shape[2]
    n_grp = len(ATT_DILATIONS)
    off_z = n_grp * 3 * ATT_WIDTH
    off_xbc = off_z + d_inner
    off_dt = off_xbc + conv_dim
    off_gate = off_dt + n_heads
    caches = (cache_kv_w128, cache_kv_w512, cache_kv_w2048)
    for g in range(n_grp):
        assert caches[g].shape[2] == ATT_WINDOWS[g] and sp % (ATT_DILATIONS[g] * ATT_BLOCK) == 0
    assert ts <= min(ATT_DILATIONS[1:]) and ts < TPAD and sp % SSM_CHUNK == 0

    tabs_p = rope_tables(jnp.arange(sp, dtype=F32))
    tabs_s = rope_tables(jnp.tile(jnp.arange(ts, dtype=F32) + PAST_LEN, bs))
    expand = (jnp.arange(LANES)[:, None] == (jnp.arange(d_inner) // SSM_HEAD_DIM)[None, :]).astype(F32)

    xp = x_prompt.reshape(bp * sp, dm)
    xs = x_sample.reshape(bs * ts, dm)
    outs = {k: [] for k in ("kvp0", "kvp1", "kvp2", "ssm_p", "conv_p", "kvs0", "kvs1", "kvs2", "ssm_s", "conv_s")}
    n_s = bs * ts
    for layer in range(depth):
        w = w_in[layer].astype(BF16)
        w_qkv = [w[:, g * 3 * ATT_WIDTH:(g + 1) * 3 * ATT_WIDTH] for g in range(n_grp)]
        w_z, w_xbc, w_gate = w[:, off_z:off_xbc], w[:, off_xbc:off_dt], w[:, off_gate:]
        w_dt = _pad_lanes(w[:, off_dt:off_gate])
        dtb = _pad_lanes(dt_bias[layer][None])
        alog = _pad_lanes(a_log[layer][None])
        dsk_e = jnp.repeat(d_skip[layer], SSM_HEAD_DIM)[None]
        nw = ssm_norm[layer][None]
        cw, cb = conv_w[layer], conv_b[layer][None]
        w_router = _pad_lanes(jnp.concatenate([w_router_coarse[layer], w_router_fine[layer]], axis=1)).astype(BF16)
        b_router = _pad_lanes(jnp.concatenate([b_router_coarse[layer], b_router_fine[layer]])[None])
        wts = (w_att_out[layer].astype(BF16), w_ssm_out[layer].astype(BF16), w_o[layer].astype(BF16),
               w_gate, norm2[layer][None], w_router, b_router)
        wg, wu, wd = (w_exp_gate[layer].astype(BF16), w_exp_up[layer].astype(BF16),
                      w_exp_down[layer].astype(BF16))

        xn = rmsnorm_bf16(xp, norm1[layer], 1024)
        att = []
        for g, d in enumerate(ATT_DILATIONS):
            qkv = qkv_proj(xn, w_qkv[g], tabs_p, bp, sp, d, 1024, BF16, f"qkv_prompt_d{d}")
            att.append(attn_prompt(qkv))
            wnd = min(ATT_WINDOWS[g], sp)
            kv = jnp.stack([_natural_rows(qkv[1], wnd), _natural_rows(qkv[2], wnd)], axis=2)
            outs[f"kvp{g}"].append(kv.astype(F32).reshape(bp, wnd, 2, ATT_HEADS, ATT_HEAD_DIM))
        z = matmul(xn, w_z, BF16, 2048, 1024, "proj_z")
        xbc = matmul(xn, w_xbc, BF16, 2048, 1024, "proj_xbc")
        dt_raw = matmul(xn, w_dt, F32, 2048, LANES, "proj_dt")
        y_ssm, h_p = ssd_prompt(xbc, z, dt_raw, cw, cb, dtb, alog, dsk_e, nw, bp, sp)
        outs["ssm_p"].append(h_p.reshape(bp, n_heads, SSM_HEAD_DIM, SSM_STATE))
        outs["conv_p"].append(xbc.reshape(bp, sp, conv_dim)[:, sp - (SSM_CONV - 1):].astype(F32))
        x2, xn2, cmb = outproj(xp, xn, y_ssm, att, ATT_DILATIONS, bp, sp, wts, 512)
        xp = moe_final(xn2, cmb, x2, wg, wu, wd, norm_f[None], layer == depth - 1)

        xn = rmsnorm_bf16(xs, norm1[layer], n_s)
        pad_t = lambda a: jnp.pad(a.reshape(bs, ts, a.shape[-1]), ((0, 0), (0, TPAD - ts), (0, 0)))
        new = jnp.stack([qkv_proj(xn, w_qkv[g], tabs_s, 1, n_s, 1, n_s, F32, f"qkv_sample_{g}")
                         .reshape(3, bs, ts, ATT_WIDTH) for g in range(n_grp)])
        new = jnp.pad(new, ((0, 0), (0, 0), (0, 0), (0, TPAD - ts), (0, 0)))
        cl = [jnp.transpose(caches[g][layer], (0, 2, 3, 4, 1)) for g in range(n_grp)]
        res = cache_attn(new, cl, ts)
        attn_s = res[0][:, :ts]
        for g in range(n_grp):
            outs[f"kvs{g}"].append(jnp.transpose(res[1 + g], (0, 4, 1, 2, 3)))
        z = matmul(xn, w_z, F32, n_s, 512, "proj_z_s")
        xbc = matmul(xn, w_xbc, F32, n_s, 512, "proj_xbc_s")
        dt_raw = matmul(xn, w_dt, F32, n_s, LANES, "proj_dt_s")
        y_s, h_s = ssd_sample(pad_t(xbc), pad_t(z), pad_t(dt_raw), state_conv[layer],
                              state_ssm[layer].reshape(bs, d_inner, SSM_STATE), cw, cb, dtb, alog, dsk_e, nw,
                              expand, ts)
        outs["ssm_s"].append(h_s.reshape(bs, n_heads, SSM_HEAD_DIM, SSM_STATE))
        hist = jnp.concatenate([state_conv[layer], xbc.reshape(bs, ts, conv_dim)], axis=1)
        outs["conv_s"].append(hist[:, ts:])
        y_s = y_s[:, :ts].reshape(n_s, d_inner)
        x2, xn2, cmb = outproj(xs, xn, y_s, attn_s.reshape(n_s, ATT_WIDTH), (), 1, n_s, wts, min(512, n_s))
        xs = moe_final(xn2, cmb, x2, wg, wu, wd, norm_f[None], layer == depth - 1)

    st = lambda k: jnp.stack(outs[k])
    return (xp.reshape(bp, sp, dm), xs.reshape(bs, ts, dm),
            st("kvp0"), st("kvp1"), st("kvp2"), st("ssm_p"), st("conv_p"),
            st("kvs0"), st("kvs1"), st("kvs2"), st("ssm_s"), st("conv_s"))
```

```python
import functools
import math

import jax
import jax.numpy as jnp
from jax import lax
from jax.experimental import pallas as pl
from jax.experimental.pallas import tpu as pltpu

F32 = jnp.float32
BF16 = jnp.bfloat16

PAST_LEN = 8192
ATT_WINDOWS = (128, 512, 2048)
ATT_DILATIONS = (1, 4, 16)
ATT_HEADS = 8
ATT_HEAD_DIM = 64
ATT_WIDTH = ATT_HEADS * ATT_HEAD_DIM
ATT_SCALE = ATT_HEAD_DIM ** -0.5
ROT_DIM = ATT_HEAD_DIM // 4
ROPE_THETA = 500000.0
ATT_BLOCK = 128

SSM_HEAD_DIM = 64
SSM_STATE = 128
SSM_GROUPS = 4
SSM_CONV = 4
SSM_CHUNK = 128
MOE_GROUPS = 4
MOE_PER_GROUP = 4
MOE_EXPERTS = MOE_GROUPS * MOE_PER_GROUP
RMS_EPS = 1e-6
SSM_NORM_EPS = 1e-5

LANES = 128
VMEM_LIMIT = 56 * 1024 * 1024
NEG_INF = float("-inf")


def _cparams(n_axes):
    return pltpu.CompilerParams(dimension_semantics=("arbitrary",) * n_axes,
                                vmem_limit_bytes=VMEM_LIMIT)


def _sigmoid(x):
    return 1.0 / (1.0 + jnp.exp(-x))


def _silu(x):
    return x * _sigmoid(x)


def _norm_kernel(x_ref, g_ref, o_ref):
    x = x_ref[...]
    y = x * lax.rsqrt(jnp.mean(x * x, axis=-1, keepdims=True) + RMS_EPS)
    o_ref[...] = (y * g_ref[...]).astype(o_ref.dtype)


def rmsnorm_bf16(x, g, tm):
    n, d = x.shape
    return pl.pallas_call(
        _norm_kernel, grid=(n // tm,),
        in_specs=[pl.BlockSpec((tm, d), lambda i: (i, 0)), pl.BlockSpec((1, d), lambda i: (0, 0))],
        out_specs=pl.BlockSpec((tm, d), lambda i: (i, 0)),
        out_shape=jax.ShapeDtypeStruct((n, d), BF16),
        compiler_params=_cparams(1), name="rmsnorm")(x, g.reshape(1, d))


def _mm_kernel(x_ref, w_ref, o_ref):
    o_ref[...] = jnp.dot(x_ref[...], w_ref[...], preferred_element_type=F32).astype(o_ref.dtype)


def matmul(x, w, out_dtype, tm, tn, name):
    m, k = x.shape
    n = w.shape[1]
    return pl.pallas_call(
        _mm_kernel, grid=(m // tm, n // tn),
        in_specs=[pl.BlockSpec((tm, k), lambda i, j: (i, 0)), pl.BlockSpec((k, tn), lambda i, j: (0, j))],
        out_specs=pl.BlockSpec((tm, tn), lambda i, j: (i, j)),
        out_shape=jax.ShapeDtypeStruct((m, n), out_dtype),
        compiler_params=_cparams(2), name=name)(x, w)


QKV_CHUNK = 256


def _qkv_kernel(x_ref, w_ref, cos_ref, sa_ref, sb_ref, o_ref, acc_ref, *, d):
    sc = jnp.where(pl.program_id(1) == 0, ATT_SCALE, 1.0).astype(F32)
    tm = x_ref.shape[0]
    ck = min(QKV_CHUNK, tm)
    for c0 in range(0, tm, ck):
        rows = slice(c0, c0 + ck)
        acc = jnp.dot(x_ref[rows, :], w_ref[...], preferred_element_type=F32)
        c = cos_ref[rows, :] * sc
        sa = sa_ref[rows, :] * sc
        sb = sb_ref[rows, :] * sc
        for ch in range(ATT_WIDTH // LANES):
            cs = slice(ch * LANES, (ch + 1) * LANES)
            t = acc[:, cs]
            r = t * c + pltpu.roll(t, LANES - ROT_DIM // 2, 1) * sa + pltpu.roll(t, ROT_DIM // 2, 1) * sb
            if d == 1:
                o_ref[0, rows, cs] = r.astype(o_ref.dtype)
            else:
                acc_ref[ch, rows, :] = r
                for res in range(d):
                    o_ref[res, c0 // d:(c0 + ck) // d, cs] = acc_ref[
                        ch, pl.ds(c0 + res, ck // d, stride=d), :].astype(o_ref.dtype)


def qkv_proj(xn, w, tabs, batch, seq, d, tm, out_dtype, name):
    n, dm = xn.shape
    tpb = seq // tm
    cos, sa, sb = tabs
    tab_spec = pl.BlockSpec((None, tm, LANES), lambda i, j: (jnp.where(j == 2, 1, 0), i % tpb, 0))
    return pl.pallas_call(
        functools.partial(_qkv_kernel, d=d), grid=(n // tm, 3),
        in_specs=[pl.BlockSpec((tm, dm), lambda i, j: (i, 0)),
                  pl.BlockSpec((dm, ATT_WIDTH), lambda i, j: (0, j)),
                  tab_spec, tab_spec, tab_spec],
        out_specs=pl.BlockSpec((None, None, d, tm // d, ATT_WIDTH),
                               lambda i, j: (j, i // tpb, 0, i % tpb, 0)),
        out_shape=jax.ShapeDtypeStruct((3, batch, d, seq // d, ATT_WIDTH), out_dtype),
        scratch_shapes=[pltpu.VMEM((ATT_WIDTH // LANES, tm, LANES), F32)],
        compiler_params=_cparams(2), name=name)(xn, w, cos, sa, sb)


def rope_tables(pos):
    half = ROT_DIM // 2
    inv = ROPE_THETA ** (-jnp.arange(half, dtype=F32) / half)
    ang = pos[:, None] * inv[None, :]
    cos = jnp.cos(ang)
    sin = jnp.sin(ang)
    s = pos.shape[0]
    ones = jnp.ones((s, ATT_HEAD_DIM - ROT_DIM), F32)
    zeros = jnp.zeros((s, ATT_HEAD_DIM - ROT_DIM), F32)
    zh = jnp.zeros((s, half), F32)
    c = jnp.concatenate([cos, cos, ones], axis=1)
    sa = jnp.concatenate([-sin, zh, zeros], axis=1)
    sb = jnp.concatenate([zh, sin, zeros], axis=1)
    rep = LANES // ATT_HEAD_DIM
    c, sa, sb = (jnp.tile(t, (1, rep)) for t in (c, sa, sb))
    return (jnp.stack([c, jnp.ones_like(c)]), jnp.stack([sa, jnp.zeros_like(sa)]),
            jnp.stack([sb, jnp.zeros_like(sb)]))


ATT_ROW_CHUNK = 32
ATT_SUB = 2


def _attn_kernel(q_ref, kc_ref, vc_ref, kp_ref, vp_ref, num_ref, st_ref, s_scr, p_scr):
    nb = pl.program_id(2)
    blk = ATT_BLOCK
    nt = (((1,), (1,)), ((), ()))
    heads = [slice(h * ATT_HEAD_DIM, (h + 1) * ATT_HEAD_DIM) for h in range(ATT_HEADS)]
    subs = [slice(sb * blk, (sb + 1) * blk) for sb in range(ATT_SUB)]

    def prev(ref_p, ref_c, sb, sl):
        return ref_p[:, sl] if sb == 0 else ref_c[subs[sb - 1], sl]

    for sb, rows in enumerate(subs):
        for h, sl in enumerate(heads):
            q = q_ref[rows, sl]
            s_scr[sb, h, :, 0:blk] = lax.dot_general(q, prev(kp_ref, kc_ref, sb, sl), nt,
                                                     preferred_element_type=F32)
            s_scr[sb, h, :, blk:2 * blk] = lax.dot_general(q, kc_ref[rows, sl], nt, preferred_element_type=F32)
    rc = ATT_ROW_CHUNK
    qi = lax.broadcasted_iota(jnp.int32, (rc, 2 * blk), 0)
    kj = lax.broadcasted_iota(jnp.int32, (rc, 2 * blk), 1)
    lane = lax.broadcasted_iota(jnp.int32, (rc, LANES), 1)
    for sb, rows in enumerate(subs):
        for r0 in range(0, blk, rc):
            qa = qi + r0
            in_prev = jnp.logical_and(kj < blk, kj >= qa)
            if sb == 0:
                in_prev = jnp.logical_and(in_prev, nb > 0)
            mask = jnp.logical_or(in_prev, jnp.logical_and(kj >= blk, kj - blk <= qa))
            st = jnp.zeros((rc, LANES), F32)
            for h in range(ATT_HEADS):
                s = jnp.where(mask, s_scr[sb, h, r0:r0 + rc, :], NEG_INF)
                m = jnp.max(s, axis=-1, keepdims=True)
                p = jnp.exp(s - m)
                l = jnp.sum(p, axis=-1, keepdims=True)
                p_scr[sb, h, r0:r0 + rc, :] = p.astype(BF16)
                st = jnp.where(lane == h, m, st)
                st = jnp.where(lane == ATT_HEADS + h, l, st)
            st_ref[sb * blk + r0:sb * blk + r0 + rc, :] = st
    for sb, rows in enumerate(subs):
        for h, sl in enumerate(heads):
            num_ref[rows, sl] = (
                jnp.dot(p_scr[sb, h, :, 0:blk], prev(vp_ref, vc_ref, sb, sl), preferred_element_type=F32)
                + jnp.dot(p_scr[sb, h, :, blk:2 * blk], vc_ref[rows, sl], preferred_element_type=F32))


def attn_prompt(qkv):
    _, b, d, l, w = qkv.shape
    step = ATT_SUB * ATT_BLOCK
    nstep = l // step

    def spec(kind, prev):
        if prev:
            return pl.BlockSpec((None, None, None, ATT_BLOCK, w),
                                lambda bi, r, n: (kind, bi, r, jnp.maximum(ATT_SUB * n - 1, 0), 0))
        return pl.BlockSpec((None, None, None, step, w), lambda bi, r, n: (kind, bi, r, n, 0))

    return pl.pallas_call(
        _attn_kernel, grid=(b, d, nstep),
        in_specs=[spec(0, False), spec(1, False), spec(2, False), spec(1, True), spec(2, True)],
        out_specs=[pl.BlockSpec((None, None, step, w), lambda bi, r, n: (bi, r, n, 0)),
                   pl.BlockSpec((None, None, step, LANES), lambda bi, r, n: (bi, r, n, 0))],
        out_shape=[jax.ShapeDtypeStruct((b, d, l, w), F32), jax.ShapeDtypeStruct((b, d, l, LANES), F32)],
        scratch_shapes=[pltpu.VMEM((ATT_SUB, ATT_HEADS, ATT_BLOCK, 2 * ATT_BLOCK), F32),
                        pltpu.VMEM((ATT_SUB, ATT_HEADS, ATT_BLOCK, 2 * ATT_BLOCK), BF16)],
        compiler_params=_cparams(3), name=f"attn_prompt_d{d}")(qkv, qkv, qkv, qkv, qkv)


def _col(mat, h, n_lanes=LANES):
    return jnp.broadcast_to(mat[:, h:h + 1], (mat.shape[0], n_lanes))


def _ssd_kernel(xbc_ref, z_ref, dt_ref, cw_ref, cb_ref, dtb_ref, alog_ref, dsk_ref, nw_ref,
                y_ref, hout_ref,
                cbuf, xs_s, xw_s, eac_s, cd_s, y_s, ht_s, *, d_inner, n_groups):
    c = pl.program_id(1)
    nc = pl.num_programs(1)
    q = SSM_CHUNK
    n_st = SSM_STATE
    gw = d_inner // n_groups
    conv_dim = xbc_ref.shape[1]
    top = 8

    @pl.when(c == 0)
    def _():
        cbuf[0:top, :] = jnp.zeros((top, conv_dim), F32)
        ht_s[...] = jnp.zeros(ht_s.shape, F32)

    cbuf[top:top + q, :] = xbc_ref[...].astype(F32)
    cwid = 512
    for j in range(conv_dim // cwid):
        cs = slice(j * cwid, (j + 1) * cwid)
        acc = cb_ref[:, cs] + cbuf[top - 3:top - 3 + q, cs] * cw_ref[0:1, cs]
        for i in range(1, SSM_CONV):
            acc = acc + cbuf[top - 3 + i:top - 3 + i + q, cs] * cw_ref[i:i + 1, cs]
        xs_s[:, cs] = _silu(acc)
    cbuf[top - 3:top, :] = cbuf[top + q - 3:top + q, :]

    dt = jax.nn.softplus(dt_ref[...] + dtb_ref[...])
    a_row = -jnp.exp(alog_ref[...])
    a = dt * a_row
    ri = lax.broadcasted_iota(jnp.int32, (q, q), 0)
    ci = lax.broadcasted_iota(jnp.int32, (q, q), 1)
    causal = ri >= ci
    tril = jnp.where(causal, 1.0, 0.0).astype(F32)
    acum = jnp.dot(tril, a, preferred_element_type=F32, precision=lax.Precision.HIGHEST)
    acum_t = acum.T
    lane = lax.broadcasted_iota(jnp.int32, (q, LANES), 1)
    lo = lane < SSM_HEAD_DIM

    hpg = gw // SSM_HEAD_DIM
    for g in range(n_groups):
        bc = xs_s[:, d_inner + g * n_st:d_inner + (g + 1) * n_st].astype(BF16)
        cc = xs_s[:, d_inner + n_groups * n_st + g * n_st:d_inner + n_groups * n_st + (g + 1) * n_st].astype(BF16)
        cb = lax.dot_general(cc, bc, (((1,), (1,)), ((), ())), preferred_element_type=F32)
        for jp in range(hpg // 2):
            h0 = g * hpg + 2 * jp
            ls = slice(g * gw + jp * LANES, g * gw + (jp + 1) * LANES)
            ac0 = _col(acum, h0)
            ac1 = _col(acum, h0 + 1)
            acum_e = jnp.where(lo, ac0, ac1)
            dt_e = jnp.where(lo, _col(dt, h0), _col(dt, h0 + 1))
            xdt = xs_s[:, ls] * dt_e
            acl_e = acum_e[q - 1:q, :]
            xw_s[:, ls] = (xdt * jnp.exp(acl_e - acum_e)).astype(BF16)
            eac_s[:, ls] = jnp.exp(acum_e)
            cd_s[:, ls] = jnp.exp(acl_e)
            xdt_b = xdt.astype(BF16)
            zero = jnp.zeros_like(xdt_b)
            m0 = (jnp.exp(jnp.where(causal, ac0 - acum_t[h0:h0 + 1, :], NEG_INF)) * cb).astype(BF16)
            m1 = (jnp.exp(jnp.where(causal, ac1 - acum_t[h0 + 1:h0 + 2, :], NEG_INF)) * cb).astype(BF16)
            y_s[:, ls] = (jnp.dot(m0, jnp.where(lo, xdt_b, zero), preferred_element_type=F32)
                          + jnp.dot(m1, jnp.where(lo, zero, xdt_b), preferred_element_type=F32))
        gs = slice(g * gw, (g + 1) * gw)
        h_prev = ht_s[g]
        y_off = jnp.dot(cc, h_prev.astype(BF16), preferred_element_type=F32) * eac_s[:, gs]
        y_s[:, gs] = y_s[:, gs] + y_off
        st = lax.dot_general(bc, xw_s[:, gs], (((0,), (0,)), ((), ())), preferred_element_type=F32)
        ht_s[g] = cd_s[:, gs] * h_prev + st

    for g in range(n_groups):
        gs = slice(g * gw, (g + 1) * gw)
        y = y_s[:, gs] + dsk_ref[:, gs] * xs_s[:, gs]
        yf = y * _silu(z_ref[:, gs].astype(F32))
        yf = yf * lax.rsqrt(jnp.mean(yf * yf, axis=-1, keepdims=True) + SSM_NORM_EPS)
        y_ref[:, gs] = (yf * nw_ref[:, gs]).astype(y_ref.dtype)

    @pl.when(c == nc - 1)
    def _():
        for g in range(n_groups):
            hout_ref[g * gw:(g + 1) * gw, :] = ht_s[g].T


def ssd_prompt(xbc, z, dt_raw, conv_w, conv_b, dtb, alog, dsk_e, nw, batch, seq):
    n, conv_dim = xbc.shape
    d_inner = z.shape[1]
    n_groups = SSM_GROUPS
    gw = d_inner // n_groups
    q = SSM_CHUNK
    cps = seq // q
    row = lambda w: pl.BlockSpec((1, w), lambda b, c: (0, 0))
    tok = lambda w: pl.BlockSpec((q, w), lambda b, c: (b * cps + c, 0))
    return pl.pallas_call(
        functools.partial(_ssd_kernel, d_inner=d_inner, n_groups=n_groups),
        grid=(batch, cps),
        in_specs=[tok(conv_dim), tok(d_inner), tok(LANES),
                  pl.BlockSpec((SSM_CONV, conv_dim), lambda b, c: (0, 0)), row(conv_dim),
                  row(LANES), row(LANES), row(d_inner), row(d_inner)],
        out_specs=[tok(d_inner), pl.BlockSpec((None, d_inner, SSM_STATE), lambda b, c: (b, 0, 0))],
        out_shape=[jax.ShapeDtypeStruct((n, d_inner), BF16),
                   jax.ShapeDtypeStruct((batch, d_inner, SSM_STATE), F32)],
        scratch_shapes=[pltpu.VMEM((8 + q, conv_dim), F32),
                        pltpu.VMEM((q, conv_dim), F32),
                        pltpu.VMEM((q, d_inner), BF16),
                        pltpu.VMEM((q, d_inner), F32),
                        pltpu.VMEM((1, d_inner), F32),
                        pltpu.VMEM((q, d_inner), F32),
                        pltpu.VMEM((n_groups, SSM_STATE, gw), F32)],
        compiler_params=_cparams(2), name="ssd_prompt")(xbc, z, dt_raw, conv_w, conv_b, dtb, alog, dsk_e, nw)


TPAD = 8


def _ssd_step_kernel(xbc_ref, z_ref, dt_ref, cst_ref, h0_ref, cw_ref, cb_ref, dtb_ref, alog_ref,
                     dsk_ref, nw_ref, exp_ref, y_ref, hout_ref,
                     cbuf, xs_s, f_s, *, d_inner, n_groups, n_tok):
    b = pl.program_id(0)
    n_st = SSM_STATE
    gw = d_inner // n_groups
    conv_dim = xbc_ref.shape[1]
    top = 8
    hist = SSM_CONV - 1

    @pl.when(b == 0)
    def _():
        cbuf[0:top, :] = jnp.zeros((top, conv_dim), F32)

    cbuf[top - hist:top, :] = cst_ref[...]
    cbuf[top:top + TPAD, :] = xbc_ref[...]
    acc = cb_ref[...] + cbuf[top - hist:top - hist + TPAD, :] * cw_ref[0:1, :]
    for i in range(1, SSM_CONV):
        acc = acc + cbuf[top - hist + i:top - hist + i + TPAD, :] * cw_ref[i:i + 1, :]
    xs_s[...] = _silu(acc)

    rid = lax.broadcasted_iota(jnp.int32, (TPAD, LANES), 0)
    dt = jax.nn.softplus(dt_ref[...] + dtb_ref[...])
    da = dt * (-jnp.exp(alog_ref[...]))
    cum = da
    for k in range(1, n_tok):
        cum = cum + jnp.where(rid >= k, pltpu.roll(da, k, 0), 0.0)
    facs = [dt, jnp.exp(cum)]
    for t in range(n_tok):
        facs.append(jnp.where(rid <= t, jnp.exp(cum[t:t + 1, :] - cum), 0.0))
    hi = lax.Precision.HIGHEST
    f_s[...] = jnp.dot(jnp.concatenate(facs, axis=0), exp_ref[...], preferred_element_type=F32, precision=hi)

    nt = (((1,), (1,)), ((), ()))
    row8 = lax.broadcasted_iota(jnp.int32, (TPAD, gw), 0)
    for g in range(n_groups):
        gs = slice(g * gw, (g + 1) * gw)
        xdt = xs_s[:, gs] * f_s[0:TPAD, gs]
        b_f = xs_s[:, d_inner + g * n_st:d_inner + (g + 1) * n_st]
        c_f = xs_s[:, d_inner + (n_groups + g) * n_st:d_inner + (n_groups + g + 1) * n_st]
        b16 = b_f.astype(BF16)
        c16 = c_f.astype(BF16)
        h0g = h0_ref[gs, :]
        y = f_s[TPAD:2 * TPAD, gs] * lax.dot_general(c16, h0g.astype(BF16), nt, preferred_element_type=F32)
        bc = lax.dot_general(b16, c16, nt, preferred_element_type=F32)
        for t in range(n_tok):
            term = bc[:, t:t + 1] * f_s[(2 + t) * TPAD:(3 + t) * TPAD, gs] * xdt
            y = y + jnp.where(row8 == t, jnp.sum(term, axis=0, keepdims=True), 0.0)
        yv = y + dsk_ref[:, gs] * xs_s[:, gs]
        yf = yv * _silu(z_ref[:, gs])
        yf = yf * lax.rsqrt(jnp.mean(yf * yf, axis=-1, keepdims=True) + SSM_NORM_EPS)
        y_ref[:, gs] = yf * nw_ref[:, gs]

        dend = f_s[(1 + n_tok) * TPAD:(2 + n_tok) * TPAD, gs]
        pend = f_s[TPAD + n_tok - 1:TPAD + n_tok, gs]
        lhs = jnp.where(row8 == n_tok, pend, dend * xdt)
        rhs = jnp.concatenate([jnp.where(rid < n_tok, b_f, 0.0), jnp.where(rid == n_tok, 1.0, 0.0)], axis=1)
        res = lax.dot_general(lhs, rhs, (((0,), (0,)), ((), ())), preferred_element_type=F32, precision=hi)
        hout_ref[gs, :] = res[:, n_st:] * h0g + res[:, :n_st]


def ssd_sample(xbc, z, dt_raw, conv_state, h0, conv_w, conv_b, dtb, alog, dsk_e, nw, expand, n_tok):
    bsz, _, conv_dim = xbc.shape
    d_inner = z.shape[2]
    row = lambda w: pl.BlockSpec((1, w), lambda b: (0, 0))
    tok = lambda w: pl.BlockSpec((None, TPAD, w), lambda b: (b, 0, 0))
    st = pl.BlockSpec((None, d_inner, SSM_STATE), lambda b: (b, 0, 0))
    return pl.pallas_call(
        functools.partial(_ssd_step_kernel, d_inner=d_inner, n_groups=SSM_GROUPS, n_tok=n_tok),
        grid=(bsz,),
        in_specs=[tok(conv_dim), tok(d_inner), tok(LANES),
                  pl.BlockSpec((None, SSM_CONV - 1, conv_dim), lambda b: (b, 0, 0)), st,
                  pl.BlockSpec((SSM_CONV, conv_dim), lambda b: (0, 0)), row(conv_dim),
                  row(LANES), row(LANES), row(d_inner), row(d_inner),
                  pl.BlockSpec((LANES, d_inner), lambda b: (0, 0))],
        out_specs=[tok(d_inner), st],
        out_shape=[jax.ShapeDtypeStruct((bsz, TPAD, d_inner), F32),
                   jax.ShapeDtypeStruct((bsz, d_inner, SSM_STATE), F32)],
        scratch_shapes=[pltpu.VMEM((8 + TPAD, conv_dim), F32),
                        pltpu.VMEM((TPAD, conv_dim), F32),
                        pltpu.VMEM(((2 + n_tok) * TPAD, d_inner), F32)],
        compiler_params=_cparams(1), name="ssd_sample")(
            xbc, z, dt_raw, conv_state, h0, conv_w, conv_b, dtb, alog, dsk_e, nw, expand)


HEADS_PER_STEP = 4


def _cache_attn_kernel(new_ref, c0_ref, c1_ref, c2_ref, o_ref, n0_ref, n1_ref, n2_ref, stage, tts, *, n_tok):
    first = jnp.logical_and(pl.program_id(0) == 0, pl.program_id(1) == 0)

    @pl.when(first)
    def _():
        stage[...] = jnp.zeros(stage.shape, F32)

    crefs = (c0_ref, c1_ref, c2_ref)
    orefs = (n0_ref, n1_ref, n2_ref)
    hd = ATT_HEAD_DIM
    nt = (((1,), (1,)), ((), ()))
    lane = lax.broadcasted_iota(jnp.int32, (hd, LANES), 1)
    t_new = lax.broadcasted_iota(jnp.int32, (TPAD, TPAD), 0)
    u_new = lax.broadcasted_iota(jnp.int32, (TPAD, TPAD), 1)

    for head in range(HEADS_PER_STEP):
        hs = slice(head * hd, (head + 1) * hd)
        parts = []
        for g, d in enumerate(ATT_DILATIONS):
            w = crefs[g].shape[-1]
            q = new_ref[g, 0, :, hs].astype(BF16)
            kn = new_ref[g, 1, :, hs].astype(BF16)
            vn = new_ref[g, 2, :, hs].astype(BF16)
            t_id = lax.broadcasted_iota(jnp.int32, (TPAD, w), 0)
            r_id = lax.broadcasted_iota(jnp.int32, (TPAD, w), 1)
            s = jnp.dot(q, crefs[g][0, head].astype(BF16), preferred_element_type=F32)
            s = jnp.where((r_id >= t_id) if d == 1 else ((r_id & (d - 1)) == t_id), s, NEG_INF)
            sn = lax.dot_general(q, kn, nt, preferred_element_type=F32)
            sn = jnp.where((u_new <= t_new) if d == 1 else (u_new == t_new), sn, NEG_INF)
            m = jnp.maximum(jnp.max(s, axis=1, keepdims=True), jnp.max(sn, axis=1, keepdims=True))
            p = jnp.exp(s - m)
            pn = jnp.exp(sn - m)
            l = jnp.sum(p, axis=1, keepdims=True) + jnp.sum(pn, axis=1, keepdims=True)
            num = (lax.dot_general(p.astype(BF16), crefs[g][1, head].astype(BF16), nt,
                                   preferred_element_type=F32)
                   + jnp.dot(pn.astype(BF16), vn, preferred_element_type=F32))
            parts.append((num, m, l))
        mx = jnp.maximum(jnp.maximum(parts[0][1], parts[1][1]), parts[2][1])
        wg = [jnp.exp(p_[1] - mx) for p_ in parts]
        num = wg[0] * parts[0][0]
        for g in range(1, 3):
            num = num + wg[g] * parts[g][0]
        den = wg[0] * parts[0][2] + wg[1] * parts[1][2] + wg[2] * parts[2][2]
        o_ref[:, hs] = num / den

    for pair in range(HEADS_PER_STEP // 2):
        ps = slice(pair * LANES, (pair + 1) * LANES)
        for g in range(3):
            for kv in range(2):
                stage[0:TPAD, :] = new_ref[g, 1 + kv, :, ps]
                tts[2 * g + kv] = stage[...].T
        for hh in range(2):
            head = pair * 2 + hh
            hs = slice(hh * hd, (hh + 1) * hd)
            for g in range(3):
                n_col = crefs[g].shape[-1] // LANES
                for kv in range(2):
                    cur = pltpu.roll(crefs[g][kv, head, :, 0:LANES], LANES - n_tok, 1)
                    for j in range(n_col):
                        if j + 1 < n_col:
                            nxt = pltpu.roll(crefs[g][kv, head, :, (j + 1) * LANES:(j + 2) * LANES],
                                             LANES - n_tok, 1)
                        else:
                            nxt = pltpu.roll(tts[2 * g + kv, hs, :], LANES - n_tok, 1)
                        orefs[g][kv, head, :, j * LANES:(j + 1) * LANES] = jnp.where(
                            lane < LANES - n_tok, cur, nxt)
                        cur = nxt


def cache_attn(new, caches, n_tok):
    bsz = new.shape[2]
    hps = HEADS_PER_STEP
    cspec = lambda c: pl.BlockSpec((None, 2, hps, ATT_HEAD_DIM, c.shape[-1]), lambda b, h: (b, 0, h, 0, 0))
    return pl.pallas_call(
        functools.partial(_cache_attn_kernel, n_tok=n_tok), grid=(bsz, ATT_HEADS // hps),
        in_specs=[pl.BlockSpec((3, 3, None, TPAD, hps * ATT_HEAD_DIM), lambda b, h: (0, 0, b, 0, h))]
        + [cspec(c) for c in caches],
        out_specs=[pl.BlockSpec((None, TPAD, hps * ATT_HEAD_DIM), lambda b, h: (b, 0, h))]
        + [cspec(c) for c in caches],
        out_shape=[jax.ShapeDtypeStruct((bsz, TPAD, ATT_WIDTH), F32)]
        + [jax.ShapeDtypeStruct(c.shape, c.dtype) for c in caches],
        scratch_shapes=[pltpu.VMEM((LANES, LANES), F32), pltpu.VMEM((6, LANES, LANES), F32)],
        compiler_params=_cparams(2), name="cache_attn")(new, *caches)


def _router(logits):
    lanef = lax.broadcasted_iota(jnp.int32, logits.shape, 1).astype(F32)
    big = 1e9
    lc = jnp.where(lanef < MOE_GROUPS, logits, NEG_INF)
    mc = jnp.max(lc, axis=-1, keepdims=True)
    g_sel = jnp.min(jnp.where(lc == mc, lanef, big), axis=-1, keepdims=True)
    p_sel = 1.0 / jnp.sum(jnp.exp(lc - mc), axis=-1, keepdims=True)
    base = MOE_GROUPS + MOE_PER_GROUP * g_sel
    lf = jnp.where(jnp.logical_and(lanef >= base, lanef < base + MOE_PER_GROUP), logits, NEG_INF)
    v1 = jnp.max(lf, axis=-1, keepdims=True)
    i1 = jnp.min(jnp.where(lf == v1, lanef, big), axis=-1, keepdims=True)
    lf2 = jnp.where(lanef == i1, NEG_INF, lf)
    v2 = jnp.max(lf2, axis=-1, keepdims=True)
    i2 = jnp.min(jnp.where(lf2 == v2, lanef, big), axis=-1, keepdims=True)
    e2 = jnp.exp(v2 - v1)
    den = 1.0 + e2
    w1 = (1.0 / den) * p_sel
    w2 = (e2 / den) * p_sel
    out = jnp.where(lanef == 0.0, i1 - MOE_GROUPS, 0.0)
    out = jnp.where(lanef == 1.0, i2 - MOE_GROUPS, out)
    out = jnp.where(lanef == 2.0, w1, out)
    return jnp.where(lanef == 3.0, w2, out)


OUTPROJ_CHUNK = 256


def _outproj_kernel(*refs, dils, tm):
    n_g = len(dils)
    if n_g:
        x_ref, xn1_ref, ys_ref = refs[:3]
        att_refs = refs[3:3 + 2 * n_g]
        rest = refs[3 + 2 * n_g:]
    else:
        x_ref, xn1_ref, ys_ref, attn_ref = refs[:4]
        rest = refs[4:]
    wa_ref, ws_ref, wo_ref, wg_ref, n2_ref, wr_ref, br_ref, x2_ref, xn_ref, cmb_ref = rest[:10]
    scr = rest[10:]

    dm = x_ref.shape[1]
    n_ch = ATT_WIDTH // LANES
    rc = min(OUTPROJ_CHUNK, tm)
    lane = lax.broadcasted_iota(jnp.int32, (rc, LANES), 1)
    lo = lane < ATT_HEAD_DIM
    for c0 in range(0, tm, rc):
        rows = slice(c0, c0 + rc)
        if n_g:
            nums, stats = [], []
            for g, d in enumerate(dils):
                num_ref, st_ref = att_refs[2 * g], att_refs[2 * g + 1]
                if d == 1:
                    nums.append([num_ref[0, rows, ch * LANES:(ch + 1) * LANES] for ch in range(n_ch)])
                    stats.append(st_ref[0, rows, :])
                else:
                    ns, ss = scr[2 * g], scr[2 * g + 1]
                    rr = slice(c0 // d, (c0 + rc) // d)
                    for r in range(d):
                        for ch in range(n_ch):
                            ns[ch, pl.ds(c0 + r, rc // d, stride=d), :] = num_ref[r, rr, ch * LANES:(ch + 1) * LANES]
                        ss[pl.ds(c0 + r, rc // d, stride=d), :] = st_ref[r, rr, :]
                    nums.append([ns[ch, rows, :] for ch in range(n_ch)])
                    stats.append(ss[rows, :])
            chunks = []
            for ch in range(n_ch):
                wts, dens = [], []
                for h in (2 * ch, 2 * ch + 1):
                    ms = [s[:, h:h + 1] for s in stats]
                    ls = [s[:, ATT_HEADS + h:ATT_HEADS + h + 1] for s in stats]
                    mx = ms[0]
                    for m in ms[1:]:
                        mx = jnp.maximum(mx, m)
                    w = [jnp.exp(m - mx) for m in ms]
                    den = w[0] * ls[0]
                    for g in range(1, n_g):
                        den = den + w[g] * ls[g]
                    wts.append(w)
                    dens.append(den)
                num = jnp.where(lo, wts[0][0], wts[1][0]) * nums[0][ch]
                for g in range(1, n_g):
                    num = num + jnp.where(lo, wts[0][g], wts[1][g]) * nums[g][ch]
                chunks.append((num / jnp.where(lo, dens[0], dens[1])).astype(BF16))
            attn = jnp.concatenate(chunks, axis=1)
        else:
            attn = attn_ref[rows, :].astype(BF16)

        a = jnp.dot(attn, wa_ref[...], preferred_element_type=F32)
        s = jnp.dot(ys_ref[rows, :].astype(BF16), ws_ref[...], preferred_element_type=F32)
        gates = jnp.dot(xn1_ref[rows, :], wg_ref[...], preferred_element_type=F32)
        mixed = _sigmoid(gates[:, :dm]) * a + _sigmoid(gates[:, dm:]) * s
        x2 = x_ref[rows, :] + jnp.dot(mixed.astype(BF16), wo_ref[...], preferred_element_type=F32)
        x2_ref[rows, :] = x2
        xn = x2 * lax.rsqrt(jnp.mean(x2 * x2, axis=-1, keepdims=True) + RMS_EPS) * n2_ref[...]
        _to_rows(xn_ref.at[pl.ds(c0 * (dm // LANES), rc * (dm // LANES)), :], xn)
        logits = jnp.dot(xn.astype(BF16), wr_ref[...], preferred_element_type=F32) + br_ref[...]
        cmb_ref[rows, :] = _router(logits)


def outproj(x, xn1, yssm, att, dils, batch, seq, weights, tm):
    n, dm = x.shape
    wa, ws, wo, wgate, n2, wr, br = weights
    tpb = seq // tm
    tok = lambda w: pl.BlockSpec((tm, w), lambda i: (i, 0))
    full = lambda a: pl.BlockSpec(a.shape, lambda i: (0,) * a.ndim)
    in_specs = [tok(dm), tok(dm), tok(yssm.shape[1])]
    args = [x, xn1, yssm]
    scratch = []
    if dils:
        for (num, st), d in zip(att, dils):
            for arr in (num, st):
                in_specs.append(pl.BlockSpec((None, d, tm // d, arr.shape[-1]),
                                             lambda i: (i // tpb, 0, i % tpb, 0)))
                args.append(arr)
                scratch.append(pltpu.VMEM((tm, LANES) if arr.shape[-1] == LANES
                                          else (arr.shape[-1] // LANES, tm, LANES), F32))
    else:
        in_specs.append(tok(att.shape[1]))
        args.append(att)
    in_specs += [full(wa), full(ws), full(wo), full(wgate), full(n2), full(wr), full(br)]
    args += [wa, ws, wo, wgate, n2, wr, br]
    return pl.pallas_call(
        functools.partial(_outproj_kernel, dils=tuple(dils), tm=tm), grid=(n // tm,),
        in_specs=in_specs,
        out_specs=[tok(dm), pl.BlockSpec((tm * (dm // LANES), LANES), lambda i: (i, 0)), tok(LANES)],
        out_shape=[jax.ShapeDtypeStruct((n, dm), F32), jax.ShapeDtypeStruct((n * (dm // LANES), LANES), F32),
                   jax.ShapeDtypeStruct((n, LANES), F32)],
        scratch_shapes=scratch,
        compiler_params=_cparams(1), name="outproj")(*args)


MOE_TILE = 256


def moe_plan(e1, e2, n_exp, tile):
    n = e1.shape[0]
    flat = jnp.stack([e1, e2], axis=1).reshape(-1)
    onehot = (flat[:, None] == jnp.arange(n_exp, dtype=jnp.int32)[None, :]).astype(jnp.int32)
    blk = min(tile, 2 * n)
    oh3 = onehot.reshape(-1, blk, n_exp)
    lower = (jnp.arange(blk)[:, None] > jnp.arange(blk)[None, :]).astype(F32)
    within = jnp.einsum("ij,tjk->tik", lower, oh3.astype(F32)).astype(jnp.int32)
    sums = jnp.sum(oh3, axis=1)
    before = jnp.cumsum(sums, axis=0) - sums
    rank = jnp.sum((within + before[:, None, :]) * oh3, axis=2).reshape(-1)
    counts = jnp.sum(onehot, axis=0)
    padded = (counts + tile - 1) // tile * tile
    ends = jnp.cumsum(padded)
    slot = (ends - padded)[flat] + rank
    n_tiles = (2 * n) // tile + n_exp
    tile_start = jnp.arange(n_tiles, dtype=jnp.int32) * tile
    tile_expert = jnp.minimum(jnp.searchsorted(ends, tile_start, side="right"), n_exp - 1).astype(jnp.int32)
    _, order = lax.sort((flat, jnp.arange(2 * n, dtype=jnp.int32)), num_keys=1, is_stable=True)
    s_id = jnp.arange(n_tiles * tile, dtype=jnp.int32)
    e_s = jnp.repeat(tile_expert, tile)
    k = s_id - (ends - padded)[e_s]
    src = jnp.clip((jnp.cumsum(counts) - counts)[e_s] + k, 0, 2 * n - 1)
    row_token = jnp.where(k < counts[e_s], order[src] // 2, s_id % n)
    tile_live = (tile_start < ends[n_exp - 1]).astype(jnp.int32)
    return slot.reshape(n, 2), row_token.reshape(n_tiles, 1, tile), tile_expert, tile_live


def _to_rows(ref, val):
    n, ch = val.shape[0], val.shape[1] // LANES
    for c in range(ch):
        ref[pl.ds(c, n, stride=ch), :] = val[:, c * LANES:(c + 1) * LANES]


def _from_rows(ref, start, n, ch):
    return [ref[pl.ds(start * ch + c, n, stride=ch), :] for c in range(ch)]


def _issue_rows(idx_ref, src_hbm, dst, sem, ch, unroll):
    n_idx = dst.shape[0] // ch

    def row_copy(j):
        src = src_hbm.at[pl.ds(pl.multiple_of(idx_ref[0, j] * ch, ch), ch), :]
        return pltpu.make_async_copy(src, dst.at[pl.ds(j * ch, ch), :], sem)

    if unroll:
        for j in range(n_idx):
            row_copy(j).start()
    else:
        def body(j, c):
            row_copy(j).start()
            return c
        lax.fori_loop(0, n_idx, body, 0)


def _wait_rows(src_hbm, dst, sem):
    pltpu.make_async_copy(src_hbm.at[pl.ds(0, dst.shape[0]), :], dst, sem).wait()


def _gather_rows(idx_cur, idx_next, src_hbm, buf, sem, ch):
    i = pl.program_id(0)
    n_idx = buf.shape[1] // ch
    slot = i % 2

    def row_copy(idx_ref, j, s):
        src = src_hbm.at[pl.ds(pl.multiple_of(idx_ref[0, j] * ch, ch), ch), :]
        return pltpu.make_async_copy(src, buf.at[s, pl.ds(j * ch, ch), :], sem.at[s])

    def wait_rows(s):
        pltpu.make_async_copy(src_hbm.at[pl.ds(0, n_idx * ch), :], buf.at[s], sem.at[s]).wait()

    @pl.when(i == 0)
    def _():
        def body(j, c):
            row_copy(idx_cur, j, 0).start()
            return c
        lax.fori_loop(0, n_idx, body, 0)

    wait_rows(slot)
    for j in range(n_idx):
        row_copy(idx_next, j, 1 - slot).start()
    return slot, lambda: wait_rows(1 - slot)


def _moe_group_kernel(te_ref, live_ref, tokc_ref, tokn_ref, x_hbm, wg_ref, wu_ref, wd_ref, o_ref, buf, sem):
    del te_ref
    i = pl.program_id(0)
    ch = wg_ref.shape[0] // LANES
    tile = buf.shape[1] // ch
    slot, drain = _gather_rows(tokc_ref, tokn_ref, x_hbm, buf, sem, ch)

    @pl.when(live_ref[i] > 0)
    def _():
        x = jnp.concatenate(_from_rows(buf.at[slot], 0, tile, ch), axis=1).astype(BF16)
        he = _silu(jnp.dot(x, wg_ref[...], preferred_element_type=F32)) * jnp.dot(
            x, wu_ref[...], preferred_element_type=F32)
        _to_rows(o_ref, jnp.dot(he.astype(BF16), wd_ref[...], preferred_element_type=F32))

    @pl.when(live_ref[i] == 0)
    def _():
        o_ref[...] = jnp.zeros(o_ref.shape, F32)

    pl.when(i == pl.num_programs(0) - 1)(drain)


def moe_grouped(x_rows, row_token, tile_expert, tile_live, wg, wu, wd):
    n_tiles, _, tile = row_token.shape
    dm, ff = wg.shape[1], wg.shape[2]
    ch = dm // LANES
    tok = lambda f: pl.BlockSpec((None, 1, tile), f, memory_space=pltpu.SMEM)
    grid_spec = pltpu.PrefetchScalarGridSpec(
        num_scalar_prefetch=2, grid=(n_tiles,),
        in_specs=[tok(lambda i, te, tl: (i, 0, 0)),
                  tok(lambda i, te, tl: (jnp.minimum(i + 1, n_tiles - 1), 0, 0)),
                  pl.BlockSpec(memory_space=pl.ANY),
                  pl.BlockSpec((None, dm, ff), lambda i, te, tl: (te[i], 0, 0)),
                  pl.BlockSpec((None, dm, ff), lambda i, te, tl: (te[i], 0, 0)),
                  pl.BlockSpec((None, ff, dm), lambda i, te, tl: (te[i], 0, 0))],
        out_specs=pl.BlockSpec((tile * ch, LANES), lambda i, te, tl: (i, 0)),
        scratch_shapes=[pltpu.VMEM((2, tile * ch, LANES), F32), pltpu.SemaphoreType.DMA((2,))])
    return pl.pallas_call(
        _moe_group_kernel, grid_spec=grid_spec,
        out_shape=jax.ShapeDtypeStruct((n_tiles * tile * ch, LANES), F32),
        compiler_params=_cparams(1), name="moe_grouped")(
            tile_expert, tile_live, row_token, row_token, x_rows, wg, wu, wd)


def _moe_combine_kernel(s0_ref, s1_ref, s2_ref, ys_hbm, x2_ref, rt_ref, nf_ref, o_ref, buf, sem, *, final_norm):
    tm, dm = x2_ref.shape
    ch = dm // LANES
    i = pl.program_id(0)
    depth = buf.shape[0]
    slot = i % depth

    @pl.when(i == 0)
    def _():
        _issue_rows(s0_ref, ys_hbm, buf.at[0], sem.at[0], ch, unroll=False)
        _issue_rows(s1_ref, ys_hbm, buf.at[1], sem.at[1], ch, unroll=False)

    _wait_rows(ys_hbm, buf.at[slot], sem.at[slot])
    ahead = (i + 2) % depth
    _issue_rows(s2_ref, ys_hbm, buf.at[ahead], sem.at[ahead], ch, unroll=True)
    y1 = _from_rows(buf.at[slot], 0, tm, ch)
    y2 = _from_rows(buf.at[slot], tm, tm, ch)
    rt = rt_ref[...]
    w1, w2 = rt[:, 2:3], rt[:, 3:4]
    xf = jnp.concatenate([x2_ref[:, c * LANES:(c + 1) * LANES] + (w1 * y1[c] + w2 * y2[c]) for c in range(ch)],
                         axis=1)
    if final_norm:
        xf = xf * lax.rsqrt(jnp.mean(xf * xf, axis=-1, keepdims=True) + RMS_EPS) * nf_ref[...]
    o_ref[...] = xf

    @pl.when(i == pl.num_programs(0) - 1)
    def _():
        for s in ((i + 1) % depth, ahead):
            _wait_rows(ys_hbm, buf.at[s], sem.at[s])


def moe_combine(ys_rows, slots, x2, route, nf, tm, final_norm):
    n, dm = x2.shape
    nt = n // tm
    ch = dm // LANES
    srows = slots.reshape(nt, tm, 2).transpose(0, 2, 1).reshape(nt, 1, 2 * tm)
    sspec = lambda f: pl.BlockSpec((None, 1, 2 * tm), f, memory_space=pltpu.SMEM)
    return pl.pallas_call(
        functools.partial(_moe_combine_kernel, final_norm=final_norm), grid=(nt,),
        in_specs=[sspec(lambda i: (i, 0, 0)), sspec(lambda i: (jnp.minimum(i + 1, nt - 1), 0, 0)),
                  sspec(lambda i: (jnp.minimum(i + 2, nt - 1), 0, 0)),
                  pl.BlockSpec(memory_space=pl.ANY),
                  pl.BlockSpec((tm, dm), lambda i: (i, 0)),
                  pl.BlockSpec((tm, LANES), lambda i: (i, 0)),
                  pl.BlockSpec((1, dm), lambda i: (0, 0))],
        out_specs=pl.BlockSpec((tm, dm), lambda i: (i, 0)),
        out_shape=jax.ShapeDtypeStruct((n, dm), F32),
        scratch_shapes=[pltpu.VMEM((3, 2 * tm * ch, LANES), F32), pltpu.SemaphoreType.DMA((3,))],
        compiler_params=_cparams(1), name="moe_combine")(srows, srows, srows, ys_rows, x2, route, nf)


def moe_final(xn, route, x2, wg, wu, wd, nf, final_norm):
    e1 = route[:, 0].astype(jnp.int32)
    e2 = route[:, 1].astype(jnp.int32)
    slots, row_token, tile_expert, tile_live = moe_plan(e1, e2, wg.shape[0], MOE_TILE)
    ys = moe_grouped(xn, row_token, tile_expert, tile_live, wg, wu, wd)
    return moe_combine(ys, slots, x2, route, nf, min(MOE_TILE, x2.shape[0]), final_norm)


def _pad_lanes(v, width=LANES):
    return jnp.pad(v, ((0, 0), (0, width - v.shape[1])))


def _natural_rows(arr, n_rows):
    b, d, l, w = arr.shape
    tail = arr[:, :, l - n_rows // d:, :]
    return jnp.swapaxes(tail, 1, 2).reshape(b, n_rows, w)


def kernel(x_prompt, x_sample, cache_kv_w128, cache_kv_w512, cache_kv_w2048, state_ssm, state_conv,
           norm1, w_in, conv_w, conv_b, dt_bias, a_log, d_skip, ssm_norm, w_att_out, w_ssm_out, w_o,
           norm2, w_router_coarse, b_router_coarse, w_router_fine, b_router_fine,
           w_exp_gate, w_exp_up, w_exp_down, norm_f):
    bp, sp, dm = x_prompt.shape
    bs, ts, _ = x_sample.shape
    depth = w_in.shape[0]
    n_heads = dt_bias.shape[1]
    d_inner = n_heads * SSM_HEAD_DIM
    conv_dim = conv_w.shape[2]
    n_grp = len(ATT_DILATIONS)
    off_z = n_grp * 3 * ATT_WIDTH
    off_xbc = off_z + d_inner
    off_dt = off_xbc + conv_dim
    off_gate = off_dt + n_heads
    caches = (cache_kv_w128, cache_kv_w512, cache_kv_w2048)
    for g in range(n_grp):
        assert caches[g].shape[2] == ATT_WINDOWS[g] and sp % (ATT_DILATIONS[g] * ATT_BLOCK * ATT_SUB) == 0
    assert ts <= min(ATT_DILATIONS[1:]) and ts < TPAD and sp % SSM_CHUNK == 0

    tabs_p = rope_tables(jnp.arange(sp, dtype=F32))
    tabs_s = rope_tables(jnp.tile(jnp.arange(ts, dtype=F32) + PAST_LEN, bs))
    expand = (jnp.arange(LANES)[:, None] == (jnp.arange(d_inner) // SSM_HEAD_DIM)[None, :]).astype(F32)

    xp = x_prompt.reshape(bp * sp, dm)
    xs = x_sample.reshape(bs * ts, dm)
    outs = {k: [] for k in ("kvp0", "kvp1", "kvp2", "ssm_p", "conv_p", "kvs0", "kvs1", "kvs2", "ssm_s", "conv_s")}
    n_s = bs * ts
    for layer in range(depth):
        w = w_in[layer].astype(BF16)
        w_qkv = [w[:, g * 3 * ATT_WIDTH:(g + 1) * 3 * ATT_WIDTH] for g in range(n_grp)]
        w_z, w_xbc, w_gate = w[:, off_z:off_xbc], w[:, off_xbc:off_dt], w[:, off_gate:]
        w_dt = _pad_lanes(w[:, off_dt:off_gate])
        dtb = _pad_lanes(dt_bias[layer][None])
        alog = _pad_lanes(a_log[layer][None])
        dsk_e = jnp.repeat(d_skip[layer], SSM_HEAD_DIM)[None]
        nw = ssm_norm[layer][None]
        cw, cb = conv_w[layer], conv_b[layer][None]
        w_router = _pad_lanes(jnp.concatenate([w_router_coarse[layer], w_router_fine[layer]], axis=1)).astype(BF16)
        b_router = _pad_lanes(jnp.concatenate([b_router_coarse[layer], b_router_fine[layer]])[None])
        wts = (w_att_out[layer].astype(BF16), w_ssm_out[layer].astype(BF16), w_o[layer].astype(BF16),
               w_gate, norm2[layer][None], w_router, b_router)
        wg, wu, wd = (w_exp_gate[layer].astype(BF16), w_exp_up[layer].astype(BF16),
                      w_exp_down[layer].astype(BF16))

        xn = rmsnorm_bf16(xp, norm1[layer], 1024)
        att = []
        for g, d in enumerate(ATT_DILATIONS):
            qkv = qkv_proj(xn, w_qkv[g], tabs_p, bp, sp, d, 1024, BF16, f"qkv_prompt_d{d}")
            att.append(attn_prompt(qkv))
            wnd = min(ATT_WINDOWS[g], sp)
            kv = jnp.stack([_natural_rows(qkv[1], wnd), _natural_rows(qkv[2], wnd)], axis=2)
            outs[f"kvp{g}"].append(kv.astype(F32).reshape(bp, wnd, 2, ATT_HEADS, ATT_HEAD_DIM))
        z = matmul(xn, w_z, BF16, 2048, 1024, "proj_z")
        xbc = matmul(xn, w_xbc, BF16, 2048, 1024, "proj_xbc")
        dt_raw = matmul(xn, w_dt, F32, 2048, LANES, "proj_dt")
        y_ssm, h_p = ssd_prompt(xbc, z, dt_raw, cw, cb, dtb, alog, dsk_e, nw, bp, sp)
        outs["ssm_p"].append(h_p.reshape(bp, n_heads, SSM_HEAD_DIM, SSM_STATE))
        outs["conv_p"].append(xbc.reshape(bp, sp, conv_dim)[:, sp - (SSM_CONV - 1):].astype(F32))
        x2, xn2, cmb = outproj(xp, xn, y_ssm, att, ATT_DILATIONS, bp, sp, wts, 512)
        xp = moe_final(xn2, cmb, x2, wg, wu, wd, norm_f[None], layer == depth - 1)

        xn = rmsnorm_bf16(xs, norm1[layer], n_s)
        pad_t = lambda a: jnp.pad(a.reshape(bs, ts, a.shape[-1]), ((0, 0), (0, TPAD - ts), (0, 0)))
        new = jnp.stack([qkv_proj(xn, w_qkv[g], tabs_s, 1, n_s, 1, n_s, F32, f"qkv_sample_{g}")
                         .reshape(3, bs, ts, ATT_WIDTH) for g in range(n_grp)])
        new = jnp.pad(new, ((0, 0), (0, 0), (0, 0), (0, TPAD - ts), (0, 0)))
        cl = [jnp.transpose(caches[g][layer], (0, 2, 3, 4, 1)) for g in range(n_grp)]
        res = cache_attn(new, cl, ts)
        attn_s = res[0][:, :ts]
        for g in range(n_grp):
            outs[f"kvs{g}"].append(jnp.transpose(res[1 + g], (0, 4, 1, 2, 3)))
        z = matmul(xn, w_z, F32, n_s, 512, "proj_z_s")
        xbc = matmul(xn, w_xbc, F32, n_s, 512, "proj_xbc_s")
        dt_raw = matmul(xn, w_dt, F32, n_s, LANES, "proj_dt_s")
        y_s, h_s = ssd_sample(pad_t(xbc), pad_t(z), pad_t(dt_raw), state_conv[layer],
                              state_ssm[layer].reshape(bs, d_inner, SSM_STATE), cw, cb, dtb, alog, dsk_e, nw,
                              expand, ts)
        outs["ssm_s"].append(h_s.reshape(bs, n_heads, SSM_HEAD_DIM, SSM_STATE))
        hist = jnp.concatenate([state_conv[layer], xbc.reshape(bs, ts, conv_dim)], axis=1)
        outs["conv_s"].append(hist[:, ts:])
        y_s = y_s[:, :ts].reshape(n_s, d_inner)
        x2, xn2, cmb = outproj(xs, xn, y_s, attn_s.reshape(n_s, ATT_WIDTH), (), 1, n_s, wts, min(512, n_s))
        xs = moe_final(xn2, cmb, x2, wg, wu, wd, norm_f[None], layer == depth - 1)

    st = lambda k: jnp.stack(outs[k])
    return (xp.reshape(bp, sp, dm), xs.reshape(bs, ts, dm),
            st("kvp0"), st("kvp1"), st("kvp2"), st("ssm_p"), st("conv_p"),
            st("kvs0"), st("kvs1"), st("kvs2"), st("ssm_s"), st("conv_s"))
```

```python
import functools
import math

import jax
import jax.numpy as jnp
from jax import lax
from jax.experimental import pallas as pl
from jax.experimental.pallas import tpu as pltpu

F32 = jnp.float32
BF16 = jnp.bfloat16

PAST_LEN = 8192
ATT_WINDOWS = (128, 512, 2048)
ATT_DILATIONS = (1, 4, 16)
ATT_HEADS = 8
ATT_HEAD_DIM = 64
ATT_WIDTH = ATT_HEADS * ATT_HEAD_DIM
ATT_SCALE = ATT_HEAD_DIM ** -0.5
ROT_DIM = ATT_HEAD_DIM // 4
ROPE_THETA = 500000.0
ATT_BLOCK = 128

SSM_HEAD_DIM = 64
SSM_STATE = 128
SSM_GROUPS = 4
SSM_CONV = 4
SSM_CHUNK = 128
MOE_GROUPS = 4
MOE_PER_GROUP = 4
MOE_EXPERTS = MOE_GROUPS * MOE_PER_GROUP
RMS_EPS = 1e-6
SSM_NORM_EPS = 1e-5

LANES = 128
VMEM_LIMIT = 56 * 1024 * 1024
NEG_INF = float("-inf")


def _cparams(n_axes):
    return pltpu.CompilerParams(dimension_semantics=("arbitrary",) * n_axes,
                                vmem_limit_bytes=VMEM_LIMIT)


def _sigmoid(x):
    return 1.0 / (1.0 + jnp.exp(-x))


def _silu(x):
    return x * _sigmoid(x)


def _norm_kernel(x_ref, g_ref, o_ref):
    x = x_ref[...]
    y = x * lax.rsqrt(jnp.mean(x * x, axis=-1, keepdims=True) + RMS_EPS)
    o_ref[...] = (y * g_ref[...]).astype(o_ref.dtype)


def rmsnorm_bf16(x, g, tm):
    n, d = x.shape
    return pl.pallas_call(
        _norm_kernel, grid=(n // tm,),
        in_specs=[pl.BlockSpec((tm, d), lambda i: (i, 0)), pl.BlockSpec((1, d), lambda i: (0, 0))],
        out_specs=pl.BlockSpec((tm, d), lambda i: (i, 0)),
        out_shape=jax.ShapeDtypeStruct((n, d), BF16),
        compiler_params=_cparams(1), name="rmsnorm")(x, g.reshape(1, d))


def _mm_kernel(x_ref, w_ref, o_ref):
    o_ref[...] = jnp.dot(x_ref[...], w_ref[...], preferred_element_type=F32).astype(o_ref.dtype)


def matmul(x, w, out_dtype, tm, tn, name):
    m, k = x.shape
    n = w.shape[1]
    return pl.pallas_call(
        _mm_kernel, grid=(m // tm, n // tn),
        in_specs=[pl.BlockSpec((tm, k), lambda i, j: (i, 0)), pl.BlockSpec((k, tn), lambda i, j: (0, j))],
        out_specs=pl.BlockSpec((tm, tn), lambda i, j: (i, j)),
        out_shape=jax.ShapeDtypeStruct((m, n), out_dtype),
        compiler_params=_cparams(2), name=name)(x, w)


QKV_CHUNK = 256


def _qkv_kernel(x_ref, w_ref, cos_ref, sa_ref, sb_ref, o_ref, acc_ref, *, d):
    sc = jnp.where(pl.program_id(1) == 0, ATT_SCALE, 1.0).astype(F32)
    tm = x_ref.shape[0]
    ck = min(QKV_CHUNK, tm)
    for c0 in range(0, tm, ck):
        rows = slice(c0, c0 + ck)
        acc = jnp.dot(x_ref[rows, :], w_ref[...], preferred_element_type=F32)
        c = cos_ref[rows, :] * sc
        sa = sa_ref[rows, :] * sc
        sb = sb_ref[rows, :] * sc
        for ch in range(ATT_WIDTH // LANES):
            cs = slice(ch * LANES, (ch + 1) * LANES)
            t = acc[:, cs]
            r = t * c + pltpu.roll(t, LANES - ROT_DIM // 2, 1) * sa + pltpu.roll(t, ROT_DIM // 2, 1) * sb
            if d == 1:
                o_ref[0, rows, cs] = r.astype(o_ref.dtype)
            else:
                acc_ref[ch, rows, :] = r
                for res in range(d):
                    o_ref[res, c0 // d:(c0 + ck) // d, cs] = acc_ref[
                        ch, pl.ds(c0 + res, ck // d, stride=d), :].astype(o_ref.dtype)


def qkv_proj(xn, w, tabs, batch, seq, d, tm, out_dtype, name):
    n, dm = xn.shape
    tpb = seq // tm
    cos, sa, sb = tabs
    tab_spec = pl.BlockSpec((None, tm, LANES), lambda i, j: (jnp.where(j == 2, 1, 0), i % tpb, 0))
    return pl.pallas_call(
        functools.partial(_qkv_kernel, d=d), grid=(n // tm, 3),
        in_specs=[pl.BlockSpec((tm, dm), lambda i, j: (i, 0)),
                  pl.BlockSpec((dm, ATT_WIDTH), lambda i, j: (0, j)),
                  tab_spec, tab_spec, tab_spec],
        out_specs=pl.BlockSpec((None, None, d, tm // d, ATT_WIDTH),
                               lambda i, j: (j, i // tpb, 0, i % tpb, 0)),
        out_shape=jax.ShapeDtypeStruct((3, batch, d, seq // d, ATT_WIDTH), out_dtype),
        scratch_shapes=[pltpu.VMEM((ATT_WIDTH // LANES, tm, LANES), F32)],
        compiler_params=_cparams(2), name=name)(xn, w, cos, sa, sb)


def rope_tables(pos):
    half = ROT_DIM // 2
    inv = ROPE_THETA ** (-jnp.arange(half, dtype=F32) / half)
    ang = pos[:, None] * inv[None, :]
    cos = jnp.cos(ang)
    sin = jnp.sin(ang)
    s = pos.shape[0]
    ones = jnp.ones((s, ATT_HEAD_DIM - ROT_DIM), F32)
    zeros = jnp.zeros((s, ATT_HEAD_DIM - ROT_DIM), F32)
    zh = jnp.zeros((s, half), F32)
    c = jnp.concatenate([cos, cos, ones], axis=1)
    sa = jnp.concatenate([-sin, zh, zeros], axis=1)
    sb = jnp.concatenate([zh, sin, zeros], axis=1)
    rep = LANES // ATT_HEAD_DIM
    c, sa, sb = (jnp.tile(t, (1, rep)) for t in (c, sa, sb))
    return (jnp.stack([c, jnp.ones_like(c)]), jnp.stack([sa, jnp.zeros_like(sa)]),
            jnp.stack([sb, jnp.zeros_like(sb)]))


ATT_ROW_CHUNK = 32
ATT_SUB = 4


def _attn_kernel(q_ref, kc_ref, vc_ref, kp_ref, vp_ref, num_ref, st_ref, s_scr, p_scr):
    nb = pl.program_id(2)
    blk = ATT_BLOCK
    nt = (((1,), (1,)), ((), ()))
    heads = [slice(h * ATT_HEAD_DIM, (h + 1) * ATT_HEAD_DIM) for h in range(ATT_HEADS)]
    subs = [slice(sb * blk, (sb + 1) * blk) for sb in range(ATT_SUB)]

    def prev(ref_p, ref_c, sb, sl):
        return ref_p[:, sl] if sb == 0 else ref_c[subs[sb - 1], sl]

    for sb, rows in enumerate(subs):
        for h, sl in enumerate(heads):
            q = q_ref[rows, sl]
            s_scr[sb, h, :, 0:blk] = lax.dot_general(q, prev(kp_ref, kc_ref, sb, sl), nt,
                                                     preferred_element_type=F32)
            s_scr[sb, h, :, blk:2 * blk] = lax.dot_general(q, kc_ref[rows, sl], nt, preferred_element_type=F32)
    rc = ATT_ROW_CHUNK
    qi = lax.broadcasted_iota(jnp.int32, (rc, 2 * blk), 0)
    kj = lax.broadcasted_iota(jnp.int32, (rc, 2 * blk), 1)
    lane = lax.broadcasted_iota(jnp.int32, (rc, LANES), 1)
    for sb, rows in enumerate(subs):
        for r0 in range(0, blk, rc):
            qa = qi + r0
            in_prev = jnp.logical_and(kj < blk, kj >= qa)
            if sb == 0:
                in_prev = jnp.logical_and(in_prev, nb > 0)
            mask = jnp.logical_or(in_prev, jnp.logical_and(kj >= blk, kj - blk <= qa))
            st = jnp.zeros((rc, LANES), F32)
            for h in range(ATT_HEADS):
                s = jnp.where(mask, s_scr[sb, h, r0:r0 + rc, :], NEG_INF)
                m = jnp.max(s, axis=-1, keepdims=True)
                p = jnp.exp(s - m)
                l = jnp.sum(p, axis=-1, keepdims=True)
                p_scr[sb, h, r0:r0 + rc, :] = p.astype(BF16)
                st = jnp.where(lane == h, m, st)
                st = jnp.where(lane == ATT_HEADS + h, l, st)
            st_ref[sb * blk + r0:sb * blk + r0 + rc, :] = st
    for sb, rows in enumerate(subs):
        for h, sl in enumerate(heads):
            num_ref[rows, sl] = (
                jnp.dot(p_scr[sb, h, :, 0:blk], prev(vp_ref, vc_ref, sb, sl), preferred_element_type=F32)
                + jnp.dot(p_scr[sb, h, :, blk:2 * blk], vc_ref[rows, sl], preferred_element_type=F32))


def attn_prompt(qkv):
    _, b, d, l, w = qkv.shape
    step = ATT_SUB * ATT_BLOCK
    nstep = l // step

    def spec(kind, prev):
        if prev:
            return pl.BlockSpec((None, None, None, ATT_BLOCK, w),
                                lambda bi, r, n: (kind, bi, r, jnp.maximum(ATT_SUB * n - 1, 0), 0))
        return pl.BlockSpec((None, None, None, step, w), lambda bi, r, n: (kind, bi, r, n, 0))

    return pl.pallas_call(
        _attn_kernel, grid=(b, d, nstep),
        in_specs=[spec(0, False), spec(1, False), spec(2, False), spec(1, True), spec(2, True)],
        out_specs=[pl.BlockSpec((None, None, step, w), lambda bi, r, n: (bi, r, n, 0)),
                   pl.BlockSpec((None, None, step, LANES), lambda bi, r, n: (bi, r, n, 0))],
        out_shape=[jax.ShapeDtypeStruct((b, d, l, w), F32), jax.ShapeDtypeStruct((b, d, l, LANES), F32)],
        scratch_shapes=[pltpu.VMEM((ATT_SUB, ATT_HEADS, ATT_BLOCK, 2 * ATT_BLOCK), F32),
                        pltpu.VMEM((ATT_SUB, ATT_HEADS, ATT_BLOCK, 2 * ATT_BLOCK), BF16)],
        compiler_params=_cparams(3), name=f"attn_prompt_d{d}")(qkv, qkv, qkv, qkv, qkv)


def _col(mat, h, n_lanes=LANES):
    return jnp.broadcast_to(mat[:, h:h + 1], (mat.shape[0], n_lanes))


def _ssd_kernel(xbc_ref, z_ref, dt_ref, cw_ref, cb_ref, dtb_ref, alog_ref, dsk_ref, nw_ref,
                y_ref, hout_ref,
                cbuf, xs_s, xw_s, eac_s, cd_s, y_s, ht_s, *, d_inner, n_groups):
    c = pl.program_id(1)
    nc = pl.num_programs(1)
    q = SSM_CHUNK
    n_st = SSM_STATE
    gw = d_inner // n_groups
    conv_dim = xbc_ref.shape[1]
    top = 8

    @pl.when(c == 0)
    def _():
        cbuf[0:top, :] = jnp.zeros((top, conv_dim), F32)
        ht_s[...] = jnp.zeros(ht_s.shape, F32)

    cbuf[top:top + q, :] = xbc_ref[...].astype(F32)
    cwid = 512
    for j in range(conv_dim // cwid):
        cs = slice(j * cwid, (j + 1) * cwid)
        acc = cb_ref[:, cs] + cbuf[top - 3:top - 3 + q, cs] * cw_ref[0:1, cs]
        for i in range(1, SSM_CONV):
            acc = acc + cbuf[top - 3 + i:top - 3 + i + q, cs] * cw_ref[i:i + 1, cs]
        xs_s[:, cs] = _silu(acc)
    cbuf[top - 3:top, :] = cbuf[top + q - 3:top + q, :]

    dt = jax.nn.softplus(dt_ref[...] + dtb_ref[...])
    a_row = -jnp.exp(alog_ref[...])
    a = dt * a_row
    ri = lax.broadcasted_iota(jnp.int32, (q, q), 0)
    ci = lax.broadcasted_iota(jnp.int32, (q, q), 1)
    causal = ri >= ci
    tril = jnp.where(causal, 1.0, 0.0).astype(F32)
    acum = jnp.dot(tril, a, preferred_element_type=F32, precision=lax.Precision.HIGHEST)
    acum_t = acum.T
    lane = lax.broadcasted_iota(jnp.int32, (q, LANES), 1)
    lo = lane < SSM_HEAD_DIM

    hpg = gw // SSM_HEAD_DIM
    for g in range(n_groups):
        bc = xs_s[:, d_inner + g * n_st:d_inner + (g + 1) * n_st].astype(BF16)
        cc = xs_s[:, d_inner + n_groups * n_st + g * n_st:d_inner + n_groups * n_st + (g + 1) * n_st].astype(BF16)
        cb = lax.dot_general(cc, bc, (((1,), (1,)), ((), ())), preferred_element_type=F32)
        for jp in range(hpg // 2):
            h0 = g * hpg + 2 * jp
            ls = slice(g * gw + jp * LANES, g * gw + (jp + 1) * LANES)
            ac0 = _col(acum, h0)
            ac1 = _col(acum, h0 + 1)
            acum_e = jnp.where(lo, ac0, ac1)
            dt_e = jnp.where(lo, _col(dt, h0), _col(dt, h0 + 1))
            xdt = xs_s[:, ls] * dt_e
            acl_e = acum_e[q - 1:q, :]
            xw_s[:, ls] = (xdt * jnp.exp(acl_e - acum_e)).astype(BF16)
            eac_s[:, ls] = jnp.exp(acum_e)
            cd_s[:, ls] = jnp.exp(acl_e)
            xdt_b = xdt.astype(BF16)
            zero = jnp.zeros_like(xdt_b)
            m0 = (jnp.exp(jnp.where(causal, ac0 - acum_t[h0:h0 + 1, :], NEG_INF)) * cb).astype(BF16)
            m1 = (jnp.exp(jnp.where(causal, ac1 - acum_t[h0 + 1:h0 + 2, :], NEG_INF)) * cb).astype(BF16)
            y_s[:, ls] = (jnp.dot(m0, jnp.where(lo, xdt_b, zero), preferred_element_type=F32)
                          + jnp.dot(m1, jnp.where(lo, zero, xdt_b), preferred_element_type=F32))
        gs = slice(g * gw, (g + 1) * gw)
        h_prev = ht_s[g]
        y_off = jnp.dot(cc, h_prev.astype(BF16), preferred_element_type=F32) * eac_s[:, gs]
        y_s[:, gs] = y_s[:, gs] + y_off
        st = lax.dot_general(bc, xw_s[:, gs], (((0,), (0,)), ((), ())), preferred_element_type=F32)
        ht_s[g] = cd_s[:, gs] * h_prev + st

    for g in range(n_groups):
        gs = slice(g * gw, (g + 1) * gw)
        y = y_s[:, gs] + dsk_ref[:, gs] * xs_s[:, gs]
        yf = y * _silu(z_ref[:, gs].astype(F32))
        yf = yf * lax.rsqrt(jnp.mean(yf * yf, axis=-1, keepdims=True) + SSM_NORM_EPS)
        y_ref[:, gs] = (yf * nw_ref[:, gs]).astype(y_ref.dtype)

    @pl.when(c == nc - 1)
    def _():
        for g in range(n_groups):
            hout_ref[g * gw:(g + 1) * gw, :] = ht_s[g].T


def ssd_prompt(xbc, z, dt_raw, conv_w, conv_b, dtb, alog, dsk_e, nw, batch, seq):
    n, conv_dim = xbc.shape
    d_inner = z.shape[1]
    n_groups = SSM_GROUPS
    gw = d_inner // n_groups
    q = SSM_CHUNK
    cps = seq // q
    row = lambda w: pl.BlockSpec((1, w), lambda b, c: (0, 0))
    tok = lambda w: pl.BlockSpec((q, w), lambda b, c: (b * cps + c, 0))
    return pl.pallas_call(
        functools.partial(_ssd_kernel, d_inner=d_inner, n_groups=n_groups),
        grid=(batch, cps),
        in_specs=[tok(conv_dim), tok(d_inner), tok(LANES),
                  pl.BlockSpec((SSM_CONV, conv_dim), lambda b, c: (0, 0)), row(conv_dim),
                  row(LANES), row(LANES), row(d_inner), row(d_inner)],
        out_specs=[tok(d_inner), pl.BlockSpec((None, d_inner, SSM_STATE), lambda b, c: (b, 0, 0))],
        out_shape=[jax.ShapeDtypeStruct((n, d_inner), BF16),
                   jax.ShapeDtypeStruct((batch, d_inner, SSM_STATE), F32)],
        scratch_shapes=[pltpu.VMEM((8 + q, conv_dim), F32),
                        pltpu.VMEM((q, conv_dim), F32),
                        pltpu.VMEM((q, d_inner), BF16),
                        pltpu.VMEM((q, d_inner), F32),
                        pltpu.VMEM((1, d_inner), F32),
                        pltpu.VMEM((q, d_inner), F32),
                        pltpu.VMEM((n_groups, SSM_STATE, gw), F32)],
        compiler_params=_cparams(2), name="ssd_prompt")(xbc, z, dt_raw, conv_w, conv_b, dtb, alog, dsk_e, nw)


TPAD = 8


def _ssd_step_kernel(xbc_ref, z_ref, dt_ref, cst_ref, h0_ref, cw_ref, cb_ref, dtb_ref, alog_ref,
                     dsk_ref, nw_ref, exp_ref, y_ref, hout_ref,
                     cbuf, xs_s, f_s, *, d_inner, n_groups, n_tok):
    b = pl.program_id(0)
    n_st = SSM_STATE
    gw = d_inner // n_groups
    conv_dim = xbc_ref.shape[1]
    top = 8
    hist = SSM_CONV - 1

    @pl.when(b == 0)
    def _():
        cbuf[0:top, :] = jnp.zeros((top, conv_dim), F32)

    cbuf[top - hist:top, :] = cst_ref[...]
    cbuf[top:top + TPAD, :] = xbc_ref[...]
    acc = cb_ref[...] + cbuf[top - hist:top - hist + TPAD, :] * cw_ref[0:1, :]
    for i in range(1, SSM_CONV):
        acc = acc + cbuf[top - hist + i:top - hist + i + TPAD, :] * cw_ref[i:i + 1, :]
    xs_s[...] = _silu(acc)

    rid = lax.broadcasted_iota(jnp.int32, (TPAD, LANES), 0)
    dt = jax.nn.softplus(dt_ref[...] + dtb_ref[...])
    da = dt * (-jnp.exp(alog_ref[...]))
    cum = da
    for k in range(1, n_tok):
        cum = cum + jnp.where(rid >= k, pltpu.roll(da, k, 0), 0.0)
    facs = [dt, jnp.exp(cum)]
    for t in range(n_tok):
        facs.append(jnp.where(rid <= t, jnp.exp(cum[t:t + 1, :] - cum), 0.0))
    hi = lax.Precision.HIGHEST
    f_s[...] = jnp.dot(jnp.concatenate(facs, axis=0), exp_ref[...], preferred_element_type=F32, precision=hi)

    nt = (((1,), (1,)), ((), ()))
    row8 = lax.broadcasted_iota(jnp.int32, (TPAD, gw), 0)
    for g in range(n_groups):
        gs = slice(g * gw, (g + 1) * gw)
        xdt = xs_s[:, gs] * f_s[0:TPAD, gs]
        b_f = xs_s[:, d_inner + g * n_st:d_inner + (g + 1) * n_st]
        c_f = xs_s[:, d_inner + (n_groups + g) * n_st:d_inner + (n_groups + g + 1) * n_st]
        b16 = b_f.astype(BF16)
        c16 = c_f.astype(BF16)
        h0g = h0_ref[gs, :]
        y = f_s[TPAD:2 * TPAD, gs] * lax.dot_general(c16, h0g.astype(BF16), nt, preferred_element_type=F32)
        bc = lax.dot_general(b16, c16, nt, preferred_element_type=F32)
        for t in range(n_tok):
            term = bc[:, t:t + 1] * f_s[(2 + t) * TPAD:(3 + t) * TPAD, gs] * xdt
            y = y + jnp.where(row8 == t, jnp.sum(term, axis=0, keepdims=True), 0.0)
        yv = y + dsk_ref[:, gs] * xs_s[:, gs]
        yf = yv * _silu(z_ref[:, gs])
        yf = yf * lax.rsqrt(jnp.mean(yf * yf, axis=-1, keepdims=True) + SSM_NORM_EPS)
        y_ref[:, gs] = yf * nw_ref[:, gs]

        dend = f_s[(1 + n_tok) * TPAD:(2 + n_tok) * TPAD, gs]
        pend = f_s[TPAD + n_tok - 1:TPAD + n_tok, gs]
        lhs = jnp.where(row8 == n_tok, pend, dend * xdt)
        rhs = jnp.concatenate([jnp.where(rid < n_tok, b_f, 0.0), jnp.where(rid == n_tok, 1.0, 0.0)], axis=1)
        res = lax.dot_general(lhs, rhs, (((0,), (0,)), ((), ())), preferred_element_type=F32, precision=hi)
        hout_ref[gs, :] = res[:, n_st:] * h0g + res[:, :n_st]


def ssd_sample(xbc, z, dt_raw, conv_state, h0, conv_w, conv_b, dtb, alog, dsk_e, nw, expand, n_tok):
    bsz, _, conv_dim = xbc.shape
    d_inner = z.shape[2]
    row = lambda w: pl.BlockSpec((1, w), lambda b: (0, 0))
    tok = lambda w: pl.BlockSpec((None, TPAD, w), lambda b: (b, 0, 0))
    st = pl.BlockSpec((None, d_inner, SSM_STATE), lambda b: (b, 0, 0))
    return pl.pallas_call(
        functools.partial(_ssd_step_kernel, d_inner=d_inner, n_groups=SSM_GROUPS, n_tok=n_tok),
        grid=(bsz,),
        in_specs=[tok(conv_dim), tok(d_inner), tok(LANES),
                  pl.BlockSpec((None, SSM_CONV - 1, conv_dim), lambda b: (b, 0, 0)), st,
                  pl.BlockSpec((SSM_CONV, conv_dim), lambda b: (0, 0)), row(conv_dim),
                  row(LANES), row(LANES), row(d_inner), row(d_inner),
                  pl.BlockSpec((LANES, d_inner), lambda b: (0, 0))],
        out_specs=[tok(d_inner), st],
        out_shape=[jax.ShapeDtypeStruct((bsz, TPAD, d_inner), F32),
                   jax.ShapeDtypeStruct((bsz, d_inner, SSM_STATE), F32)],
        scratch_shapes=[pltpu.VMEM((8 + TPAD, conv_dim), F32),
                        pltpu.VMEM((TPAD, conv_dim), F32),
                        pltpu.VMEM(((2 + n_tok) * TPAD, d_inner), F32)],
        compiler_params=_cparams(1), name="ssd_sample")(
            xbc, z, dt_raw, conv_state, h0, conv_w, conv_b, dtb, alog, dsk_e, nw, expand)


HEADS_PER_STEP = 4


def _cache_attn_kernel(new_ref, c0_ref, c1_ref, c2_ref, o_ref, n0_ref, n1_ref, n2_ref, stage, tts, *, n_tok):
    first = jnp.logical_and(pl.program_id(0) == 0, pl.program_id(1) == 0)

    @pl.when(first)
    def _():
        stage[...] = jnp.zeros(stage.shape, F32)

    crefs = (c0_ref, c1_ref, c2_ref)
    orefs = (n0_ref, n1_ref, n2_ref)
    hd = ATT_HEAD_DIM
    nt = (((1,), (1,)), ((), ()))
    lane = lax.broadcasted_iota(jnp.int32, (hd, LANES), 1)
    t_new = lax.broadcasted_iota(jnp.int32, (TPAD, TPAD), 0)
    u_new = lax.broadcasted_iota(jnp.int32, (TPAD, TPAD), 1)

    for head in range(HEADS_PER_STEP):
        hs = slice(head * hd, (head + 1) * hd)
        parts = []
        for g, d in enumerate(ATT_DILATIONS):
            w = crefs[g].shape[-1]
            q = new_ref[g, 0, :, hs].astype(BF16)
            kn = new_ref[g, 1, :, hs].astype(BF16)
            vn = new_ref[g, 2, :, hs].astype(BF16)
            t_id = lax.broadcasted_iota(jnp.int32, (TPAD, w), 0)
            r_id = lax.broadcasted_iota(jnp.int32, (TPAD, w), 1)
            s = jnp.dot(q, crefs[g][0, head].astype(BF16), preferred_element_type=F32)
            s = jnp.where((r_id >= t_id) if d == 1 else ((r_id & (d - 1)) == t_id), s, NEG_INF)
            sn = lax.dot_general(q, kn, nt, preferred_element_type=F32)
            sn = jnp.where((u_new <= t_new) if d == 1 else (u_new == t_new), sn, NEG_INF)
            m = jnp.maximum(jnp.max(s, axis=1, keepdims=True), jnp.max(sn, axis=1, keepdims=True))
            p = jnp.exp(s - m)
            pn = jnp.exp(sn - m)
            l = jnp.sum(p, axis=1, keepdims=True) + jnp.sum(pn, axis=1, keepdims=True)
            num = (lax.dot_general(p.astype(BF16), crefs[g][1, head].astype(BF16), nt,
                                   preferred_element_type=F32)
                   + jnp.dot(pn.astype(BF16), vn, preferred_element_type=F32))
            parts.append((num, m, l))
        mx = jnp.maximum(jnp.maximum(parts[0][1], parts[1][1]), parts[2][1])
        wg = [jnp.exp(p_[1] - mx) for p_ in parts]
        num = wg[0] * parts[0][0]
        for g in range(1, 3):
            num = num + wg[g] * parts[g][0]
        den = wg[0] * parts[0][2] + wg[1] * parts[1][2] + wg[2] * parts[2][2]
        o_ref[:, hs] = num / den

    for pair in range(HEADS_PER_STEP // 2):
        ps = slice(pair * LANES, (pair + 1) * LANES)
        for g in range(3):
            for kv in range(2):
                stage[0:TPAD, :] = new_ref[g, 1 + kv, :, ps]
                tts[2 * g + kv] = stage[...].T
        for hh in range(2):
            head = pair * 2 + hh
            hs = slice(hh * hd, (hh + 1) * hd)
            for g in range(3):
                n_col = crefs[g].shape[-1] // LANES
                for kv in range(2):
                    cur = pltpu.roll(crefs[g][kv, head, :, 0:LANES], LANES - n_tok, 1)
                    for j in range(n_col):
                        if j + 1 < n_col:
                            nxt = pltpu.roll(crefs[g][kv, head, :, (j + 1) * LANES:(j + 2) * LANES],
                                             LANES - n_tok, 1)
                        else:
                            nxt = pltpu.roll(tts[2 * g + kv, hs, :], LANES - n_tok, 1)
                        orefs[g][kv, head, :, j * LANES:(j + 1) * LANES] = jnp.where(
                            lane < LANES - n_tok, cur, nxt)
                        cur = nxt


def cache_attn(new, caches, n_tok):
    bsz = new.shape[2]
    hps = HEADS_PER_STEP
    cspec = lambda c: pl.BlockSpec((None, 2, hps, ATT_HEAD_DIM, c.shape[-1]), lambda b, h: (b, 0, h, 0, 0))
    return pl.pallas_call(
        functools.partial(_cache_attn_kernel, n_tok=n_tok), grid=(bsz, ATT_HEADS // hps),
        in_specs=[pl.BlockSpec((3, 3, None, TPAD, hps * ATT_HEAD_DIM), lambda b, h: (0, 0, b, 0, h))]
        + [cspec(c) for c in caches],
        out_specs=[pl.BlockSpec((None, TPAD, hps * ATT_HEAD_DIM), lambda b, h: (b, 0, h))]
        + [cspec(c) for c in caches],
        out_shape=[jax.ShapeDtypeStruct((bsz, TPAD, ATT_WIDTH), F32)]
        + [jax.ShapeDtypeStruct(c.shape, c.dtype) for c in caches],
        scratch_shapes=[pltpu.VMEM((LANES, LANES), F32), pltpu.VMEM((6, LANES, LANES), F32)],
        compiler_params=_cparams(2), name="cache_attn")(new, *caches)


def _router(logits):
    lanef = lax.broadcasted_iota(jnp.int32, logits.shape, 1).astype(F32)
    big = 1e9
    lc = jnp.where(lanef < MOE_GROUPS, logits, NEG_INF)
    mc = jnp.max(lc, axis=-1, keepdims=True)
    g_sel = jnp.min(jnp.where(lc == mc, lanef, big), axis=-1, keepdims=True)
    p_sel = 1.0 / jnp.sum(jnp.exp(lc - mc), axis=-1, keepdims=True)
    base = MOE_GROUPS + MOE_PER_GROUP * g_sel
    lf = jnp.where(jnp.logical_and(lanef >= base, lanef < base + MOE_PER_GROUP), logits, NEG_INF)
    v1 = jnp.max(lf, axis=-1, keepdims=True)
    i1 = jnp.min(jnp.where(lf == v1, lanef, big), axis=-1, keepdims=True)
    lf2 = jnp.where(lanef == i1, NEG_INF, lf)
    v2 = jnp.max(lf2, axis=-1, keepdims=True)
    i2 = jnp.min(jnp.where(lf2 == v2, lanef, big), axis=-1, keepdims=True)
    e2 = jnp.exp(v2 - v1)
    den = 1.0 + e2
    w1 = (1.0 / den) * p_sel
    w2 = (e2 / den) * p_sel
    out = jnp.where(lanef == 0.0, i1 - MOE_GROUPS, 0.0)
    out = jnp.where(lanef == 1.0, i2 - MOE_GROUPS, out)
    out = jnp.where(lanef == 2.0, w1, out)
    return jnp.where(lanef == 3.0, w2, out)


OUTPROJ_CHUNK = 256


def _outproj_kernel(*refs, dils, tm):
    n_g = len(dils)
    if n_g:
        x_ref, xn1_ref, ys_ref = refs[:3]
        att_refs = refs[3:3 + 2 * n_g]
        rest = refs[3 + 2 * n_g:]
    else:
        x_ref, xn1_ref, ys_ref, attn_ref = refs[:4]
        rest = refs[4:]
    wa_ref, ws_ref, wo_ref, wg_ref, n2_ref, wr_ref, br_ref, x2_ref, xn_ref, cmb_ref = rest[:10]
    scr = rest[10:]

    dm = x_ref.shape[1]
    n_ch = ATT_WIDTH // LANES
    rc = min(OUTPROJ_CHUNK, tm)
    lane = lax.broadcasted_iota(jnp.int32, (rc, LANES), 1)
    lo = lane < ATT_HEAD_DIM
    for c0 in range(0, tm, rc):
        rows = slice(c0, c0 + rc)
        if n_g:
            nums, stats = [], []
            for g, d in enumerate(dils):
                num_ref, st_ref = att_refs[2 * g], att_refs[2 * g + 1]
                if d == 1:
                    nums.append([num_ref[0, rows, ch * LANES:(ch + 1) * LANES] for ch in range(n_ch)])
                    stats.append(st_ref[0, rows, :])
                else:
                    ns, ss = scr[2 * g], scr[2 * g + 1]
                    rr = slice(c0 // d, (c0 + rc) // d)
                    for r in range(d):
                        for ch in range(n_ch):
                            ns[ch, pl.ds(c0 + r, rc // d, stride=d), :] = num_ref[r, rr, ch * LANES:(ch + 1) * LANES]
                        ss[pl.ds(c0 + r, rc // d, stride=d), :] = st_ref[r, rr, :]
                    nums.append([ns[ch, rows, :] for ch in range(n_ch)])
                    stats.append(ss[rows, :])
            chunks = []
            for ch in range(n_ch):
                wts, dens = [], []
                for h in (2 * ch, 2 * ch + 1):
                    ms = [s[:, h:h + 1] for s in stats]
                    ls = [s[:, ATT_HEADS + h:ATT_HEADS + h + 1] for s in stats]
                    mx = ms[0]
                    for m in ms[1:]:
                        mx = jnp.maximum(mx, m)
                    w = [jnp.exp(m - mx) for m in ms]
                    den = w[0] * ls[0]
                    for g in range(1, n_g):
                        den = den + w[g] * ls[g]
                    wts.append(w)
                    dens.append(den)
                num = jnp.where(lo, wts[0][0], wts[1][0]) * nums[0][ch]
                for g in range(1, n_g):
                    num = num + jnp.where(lo, wts[0][g], wts[1][g]) * nums[g][ch]
                chunks.append((num / jnp.where(lo, dens[0], dens[1])).astype(BF16))
            attn = jnp.concatenate(chunks, axis=1)
        else:
            attn = attn_ref[rows, :].astype(BF16)

        a = jnp.dot(attn, wa_ref[...], preferred_element_type=F32)
        s = jnp.dot(ys_ref[rows, :].astype(BF16), ws_ref[...], preferred_element_type=F32)
        gates = jnp.dot(xn1_ref[rows, :], wg_ref[...], preferred_element_type=F32)
        mixed = _sigmoid(gates[:, :dm]) * a + _sigmoid(gates[:, dm:]) * s
        x2 = x_ref[rows, :] + jnp.dot(mixed.astype(BF16), wo_ref[...], preferred_element_type=F32)
        x2_ref[rows, :] = x2
        xn = x2 * lax.rsqrt(jnp.mean(x2 * x2, axis=-1, keepdims=True) + RMS_EPS) * n2_ref[...]
        _to_rows(xn_ref.at[pl.ds(c0 * (dm // LANES), rc * (dm // LANES)), :], xn)
        logits = jnp.dot(xn.astype(BF16), wr_ref[...], preferred_element_type=F32) + br_ref[...]
        cmb_ref[rows, :] = _router(logits)


def outproj(x, xn1, yssm, att, dils, batch, seq, weights, tm):
    n, dm = x.shape
    wa, ws, wo, wgate, n2, wr, br = weights
    tpb = seq // tm
    tok = lambda w: pl.BlockSpec((tm, w), lambda i: (i, 0))
    full = lambda a: pl.BlockSpec(a.shape, lambda i: (0,) * a.ndim)
    in_specs = [tok(dm), tok(dm), tok(yssm.shape[1])]
    args = [x, xn1, yssm]
    scratch = []
    if dils:
        for (num, st), d in zip(att, dils):
            for arr in (num, st):
                in_specs.append(pl.BlockSpec((None, d, tm // d, arr.shape[-1]),
                                             lambda i: (i // tpb, 0, i % tpb, 0)))
                args.append(arr)
                scratch.append(pltpu.VMEM((tm, LANES) if arr.shape[-1] == LANES
                                          else (arr.shape[-1] // LANES, tm, LANES), F32))
    else:
        in_specs.append(tok(att.shape[1]))
        args.append(att)
    in_specs += [full(wa), full(ws), full(wo), full(wgate), full(n2), full(wr), full(br)]
    args += [wa, ws, wo, wgate, n2, wr, br]
    return pl.pallas_call(
        functools.partial(_outproj_kernel, dils=tuple(dils), tm=tm), grid=(n // tm,),
        in_specs=in_specs,
        out_specs=[tok(dm), pl.BlockSpec((tm * (dm // LANES), LANES), lambda i: (i, 0)), tok(LANES)],
        out_shape=[jax.ShapeDtypeStruct((n, dm), F32), jax.ShapeDtypeStruct((n * (dm // LANES), LANES), F32),
                   jax.ShapeDtypeStruct((n, LANES), F32)],
        scratch_shapes=scratch,
        compiler_params=_cparams(1), name="outproj")(*args)


MOE_TILE = 256


def moe_plan(e1, e2, n_exp, tile):
    n = e1.shape[0]
    flat = jnp.stack([e1, e2], axis=1).reshape(-1)
    onehot = (flat[:, None] == jnp.arange(n_exp, dtype=jnp.int32)[None, :]).astype(jnp.int32)
    blk = min(tile, 2 * n)
    oh3 = onehot.reshape(-1, blk, n_exp)
    lower = (jnp.arange(blk)[:, None] > jnp.arange(blk)[None, :]).astype(F32)
    within = jnp.einsum("ij,tjk->tik", lower, oh3.astype(F32)).astype(jnp.int32)
    sums = jnp.sum(oh3, axis=1)
    before = jnp.cumsum(sums, axis=0) - sums
    rank = jnp.sum((within + before[:, None, :]) * oh3, axis=2).reshape(-1)
    counts = jnp.sum(onehot, axis=0)
    padded = (counts + tile - 1) // tile * tile
    ends = jnp.cumsum(padded)
    slot = (ends - padded)[flat] + rank
    n_tiles = (2 * n) // tile + n_exp
    tile_start = jnp.arange(n_tiles, dtype=jnp.int32) * tile
    tile_expert = jnp.minimum(jnp.searchsorted(ends, tile_start, side="right"), n_exp - 1).astype(jnp.int32)
    _, order = lax.sort((flat, jnp.arange(2 * n, dtype=jnp.int32)), num_keys=1, is_stable=True)
    s_id = jnp.arange(n_tiles * tile, dtype=jnp.int32)
    e_s = jnp.repeat(tile_expert, tile)
    k = s_id - (ends - padded)[e_s]
    src = jnp.clip((jnp.cumsum(counts) - counts)[e_s] + k, 0, 2 * n - 1)
    row_token = jnp.where(k < counts[e_s], order[src] // 2, s_id % n)
    tile_live = (tile_start < ends[n_exp - 1]).astype(jnp.int32)
    return slot.reshape(n, 2), row_token.reshape(n_tiles, 1, tile), tile_expert, tile_live


def _to_rows(ref, val):
    n, ch = val.shape[0], val.shape[1] // LANES
    for c in range(ch):
        ref[pl.ds(c, n, stride=ch), :] = val[:, c * LANES:(c + 1) * LANES]


def _from_rows(ref, start, n, ch):
    return [ref[pl.ds(start * ch + c, n, stride=ch), :] for c in range(ch)]


def _issue_rows(idx_ref, src_hbm, dst, sem, ch, unroll):
    n_idx = dst.shape[0] // ch

    def row_copy(j):
        src = src_hbm.at[pl.ds(pl.multiple_of(idx_ref[0, j] * ch, ch), ch), :]
        return pltpu.make_async_copy(src, dst.at[pl.ds(j * ch, ch), :], sem)

    if unroll:
        for j in range(n_idx):
            row_copy(j).start()
    else:
        def body(j, c):
            row_copy(j).start()
            return c
        lax.fori_loop(0, n_idx, body, 0)


def _wait_rows(src_hbm, dst, sem):
    pltpu.make_async_copy(src_hbm.at[pl.ds(0, dst.shape[0]), :], dst, sem).wait()


def _gather_rows(idx_cur, idx_next, src_hbm, buf, sem, ch):
    i = pl.program_id(0)
    n_idx = buf.shape[1] // ch
    slot = i % 2

    def row_copy(idx_ref, j, s):
        src = src_hbm.at[pl.ds(pl.multiple_of(idx_ref[0, j] * ch, ch), ch), :]
        return pltpu.make_async_copy(src, buf.at[s, pl.ds(j * ch, ch), :], sem.at[s])

    def wait_rows(s):
        pltpu.make_async_copy(src_hbm.at[pl.ds(0, n_idx * ch), :], buf.at[s], sem.at[s]).wait()

    @pl.when(i == 0)
    def _():
        def body(j, c):
            row_copy(idx_cur, j, 0).start()
            return c
        lax.fori_loop(0, n_idx, body, 0)

    wait_rows(slot)
    for j in range(n_idx):
        row_copy(idx_next, j, 1 - slot).start()
    return slot, lambda: wait_rows(1 - slot)


def _moe_group_kernel(te_ref, live_ref, tokc_ref, tokn_ref, x_hbm, wg_ref, wu_ref, wd_ref, o_ref, buf, sem):
    del te_ref
    i = pl.program_id(0)
    ch = wg_ref.shape[0] // LANES
    tile = buf.shape[1] // ch
    slot, drain = _gather_rows(tokc_ref, tokn_ref, x_hbm, buf, sem, ch)

    @pl.when(live_ref[i] > 0)
    def _():
        x = jnp.concatenate(_from_rows(buf.at[slot], 0, tile, ch), axis=1).astype(BF16)
        he = _silu(jnp.dot(x, wg_ref[...], preferred_element_type=F32)) * jnp.dot(
            x, wu_ref[...], preferred_element_type=F32)
        _to_rows(o_ref, jnp.dot(he.astype(BF16), wd_ref[...], preferred_element_type=F32))

    @pl.when(live_ref[i] == 0)
    def _():
        o_ref[...] = jnp.zeros(o_ref.shape, F32)

    pl.when(i == pl.num_programs(0) - 1)(drain)


def moe_grouped(x_rows, row_token, tile_expert, tile_live, wg, wu, wd):
    n_tiles, _, tile = row_token.shape
    dm, ff = wg.shape[1], wg.shape[2]
    ch = dm // LANES
    tok = lambda f: pl.BlockSpec((None, 1, tile), f, memory_space=pltpu.SMEM)
    grid_spec = pltpu.PrefetchScalarGridSpec(
        num_scalar_prefetch=2, grid=(n_tiles,),
        in_specs=[tok(lambda i, te, tl: (i, 0, 0)),
                  tok(lambda i, te, tl: (jnp.minimum(i + 1, n_tiles - 1), 0, 0)),
                  pl.BlockSpec(memory_space=pl.ANY),
                  pl.BlockSpec((None, dm, ff), lambda i, te, tl: (te[i], 0, 0)),
                  pl.BlockSpec((None, dm, ff), lambda i, te, tl: (te[i], 0, 0)),
                  pl.BlockSpec((None, ff, dm), lambda i, te, tl: (te[i], 0, 0))],
        out_specs=pl.BlockSpec((tile * ch, LANES), lambda i, te, tl: (i, 0)),
        scratch_shapes=[pltpu.VMEM((2, tile * ch, LANES), F32), pltpu.SemaphoreType.DMA((2,))])
    return pl.pallas_call(
        _moe_group_kernel, grid_spec=grid_spec,
        out_shape=jax.ShapeDtypeStruct((n_tiles * tile * ch, LANES), F32),
        compiler_params=_cparams(1), name="moe_grouped")(
            tile_expert, tile_live, row_token, row_token, x_rows, wg, wu, wd)


def _moe_combine_kernel(s0_ref, s1_ref, s2_ref, ys_hbm, x2_ref, rt_ref, nf_ref, o_ref, buf, sem, *, final_norm):
    tm, dm = x2_ref.shape
    ch = dm // LANES
    i = pl.program_id(0)
    depth = buf.shape[0]
    slot = i % depth

    @pl.when(i == 0)
    def _():
        _issue_rows(s0_ref, ys_hbm, buf.at[0], sem.at[0], ch, unroll=False)
        _issue_rows(s1_ref, ys_hbm, buf.at[1], sem.at[1], ch, unroll=False)

    _wait_rows(ys_hbm, buf.at[slot], sem.at[slot])
    ahead = (i + 2) % depth
    _issue_rows(s2_ref, ys_hbm, buf.at[ahead], sem.at[ahead], ch, unroll=True)
    y1 = _from_rows(buf.at[slot], 0, tm, ch)
    y2 = _from_rows(buf.at[slot], tm, tm, ch)
    rt = rt_ref[...]
    w1, w2 = rt[:, 2:3], rt[:, 3:4]
    xf = jnp.concatenate([x2_ref[:, c * LANES:(c + 1) * LANES] + (w1 * y1[c] + w2 * y2[c]) for c in range(ch)],
                         axis=1)
    if final_norm:
        xf = xf * lax.rsqrt(jnp.mean(xf * xf, axis=-1, keepdims=True) + RMS_EPS) * nf_ref[...]
    o_ref[...] = xf

    @pl.when(i == pl.num_programs(0) - 1)
    def _():
        for s in ((i + 1) % depth, ahead):
            _wait_rows(ys_hbm, buf.at[s], sem.at[s])


def moe_combine(ys_rows, slots, x2, route, nf, tm, final_norm):
    n, dm = x2.shape
    nt = n // tm
    ch = dm // LANES
    srows = slots.reshape(nt, tm, 2).transpose(0, 2, 1).reshape(nt, 1, 2 * tm)
    sspec = lambda f: pl.BlockSpec((None, 1, 2 * tm), f, memory_space=pltpu.SMEM)
    return pl.pallas_call(
        functools.partial(_moe_combine_kernel, final_norm=final_norm), grid=(nt,),
        in_specs=[sspec(lambda i: (i, 0, 0)), sspec(lambda i: (jnp.minimum(i + 1, nt - 1), 0, 0)),
                  sspec(lambda i: (jnp.minimum(i + 2, nt - 1), 0, 0)),
                  pl.BlockSpec(memory_space=pl.ANY),
                  pl.BlockSpec((tm, dm), lambda i: (i, 0)),
                  pl.BlockSpec((tm, LANES), lambda i: (i, 0)),
                  pl.BlockSpec((1, dm), lambda i: (0, 0))],
        out_specs=pl.BlockSpec((tm, dm), lambda i: (i, 0)),
        out_shape=jax.ShapeDtypeStruct((n, dm), F32),
        scratch_shapes=[pltpu.VMEM((3, 2 * tm * ch, LANES), F32), pltpu.SemaphoreType.DMA((3,))],
        compiler_params=_cparams(1), name="moe_combine")(srows, srows, srows, ys_rows, x2, route, nf)


def moe_final(xn, route, x2, wg, wu, wd, nf, final_norm):
    e1 = route[:, 0].astype(jnp.int32)
    e2 = route[:, 1].astype(jnp.int32)
    slots, row_token, tile_expert, tile_live = moe_plan(e1, e2, wg.shape[0], MOE_TILE)
    ys = moe_grouped(xn, row_token, tile_expert, tile_live, wg, wu, wd)
    return moe_combine(ys, slots, x2, route, nf, min(MOE_TILE, x2.shape[0]), final_norm)


def _pad_lanes(v, width=LANES):
    return jnp.pad(v, ((0, 0), (0, width - v.shape[1])))


def _natural_rows(arr, n_rows):
    b, d, l, w = arr.shape
    tail = arr[:, :, l - n_rows // d:, :]
    return jnp.swapaxes(tail, 1, 2).reshape(b, n_rows, w)


def kernel(x_prompt, x_sample, cache_kv_w128, cache_kv_w512, cache_kv_w2048, state_ssm, state_conv,
           norm1, w_in, conv_w, conv_b, dt_bias, a_log, d_skip, ssm_norm, w_att_out, w_ssm_out, w_o,
           norm2, w_router_coarse, b_router_coarse, w_router_fine, b_router_fine,
           w_exp_gate, w_exp_up, w_exp_down, norm_f):
    bp, sp, dm = x_prompt.shape
    bs, ts, _ = x_sample.shape
    depth = w_in.shape[0]
    n_heads = dt_bias.shape[1]
    d_inner = n_heads * SSM_HEAD_DIM
    conv_dim = conv_w.shape[2]
    n_grp = len(ATT_DILATIONS)
    off_z = n_grp * 3 * ATT_WIDTH
    off_xbc = off_z + d_inner
    off_dt = off_xbc + conv_dim
    off_gate = off_dt + n_heads
    caches = (cache_kv_w128, cache_kv_w512, cache_kv_w2048)
    for g in range(n_grp):
        assert caches[g].shape[2] == ATT_WINDOWS[g] and sp % (ATT_DILATIONS[g] * ATT_BLOCK * ATT_SUB) == 0
    assert ts <= min(ATT_DILATIONS[1:]) and ts < TPAD and sp % SSM_CHUNK == 0

    tabs_p = rope_tables(jnp.arange(sp, dtype=F32))
    tabs_s = rope_tables(jnp.tile(jnp.arange(ts, dtype=F32) + PAST_LEN, bs))
    expand = (jnp.arange(LANES)[:, None] == (jnp.arange(d_inner) // SSM_HEAD_DIM)[None, :]).astype(F32)

    xp = x_prompt.reshape(bp * sp, dm)
    xs = x_sample.reshape(bs * ts, dm)
    outs = {k: [] for k in ("kvp0", "kvp1", "kvp2", "ssm_p", "conv_p", "kvs0", "kvs1", "kvs2", "ssm_s", "conv_s")}
    n_s = bs * ts
    for layer in range(depth):
        w = w_in[layer].astype(BF16)
        w_qkv = [w[:, g * 3 * ATT_WIDTH:(g + 1) * 3 * ATT_WIDTH] for g in range(n_grp)]
        w_z, w_xbc, w_gate = w[:, off_z:off_xbc], w[:, off_xbc:off_dt], w[:, off_gate:]
        w_dt = _pad_lanes(w[:, off_dt:off_gate])
        dtb = _pad_lanes(dt_bias[layer][None])
        alog = _pad_lanes(a_log[layer][None])
        dsk_e = jnp.repeat(d_skip[layer], SSM_HEAD_DIM)[None]
        nw = ssm_norm[layer][None]
        cw, cb = conv_w[layer], conv_b[layer][None]
        w_router = _pad_lanes(jnp.concatenate([w_router_coarse[layer], w_router_fine[layer]], axis=1)).astype(BF16)
        b_router = _pad_lanes(jnp.concatenate([b_router_coarse[layer], b_router_fine[layer]])[None])
        wts = (w_att_out[layer].astype(BF16), w_ssm_out[layer].astype(BF16), w_o[layer].astype(BF16),
               w_gate, norm2[layer][None], w_router, b_router)
        wg, wu, wd = (w_exp_gate[layer].astype(BF16), w_exp_up[layer].astype(BF16),
                      w_exp_down[layer].astype(BF16))

        xn = rmsnorm_bf16(xp, norm1[layer], 1024)
        att = []
        for g, d in enumerate(ATT_DILATIONS):
            qkv = qkv_proj(xn, w_qkv[g], tabs_p, bp, sp, d, 1024, BF16, f"qkv_prompt_d{d}")
            att.append(attn_prompt(qkv))
            wnd = min(ATT_WINDOWS[g], sp)
            kv = jnp.stack([_natural_rows(qkv[1], wnd), _natural_rows(qkv[2], wnd)], axis=2)
            outs[f"kvp{g}"].append(kv.astype(F32).reshape(bp, wnd, 2, ATT_HEADS, ATT_HEAD_DIM))
        z = matmul(xn, w_z, BF16, 2048, 1024, "proj_z")
        xbc = matmul(xn, w_xbc, BF16, 2048, 1024, "proj_xbc")
        dt_raw = matmul(xn, w_dt, F32, 2048, LANES, "proj_dt")
        y_ssm, h_p = ssd_prompt(xbc, z, dt_raw, cw, cb, dtb, alog, dsk_e, nw, bp, sp)
        outs["ssm_p"].append(h_p.reshape(bp, n_heads, SSM_HEAD_DIM, SSM_STATE))
        outs["conv_p"].append(xbc.reshape(bp, sp, conv_dim)[:, sp - (SSM_CONV - 1):].astype(F32))
        x2, xn2, cmb = outproj(xp, xn, y_ssm, att, ATT_DILATIONS, bp, sp, wts, 512)
        xp = moe_final(xn2, cmb, x2, wg, wu, wd, norm_f[None], layer == depth - 1)

        xn = rmsnorm_bf16(xs, norm1[layer], n_s)
        pad_t = lambda a: jnp.pad(a.reshape(bs, ts, a.shape[-1]), ((0, 0), (0, TPAD - ts), (0, 0)))
        new = jnp.stack([qkv_proj(xn, w_qkv[g], tabs_s, 1, n_s, 1, n_s, F32, f"qkv_sample_{g}")
                         .reshape(3, bs, ts, ATT_WIDTH) for g in range(n_grp)])
        new = jnp.pad(new, ((0, 0), (0, 0), (0, 0), (0, TPAD - ts), (0, 0)))
        cl = [jnp.transpose(caches[g][layer], (0, 2, 3, 4, 1)) for g in range(n_grp)]
        res = cache_attn(new, cl, ts)
        attn_s = res[0][:, :ts]
        for g in range(n_grp):
            outs[f"kvs{g}"].append(jnp.transpose(res[1 + g], (0, 4, 1, 2, 3)))
        z = matmul(xn, w_z, F32, n_s, 512, "proj_z_s")
        xbc = matmul(xn, w_xbc, F32, n_s, 512, "proj_xbc_s")
        dt_raw = matmul(xn, w_dt, F32, n_s, LANES, "proj_dt_s")
        y_s, h_s = ssd_sample(pad_t(xbc), pad_t(z), pad_t(dt_raw), state_conv[layer],
                              state_ssm[layer].reshape(bs, d_inner, SSM_STATE), cw, cb, dtb, alog, dsk_e, nw,
                              expand, ts)
        outs["ssm_s"].append(h_s.reshape(bs, n_heads, SSM_HEAD_DIM, SSM_STATE))
        hist = jnp.concatenate([state_conv[layer], xbc.reshape(bs, ts, conv_dim)], axis=1)
        outs["conv_s"].append(hist[:, ts:])
        y_s = y_s[:, :ts].reshape(n_s, d_inner)
        x2, xn2, cmb = outproj(xs, xn, y_s, attn_s.reshape(n_s, ATT_WIDTH), (), 1, n_s, wts, min(512, n_s))
        xs = moe_final(xn2, cmb, x2, wg, wu, wd, norm_f[None], layer == depth - 1)

    st = lambda k: jnp.stack(outs[k])
    return (xp.reshape(bp, sp, dm), xs.reshape(bs, ts, dm),
            st("kvp0"), st("kvp1"), st("kvp2"), st("ssm_p"), st("conv_p"),
            st("kvs0"), st("kvs1"), st("kvs2"), st("ssm_s"), st("conv_s"))
```

```python
import functools
import math

import jax
import jax.numpy as jnp
from jax import lax
from jax.experimental import pallas as pl
from jax.experimental.pallas import tpu as pltpu

F32 = jnp.float32
BF16 = jnp.bfloat16

PAST_LEN = 8192
ATT_WINDOWS = (128, 512, 2048)
ATT_DILATIONS = (1, 4, 16)
ATT_HEADS = 8
ATT_HEAD_DIM = 64
ATT_WIDTH = ATT_HEADS * ATT_HEAD_DIM
ATT_SCALE = ATT_HEAD_DIM ** -0.5
ROT_DIM = ATT_HEAD_DIM // 4
ROPE_THETA = 500000.0
ATT_BLOCK = 128

SSM_HEAD_DIM = 64
SSM_STATE = 128
SSM_GROUPS = 4
SSM_CONV = 4
SSM_CHUNK = 128
MOE_GROUPS = 4
MOE_PER_GROUP = 4
MOE_EXPERTS = MOE_GROUPS * MOE_PER_GROUP
RMS_EPS = 1e-6
SSM_NORM_EPS = 1e-5

LANES = 128
VMEM_LIMIT = 56 * 1024 * 1024
NEG_INF = float("-inf")


def _cparams(n_axes):
    return pltpu.CompilerParams(dimension_semantics=("arbitrary",) * n_axes,
                                vmem_limit_bytes=VMEM_LIMIT)


def _sigmoid(x):
    return 1.0 / (1.0 + jnp.exp(-x))


def _silu(x):
    return x * _sigmoid(x)


def _norm_kernel(x_ref, g_ref, o_ref):
    x = x_ref[...]
    y = x * lax.rsqrt(jnp.mean(x * x, axis=-1, keepdims=True) + RMS_EPS)
    o_ref[...] = (y * g_ref[...]).astype(o_ref.dtype)


def rmsnorm_bf16(x, g, tm):
    n, d = x.shape
    return pl.pallas_call(
        _norm_kernel, grid=(n // tm,),
        in_specs=[pl.BlockSpec((tm, d), lambda i: (i, 0)), pl.BlockSpec((1, d), lambda i: (0, 0))],
        out_specs=pl.BlockSpec((tm, d), lambda i: (i, 0)),
        out_shape=jax.ShapeDtypeStruct((n, d), BF16),
        compiler_params=_cparams(1), name="rmsnorm")(x, g.reshape(1, d))


def _mm_kernel(x_ref, w_ref, o_ref):
    o_ref[...] = jnp.dot(x_ref[...], w_ref[...], preferred_element_type=F32).astype(o_ref.dtype)


def matmul(x, w, out_dtype, tm, tn, name):
    m, k = x.shape
    n = w.shape[1]
    return pl.pallas_call(
        _mm_kernel, grid=(m // tm, n // tn),
        in_specs=[pl.BlockSpec((tm, k), lambda i, j: (i, 0)), pl.BlockSpec((k, tn), lambda i, j: (0, j))],
        out_specs=pl.BlockSpec((tm, tn), lambda i, j: (i, j)),
        out_shape=jax.ShapeDtypeStruct((m, n), out_dtype),
        compiler_params=_cparams(2), name=name)(x, w)


QKV_CHUNK = 256


def _qkv_kernel(x_ref, w_ref, cos_ref, sa_ref, sb_ref, o_ref, acc_ref, *, d):
    sc = jnp.where(pl.program_id(1) == 0, ATT_SCALE, 1.0).astype(F32)
    tm = x_ref.shape[0]
    ck = min(QKV_CHUNK, tm)
    for c0 in range(0, tm, ck):
        rows = slice(c0, c0 + ck)
        acc = jnp.dot(x_ref[rows, :], w_ref[...], preferred_element_type=F32)
        c = cos_ref[rows, :] * sc
        sa = sa_ref[rows, :] * sc
        sb = sb_ref[rows, :] * sc
        for ch in range(ATT_WIDTH // LANES):
            cs = slice(ch * LANES, (ch + 1) * LANES)
            t = acc[:, cs]
            r = t * c + pltpu.roll(t, LANES - ROT_DIM // 2, 1) * sa + pltpu.roll(t, ROT_DIM // 2, 1) * sb
            if d == 1:
                o_ref[0, rows, cs] = r.astype(o_ref.dtype)
            else:
                acc_ref[ch, rows, :] = r
                for res in range(d):
                    o_ref[res, c0 // d:(c0 + ck) // d, cs] = acc_ref[
                        ch, pl.ds(c0 + res, ck // d, stride=d), :].astype(o_ref.dtype)


def qkv_proj(xn, w, tabs, batch, seq, d, tm, out_dtype, name):
    n, dm = xn.shape
    tpb = seq // tm
    cos, sa, sb = tabs
    tab_spec = pl.BlockSpec((None, tm, LANES), lambda i, j: (jnp.where(j == 2, 1, 0), i % tpb, 0))
    return pl.pallas_call(
        functools.partial(_qkv_kernel, d=d), grid=(n // tm, 3),
        in_specs=[pl.BlockSpec((tm, dm), lambda i, j: (i, 0)),
                  pl.BlockSpec((dm, ATT_WIDTH), lambda i, j: (0, j)),
                  tab_spec, tab_spec, tab_spec],
        out_specs=pl.BlockSpec((None, None, d, tm // d, ATT_WIDTH),
                               lambda i, j: (j, i // tpb, 0, i % tpb, 0)),
        out_shape=jax.ShapeDtypeStruct((3, batch, d, seq // d, ATT_WIDTH), out_dtype),
        scratch_shapes=[pltpu.VMEM((ATT_WIDTH // LANES, tm, LANES), F32)],
        compiler_params=_cparams(2), name=name)(xn, w, cos, sa, sb)


def rope_tables(pos):
    half = ROT_DIM // 2
    inv = ROPE_THETA ** (-jnp.arange(half, dtype=F32) / half)
    ang = pos[:, None] * inv[None, :]
    cos = jnp.cos(ang)
    sin = jnp.sin(ang)
    s = pos.shape[0]
    ones = jnp.ones((s, ATT_HEAD_DIM - ROT_DIM), F32)
    zeros = jnp.zeros((s, ATT_HEAD_DIM - ROT_DIM), F32)
    zh = jnp.zeros((s, half), F32)
    c = jnp.concatenate([cos, cos, ones], axis=1)
    sa = jnp.concatenate([-sin, zh, zeros], axis=1)
    sb = jnp.concatenate([zh, sin, zeros], axis=1)
    rep = LANES // ATT_HEAD_DIM
    c, sa, sb = (jnp.tile(t, (1, rep)) for t in (c, sa, sb))
    return (jnp.stack([c, jnp.ones_like(c)]), jnp.stack([sa, jnp.zeros_like(sa)]),
            jnp.stack([sb, jnp.zeros_like(sb)]))


ATT_ROW_CHUNK = 32
ATT_SUB = 2


def _attn_kernel(q_ref, kc_ref, vc_ref, kp_ref, vp_ref, num_ref, st_ref, s_scr, p_scr):
    nb = pl.program_id(2)
    blk = ATT_BLOCK
    nt = (((1,), (1,)), ((), ()))
    heads = [slice(h * ATT_HEAD_DIM, (h + 1) * ATT_HEAD_DIM) for h in range(ATT_HEADS)]
    subs = [slice(sb * blk, (sb + 1) * blk) for sb in range(ATT_SUB)]

    def prev(ref_p, ref_c, sb, sl):
        return ref_p[:, sl] if sb == 0 else ref_c[subs[sb - 1], sl]

    for sb, rows in enumerate(subs):
        for h, sl in enumerate(heads):
            q = q_ref[rows, sl]
            s_scr[sb, h, :, 0:blk] = lax.dot_general(q, prev(kp_ref, kc_ref, sb, sl), nt,
                                                     preferred_element_type=F32)
            s_scr[sb, h, :, blk:2 * blk] = lax.dot_general(q, kc_ref[rows, sl], nt, preferred_element_type=F32)
    rc = ATT_ROW_CHUNK
    qi = lax.broadcasted_iota(jnp.int32, (rc, 2 * blk), 0)
    kj = lax.broadcasted_iota(jnp.int32, (rc, 2 * blk), 1)
    lane = lax.broadcasted_iota(jnp.int32, (rc, LANES), 1)
    for sb, rows in enumerate(subs):
        for r0 in range(0, blk, rc):
            qa = qi + r0
            in_prev = jnp.logical_and(kj < blk, kj >= qa)
            if sb == 0:
                in_prev = jnp.logical_and(in_prev, nb > 0)
            mask = jnp.logical_or(in_prev, jnp.logical_and(kj >= blk, kj - blk <= qa))
            st = jnp.zeros((rc, LANES), F32)
            for h in range(ATT_HEADS):
                s = jnp.where(mask, s_scr[sb, h, r0:r0 + rc, :], NEG_INF)
                m = jnp.max(s, axis=-1, keepdims=True)
                p = jnp.exp(s - m)
                l = jnp.sum(p, axis=-1, keepdims=True)
                p_scr[sb, h, r0:r0 + rc, :] = p.astype(BF16)
                st = jnp.where(lane == h, m, st)
                st = jnp.where(lane == ATT_HEADS + h, l, st)
            st_ref[sb * blk + r0:sb * blk + r0 + rc, :] = st
    for sb, rows in enumerate(subs):
        for h, sl in enumerate(heads):
            num_ref[rows, sl] = (
                jnp.dot(p_scr[sb, h, :, 0:blk], prev(vp_ref, vc_ref, sb, sl), preferred_element_type=F32)
                + jnp.dot(p_scr[sb, h, :, blk:2 * blk], vc_ref[rows, sl], preferred_element_type=F32))


def attn_prompt(qkv):
    _, b, d, l, w = qkv.shape
    step = ATT_SUB * ATT_BLOCK
    nstep = l // step

    def spec(kind, prev):
        if prev:
            return pl.BlockSpec((None, None, None, ATT_BLOCK, w),
                                lambda bi, r, n: (kind, bi, r, jnp.maximum(ATT_SUB * n - 1, 0), 0))
        return pl.BlockSpec((None, None, None, step, w), lambda bi, r, n: (kind, bi, r, n, 0))

    return pl.pallas_call(
        _attn_kernel, grid=(b, d, nstep),
        in_specs=[spec(0, False), spec(1, False), spec(2, False), spec(1, True), spec(2, True)],
        out_specs=[pl.BlockSpec((None, None, step, w), lambda bi, r, n: (bi, r, n, 0)),
                   pl.BlockSpec((None, None, step, LANES), lambda bi, r, n: (bi, r, n, 0))],
        out_shape=[jax.ShapeDtypeStruct((b, d, l, w), F32), jax.ShapeDtypeStruct((b, d, l, LANES), F32)],
        scratch_shapes=[pltpu.VMEM((ATT_SUB, ATT_HEADS, ATT_BLOCK, 2 * ATT_BLOCK), F32),
                        pltpu.VMEM((ATT_SUB, ATT_HEADS, ATT_BLOCK, 2 * ATT_BLOCK), BF16)],
        compiler_params=_cparams(3), name=f"attn_prompt_d{d}")(qkv, qkv, qkv, qkv, qkv)


def _col(mat, h, n_lanes=LANES):
    return jnp.broadcast_to(mat[:, h:h + 1], (mat.shape[0], n_lanes))


def _ssd_kernel(xbc_ref, z_ref, dt_ref, cw_ref, cb_ref, dtb_ref, alog_ref, dsk_ref, nw_ref,
                y_ref, hout_ref,
                cbuf, xs_s, xw_s, eac_s, cd_s, y_s, ht_s, *, d_inner, n_groups):
    c = pl.program_id(1)
    nc = pl.num_programs(1)
    q = SSM_CHUNK
    n_st = SSM_STATE
    gw = d_inner // n_groups
    conv_dim = xbc_ref.shape[1]
    top = 8

    @pl.when(c == 0)
    def _():
        cbuf[0:top, :] = jnp.zeros((top, conv_dim), F32)
        ht_s[...] = jnp.zeros(ht_s.shape, F32)

    cbuf[top:top + q, :] = xbc_ref[...].astype(F32)
    cwid = 512
    for j in range(conv_dim // cwid):
        cs = slice(j * cwid, (j + 1) * cwid)
        acc = cb_ref[:, cs] + cbuf[top - 3:top - 3 + q, cs] * cw_ref[0:1, cs]
        for i in range(1, SSM_CONV):
            acc = acc + cbuf[top - 3 + i:top - 3 + i + q, cs] * cw_ref[i:i + 1, cs]
        xs_s[:, cs] = _silu(acc)
    cbuf[top - 3:top, :] = cbuf[top + q - 3:top + q, :]

    dt = jax.nn.softplus(dt_ref[...] + dtb_ref[...])
    a_row = -jnp.exp(alog_ref[...])
    a = dt * a_row
    ri = lax.broadcasted_iota(jnp.int32, (q, q), 0)
    ci = lax.broadcasted_iota(jnp.int32, (q, q), 1)
    causal = ri >= ci
    tril = jnp.where(causal, 1.0, 0.0).astype(F32)
    acum = jnp.dot(tril, a, preferred_element_type=F32, precision=lax.Precision.HIGHEST)
    acum_t = acum.T
    lane = lax.broadcasted_iota(jnp.int32, (q, LANES), 1)
    lo = lane < SSM_HEAD_DIM

    hpg = gw // SSM_HEAD_DIM
    for g in range(n_groups):
        bc = xs_s[:, d_inner + g * n_st:d_inner + (g + 1) * n_st].astype(BF16)
        cc = xs_s[:, d_inner + n_groups * n_st + g * n_st:d_inner + n_groups * n_st + (g + 1) * n_st].astype(BF16)
        cb = lax.dot_general(cc, bc, (((1,), (1,)), ((), ())), preferred_element_type=F32)
        for jp in range(hpg // 2):
            h0 = g * hpg + 2 * jp
            ls = slice(g * gw + jp * LANES, g * gw + (jp + 1) * LANES)
            ac0 = _col(acum, h0)
            ac1 = _col(acum, h0 + 1)
            acum_e = jnp.where(lo, ac0, ac1)
            dt_e = jnp.where(lo, _col(dt, h0), _col(dt, h0 + 1))
            xdt = xs_s[:, ls] * dt_e
            acl_e = acum_e[q - 1:q, :]
            xw_s[:, ls] = (xdt * jnp.exp(acl_e - acum_e)).astype(BF16)
            eac_s[:, ls] = jnp.exp(acum_e)
            cd_s[:, ls] = jnp.exp(acl_e)
            xdt_b = xdt.astype(BF16)
            zero = jnp.zeros_like(xdt_b)
            m0 = (jnp.exp(jnp.where(causal, ac0 - acum_t[h0:h0 + 1, :], NEG_INF)) * cb).astype(BF16)
            m1 = (jnp.exp(jnp.where(causal, ac1 - acum_t[h0 + 1:h0 + 2, :], NEG_INF)) * cb).astype(BF16)
            y_s[:, ls] = (jnp.dot(m0, jnp.where(lo, xdt_b, zero), preferred_element_type=F32)
                          + jnp.dot(m1, jnp.where(lo, zero, xdt_b), preferred_element_type=F32))
        gs = slice(g * gw, (g + 1) * gw)
        h_prev = ht_s[g]
        y_off = jnp.dot(cc, h_prev.astype(BF16), preferred_element_type=F32) * eac_s[:, gs]
        y_s[:, gs] = y_s[:, gs] + y_off
        st = lax.dot_general(bc, xw_s[:, gs], (((0,), (0,)), ((), ())), preferred_element_type=F32)
        ht_s[g] = cd_s[:, gs] * h_prev + st

    for g in range(n_groups):
        gs = slice(g * gw, (g + 1) * gw)
        y = y_s[:, gs] + dsk_ref[:, gs] * xs_s[:, gs]
        yf = y * _silu(z_ref[:, gs].astype(F32))
        yf = yf * lax.rsqrt(jnp.mean(yf * yf, axis=-1, keepdims=True) + SSM_NORM_EPS)
        y_ref[:, gs] = (yf * nw_ref[:, gs]).astype(y_ref.dtype)

    @pl.when(c == nc - 1)
    def _():
        for g in range(n_groups):
            hout_ref[g * gw:(g + 1) * gw, :] = ht_s[g].T


def ssd_prompt(xbc, z, dt_raw, conv_w, conv_b, dtb, alog, dsk_e, nw, batch, seq):
    n, conv_dim = xbc.shape
    d_inner = z.shape[1]
    n_groups = SSM_GROUPS
    gw = d_inner // n_groups
    q = SSM_CHUNK
    cps = seq // q
    row = lambda w: pl.BlockSpec((1, w), lambda b, c: (0, 0))
    tok = lambda w: pl.BlockSpec((q, w), lambda b, c: (b * cps + c, 0))
    return pl.pallas_call(
        functools.partial(_ssd_kernel, d_inner=d_inner, n_groups=n_groups),
        grid=(batch, cps),
        in_specs=[tok(conv_dim), tok(d_inner), tok(LANES),
                  pl.BlockSpec((SSM_CONV, conv_dim), lambda b, c: (0, 0)), row(conv_dim),
                  row(LANES), row(LANES), row(d_inner), row(d_inner)],
        out_specs=[tok(d_inner), pl.BlockSpec((None, d_inner, SSM_STATE), lambda b, c: (b, 0, 0))],
        out_shape=[jax.ShapeDtypeStruct((n, d_inner), BF16),
                   jax.ShapeDtypeStruct((batch, d_inner, SSM_STATE), F32)],
        scratch_shapes=[pltpu.VMEM((8 + q, conv_dim), F32),
                        pltpu.VMEM((q, conv_dim), F32),
                        pltpu.VMEM((q, d_inner), BF16),
                        pltpu.VMEM((q, d_inner), F32),
                        pltpu.VMEM((1, d_inner), F32),
                        pltpu.VMEM((q, d_inner), F32),
                        pltpu.VMEM((n_groups, SSM_STATE, gw), F32)],
        compiler_params=_cparams(2), name="ssd_prompt")(xbc, z, dt_raw, conv_w, conv_b, dtb, alog, dsk_e, nw)


TPAD = 8


def _ssd_step_kernel(xbc_ref, z_ref, dt_ref, cst_ref, h0_ref, cw_ref, cb_ref, dtb_ref, alog_ref,
                     dsk_ref, nw_ref, exp_ref, y_ref, hout_ref,
                     cbuf, xs_s, f_s, *, d_inner, n_groups, n_tok):
    b = pl.program_id(0)
    n_st = SSM_STATE
    gw = d_inner // n_groups
    conv_dim = xbc_ref.shape[1]
    top = 8
    hist = SSM_CONV - 1

    @pl.when(b == 0)
    def _():
        cbuf[0:top, :] = jnp.zeros((top, conv_dim), F32)

    cbuf[top - hist:top, :] = cst_ref[...]
    cbuf[top:top + TPAD, :] = xbc_ref[...]
    acc = cb_ref[...] + cbuf[top - hist:top - hist + TPAD, :] * cw_ref[0:1, :]
    for i in range(1, SSM_CONV):
        acc = acc + cbuf[top - hist + i:top - hist + i + TPAD, :] * cw_ref[i:i + 1, :]
    xs_s[...] = _silu(acc)

    rid = lax.broadcasted_iota(jnp.int32, (TPAD, LANES), 0)
    dt = jax.nn.softplus(dt_ref[...] + dtb_ref[...])
    da = dt * (-jnp.exp(alog_ref[...]))
    cum = da
    for k in range(1, n_tok):
        cum = cum + jnp.where(rid >= k, pltpu.roll(da, k, 0), 0.0)
    facs = [dt, jnp.exp(cum)]
    for t in range(n_tok):
        facs.append(jnp.where(rid <= t, jnp.exp(cum[t:t + 1, :] - cum), 0.0))
    hi = lax.Precision.HIGHEST
    f_s[...] = jnp.dot(jnp.concatenate(facs, axis=0), exp_ref[...], preferred_element_type=F32, precision=hi)

    nt = (((1,), (1,)), ((), ()))
    row8 = lax.broadcasted_iota(jnp.int32, (TPAD, gw), 0)
    for g in range(n_groups):
        gs = slice(g * gw, (g + 1) * gw)
        xdt = xs_s[:, gs] * f_s[0:TPAD, gs]
        b_f = xs_s[:, d_inner + g * n_st:d_inner + (g + 1) * n_st]
        c_f = xs_s[:, d_inner + (n_groups + g) * n_st:d_inner + (n_groups + g + 1) * n_st]
        b16 = b_f.astype(BF16)
        c16 = c_f.astype(BF16)
        h0g = h0_ref[gs, :]
        y = f_s[TPAD:2 * TPAD, gs] * lax.dot_general(c16, h0g.astype(BF16), nt, preferred_element_type=F32)
        bc = lax.dot_general(b16, c16, nt, preferred_element_type=F32)
        for t in range(n_tok):
            term = bc[:, t:t + 1] * f_s[(2 + t) * TPAD:(3 + t) * TPAD, gs] * xdt
            y = y + jnp.where(row8 == t, jnp.sum(term, axis=0, keepdims=True), 0.0)
        yv = y + dsk_ref[:, gs] * xs_s[:, gs]
        yf = yv * _silu(z_ref[:, gs])
        yf = yf * lax.rsqrt(jnp.mean(yf * yf, axis=-1, keepdims=True) + SSM_NORM_EPS)
        y_ref[:, gs] = yf * nw_ref[:, gs]

        dend = f_s[(1 + n_tok) * TPAD:(2 + n_tok) * TPAD, gs]
        pend = f_s[TPAD + n_tok - 1:TPAD + n_tok, gs]
        lhs = jnp.where(row8 == n_tok, pend, dend * xdt)
        rhs = jnp.concatenate([jnp.where(rid < n_tok, b_f, 0.0), jnp.where(rid == n_tok, 1.0, 0.0)], axis=1)
        res = lax.dot_general(lhs, rhs, (((0,), (0,)), ((), ())), preferred_element_type=F32, precision=hi)
        hout_ref[gs, :] = res[:, n_st:] * h0g + res[:, :n_st]


def ssd_sample(xbc, z, dt_raw, conv_state, h0, conv_w, conv_b, dtb, alog, dsk_e, nw, expand, n_tok):
    bsz, _, conv_dim = xbc.shape
    d_inner = z.shape[2]
    row = lambda w: pl.BlockSpec((1, w), lambda b: (0, 0))
    tok = lambda w: pl.BlockSpec((None, TPAD, w), lambda b: (b, 0, 0))
    st = pl.BlockSpec((None, d_inner, SSM_STATE), lambda b: (b, 0, 0))
    return pl.pallas_call(
        functools.partial(_ssd_step_kernel, d_inner=d_inner, n_groups=SSM_GROUPS, n_tok=n_tok),
        grid=(bsz,),
        in_specs=[tok(conv_dim), tok(d_inner), tok(LANES),
                  pl.BlockSpec((None, SSM_CONV - 1, conv_dim), lambda b: (b, 0, 0)), st,
                  pl.BlockSpec((SSM_CONV, conv_dim), lambda b: (0, 0)), row(conv_dim),
                  row(LANES), row(LANES), row(d_inner), row(d_inner),
                  pl.BlockSpec((LANES, d_inner), lambda b: (0, 0))],
        out_specs=[tok(d_inner), st],
        out_shape=[jax.ShapeDtypeStruct((bsz, TPAD, d_inner), F32),
                   jax.ShapeDtypeStruct((bsz, d_inner, SSM_STATE), F32)],
        scratch_shapes=[pltpu.VMEM((8 + TPAD, conv_dim), F32),
                        pltpu.VMEM((TPAD, conv_dim), F32),
                        pltpu.VMEM(((2 + n_tok) * TPAD, d_inner), F32)],
        compiler_params=_cparams(1), name="ssd_sample")(
            xbc, z, dt_raw, conv_state, h0, conv_w, conv_b, dtb, alog, dsk_e, nw, expand)


HEADS_PER_STEP = 4


def _cache_attn_kernel(new_ref, c0_ref, c1_ref, c2_ref, o_ref, n0_ref, n1_ref, n2_ref, stage, tts, *, n_tok):
    first = jnp.logical_and(pl.program_id(0) == 0, pl.program_id(1) == 0)

    @pl.when(first)
    def _():
        stage[...] = jnp.zeros(stage.shape, F32)

    crefs = (c0_ref, c1_ref, c2_ref)
    orefs = (n0_ref, n1_ref, n2_ref)
    hd = ATT_HEAD_DIM
    nt = (((1,), (1,)), ((), ()))
    lane = lax.broadcasted_iota(jnp.int32, (hd, LANES), 1)
    t_new = lax.broadcasted_iota(jnp.int32, (TPAD, TPAD), 0)
    u_new = lax.broadcasted_iota(jnp.int32, (TPAD, TPAD), 1)

    for head in range(HEADS_PER_STEP):
        hs = slice(head * hd, (head + 1) * hd)
        parts = []
        for g, d in enumerate(ATT_DILATIONS):
            w = crefs[g].shape[-1]
            q = new_ref[g, 0, :, hs].astype(BF16)
            kn = new_ref[g, 1, :, hs].astype(BF16)
            vn = new_ref[g, 2, :, hs].astype(BF16)
            t_id = lax.broadcasted_iota(jnp.int32, (TPAD, w), 0)
            r_id = lax.broadcasted_iota(jnp.int32, (TPAD, w), 1)
            s = jnp.dot(q, crefs[g][0, head].astype(BF16), preferred_element_type=F32)
            s = jnp.where((r_id >= t_id) if d == 1 else ((r_id & (d - 1)) == t_id), s, NEG_INF)
            sn = lax.dot_general(q, kn, nt, preferred_element_type=F32)
            sn = jnp.where((u_new <= t_new) if d == 1 else (u_new == t_new), sn, NEG_INF)
            m = jnp.maximum(jnp.max(s, axis=1, keepdims=True), jnp.max(sn, axis=1, keepdims=True))
            p = jnp.exp(s - m)
            pn = jnp.exp(sn - m)
            l = jnp.sum(p, axis=1, keepdims=True) + jnp.sum(pn, axis=1, keepdims=True)
            num = (lax.dot_general(p.astype(BF16), crefs[g][1, head].astype(BF16), nt,
                                   preferred_element_type=F32)
                   + jnp.dot(pn.astype(BF16), vn, preferred_element_type=F32))
            parts.append((num, m, l))
        mx = jnp.maximum(jnp.maximum(parts[0][1], parts[1][1]), parts[2][1])
        wg = [jnp.exp(p_[1] - mx) for p_ in parts]
        num = wg[0] * parts[0][0]
        for g in range(1, 3):
            num = num + wg[g] * parts[g][0]
        den = wg[0] * parts[0][2] + wg[1] * parts[1][2] + wg[2] * parts[2][2]
        o_ref[:, hs] = num / den

    for pair in range(HEADS_PER_STEP // 2):
        ps = slice(pair * LANES, (pair + 1) * LANES)
        for g in range(3):
            for kv in range(2):
                stage[0:TPAD, :] = new_ref[g, 1 + kv, :, ps]
                tts[2 * g + kv] = stage[...].T
        for hh in range(2):
            head = pair * 2 + hh
            hs = slice(hh * hd, (hh + 1) * hd)
            for g in range(3):
                n_col = crefs[g].shape[-1] // LANES
                for kv in range(2):
                    cur = pltpu.roll(crefs[g][kv, head, :, 0:LANES], LANES - n_tok, 1)
                    for j in range(n_col):
                        if j + 1 < n_col:
                            nxt = pltpu.roll(crefs[g][kv, head, :, (j + 1) * LANES:(j + 2) * LANES],
                                             LANES - n_tok, 1)
                        else:
                            nxt = pltpu.roll(tts[2 * g + kv, hs, :], LANES - n_tok, 1)
                        orefs[g][kv, head, :, j * LANES:(j + 1) * LANES] = jnp.where(
                            lane < LANES - n_tok, cur, nxt)
                        cur = nxt


def cache_attn(new, caches, n_tok):
    bsz = new.shape[2]
    hps = HEADS_PER_STEP
    cspec = lambda c: pl.BlockSpec((None, 2, hps, ATT_HEAD_DIM, c.shape[-1]), lambda b, h: (b, 0, h, 0, 0))
    return pl.pallas_call(
        functools.partial(_cache_attn_kernel, n_tok=n_tok), grid=(bsz, ATT_HEADS // hps),
        in_specs=[pl.BlockSpec((3, 3, None, TPAD, hps * ATT_HEAD_DIM), lambda b, h: (0, 0, b, 0, h))]
        + [cspec(c) for c in caches],
        out_specs=[pl.BlockSpec((None, TPAD, hps * ATT_HEAD_DIM), lambda b, h: (b, 0, h))]
        + [cspec(c) for c in caches],
        out_shape=[jax.ShapeDtypeStruct((bsz, TPAD, ATT_WIDTH), F32)]
        + [jax.ShapeDtypeStruct(c.shape, c.dtype) for c in caches],
        scratch_shapes=[pltpu.VMEM((LANES, LANES), F32), pltpu.VMEM((6, LANES, LANES), F32)],
        compiler_params=_cparams(2), name="cache_attn")(new, *caches)


def _router(logits):
    lanef = lax.broadcasted_iota(jnp.int32, logits.shape, 1).astype(F32)
    big = 1e9
    lc = jnp.where(lanef < MOE_GROUPS, logits, NEG_INF)
    mc = jnp.max(lc, axis=-1, keepdims=True)
    g_sel = jnp.min(jnp.where(lc == mc, lanef, big), axis=-1, keepdims=True)
    p_sel = 1.0 / jnp.sum(jnp.exp(lc - mc), axis=-1, keepdims=True)
    base = MOE_GROUPS + MOE_PER_GROUP * g_sel
    lf = jnp.where(jnp.logical_and(lanef >= base, lanef < base + MOE_PER_GROUP), logits, NEG_INF)
    v1 = jnp.max(lf, axis=-1, keepdims=True)
    i1 = jnp.min(jnp.where(lf == v1, lanef, big), axis=-1, keepdims=True)
    lf2 = jnp.where(lanef == i1, NEG_INF, lf)
    v2 = jnp.max(lf2, axis=-1, keepdims=True)
    i2 = jnp.min(jnp.where(lf2 == v2, lanef, big), axis=-1, keepdims=True)
    e2 = jnp.exp(v2 - v1)
    den = 1.0 + e2
    w1 = (1.0 / den) * p_sel
    w2 = (e2 / den) * p_sel
    out = jnp.where(lanef == 0.0, i1 - MOE_GROUPS, 0.0)
    out = jnp.where(lanef == 1.0, i2 - MOE_GROUPS, out)
    out = jnp.where(lanef == 2.0, w1, out)
    return jnp.where(lanef == 3.0, w2, out)


OUTPROJ_CHUNK = 256


def _outproj_kernel(*refs, dils, tm):
    n_g = len(dils)
    if n_g:
        x_ref, xn1_ref, ys_ref = refs[:3]
        att_refs = refs[3:3 + 2 * n_g]
        rest = refs[3 + 2 * n_g:]
    else:
        x_ref, xn1_ref, ys_ref, attn_ref = refs[:4]
        rest = refs[4:]
    wa_ref, ws_ref, wo_ref, wg_ref, n2_ref, wr_ref, br_ref, x2_ref, xn_ref, cmb_ref = rest[:10]
    scr = rest[10:]

    dm = x_ref.shape[1]
    n_ch = ATT_WIDTH // LANES
    rc = min(OUTPROJ_CHUNK, tm)
    lane = lax.broadcasted_iota(jnp.int32, (rc, LANES), 1)
    lo = lane < ATT_HEAD_DIM
    for c0 in range(0, tm, rc):
        rows = slice(c0, c0 + rc)
        if n_g:
            nums, stats = [], []
            for g, d in enumerate(dils):
                num_ref, st_ref = att_refs[2 * g], att_refs[2 * g + 1]
                if d == 1:
                    nums.append([num_ref[0, rows, ch * LANES:(ch + 1) * LANES] for ch in range(n_ch)])
                    stats.append(st_ref[0, rows, :])
                else:
                    ns, ss = scr[2 * g], scr[2 * g + 1]
                    rr = slice(c0 // d, (c0 + rc) // d)
                    for r in range(d):
                        for ch in range(n_ch):
                            ns[ch, pl.ds(c0 + r, rc // d, stride=d), :] = num_ref[r, rr, ch * LANES:(ch + 1) * LANES]
                        ss[pl.ds(c0 + r, rc // d, stride=d), :] = st_ref[r, rr, :]
                    nums.append([ns[ch, rows, :] for ch in range(n_ch)])
                    stats.append(ss[rows, :])
            chunks = []
            for ch in range(n_ch):
                wts, dens = [], []
                for h in (2 * ch, 2 * ch + 1):
                    ms = [s[:, h:h + 1] for s in stats]
                    ls = [s[:, ATT_HEADS + h:ATT_HEADS + h + 1] for s in stats]
                    mx = ms[0]
                    for m in ms[1:]:
                        mx = jnp.maximum(mx, m)
                    w = [jnp.exp(m - mx) for m in ms]
                    den = w[0] * ls[0]
                    for g in range(1, n_g):
                        den = den + w[g] * ls[g]
                    wts.append(w)
                    dens.append(den)
                num = jnp.where(lo, wts[0][0], wts[1][0]) * nums[0][ch]
                for g in range(1, n_g):
                    num = num + jnp.where(lo, wts[0][g], wts[1][g]) * nums[g][ch]
                chunks.append((num / jnp.where(lo, dens[0], dens[1])).astype(BF16))
            attn = jnp.concatenate(chunks, axis=1)
        else:
            attn = attn_ref[rows, :].astype(BF16)

        a = jnp.dot(attn, wa_ref[...], preferred_element_type=F32)
        s = jnp.dot(ys_ref[rows, :].astype(BF16), ws_ref[...], preferred_element_type=F32)
        gates = jnp.dot(xn1_ref[rows, :], wg_ref[...], preferred_element_type=F32)
        mixed = _sigmoid(gates[:, :dm]) * a + _sigmoid(gates[:, dm:]) * s
        x2 = x_ref[rows, :] + jnp.dot(mixed.astype(BF16), wo_ref[...], preferred_element_type=F32)
        x2_ref[rows, :] = x2
        xn = x2 * lax.rsqrt(jnp.mean(x2 * x2, axis=-1, keepdims=True) + RMS_EPS) * n2_ref[...]
        _to_rows(xn_ref.at[pl.ds(c0 * (dm // LANES), rc * (dm // LANES)), :], xn)
        logits = jnp.dot(xn.astype(BF16), wr_ref[...], preferred_element_type=F32) + br_ref[...]
        cmb_ref[rows, :] = _router(logits)


def outproj(x, xn1, yssm, att, dils, batch, seq, weights, tm):
    n, dm = x.shape
    wa, ws, wo, wgate, n2, wr, br = weights
    tpb = seq // tm
    tok = lambda w: pl.BlockSpec((tm, w), lambda i: (i, 0))
    full = lambda a: pl.BlockSpec(a.shape, lambda i: (0,) * a.ndim)
    in_specs = [tok(dm), tok(dm), tok(yssm.shape[1])]
    args = [x, xn1, yssm]
    scratch = []
    if dils:
        for (num, st), d in zip(att, dils):
            for arr in (num, st):
                in_specs.append(pl.BlockSpec((None, d, tm // d, arr.shape[-1]),
                                             lambda i: (i // tpb, 0, i % tpb, 0)))
                args.append(arr)
                scratch.append(pltpu.VMEM((tm, LANES) if arr.shape[-1] == LANES
                                          else (arr.shape[-1] // LANES, tm, LANES), F32))
    else:
        in_specs.append(tok(att.shape[1]))
        args.append(att)
    in_specs += [full(wa), full(ws), full(wo), full(wgate), full(n2), full(wr), full(br)]
    args += [wa, ws, wo, wgate, n2, wr, br]
    return pl.pallas_call(
        functools.partial(_outproj_kernel, dils=tuple(dils), tm=tm), grid=(n // tm,),
        in_specs=in_specs,
        out_specs=[tok(dm), pl.BlockSpec((tm * (dm // LANES), LANES), lambda i: (i, 0)), tok(LANES)],
        out_shape=[jax.ShapeDtypeStruct((n, dm), F32), jax.ShapeDtypeStruct((n * (dm // LANES), LANES), F32),
                   jax.ShapeDtypeStruct((n, LANES), F32)],
        scratch_shapes=scratch,
        compiler_params=_cparams(1), name="outproj")(*args)


MOE_TILE = 256


def moe_plan(e1, e2, n_exp, tile):
    n = e1.shape[0]
    flat = jnp.stack([e1, e2], axis=1).reshape(-1)
    onehot = (flat[:, None] == jnp.arange(n_exp, dtype=jnp.int32)[None, :]).astype(jnp.int32)
    blk = min(tile, 2 * n)
    oh3 = onehot.reshape(-1, blk, n_exp)
    lower = (jnp.arange(blk)[:, None] > jnp.arange(blk)[None, :]).astype(F32)
    within = jnp.einsum("ij,tjk->tik", lower, oh3.astype(F32)).astype(jnp.int32)
    sums = jnp.sum(oh3, axis=1)
    before = jnp.cumsum(sums, axis=0) - sums
    rank = jnp.sum((within + before[:, None, :]) * oh3, axis=2).reshape(-1)
    counts = jnp.sum(onehot, axis=0)
    padded = (counts + tile - 1) // tile * tile
    ends = jnp.cumsum(padded)
    slot = (ends - padded)[flat] + rank
    n_tiles = (2 * n) // tile + n_exp
    tile_start = jnp.arange(n_tiles, dtype=jnp.int32) * tile
    tile_expert = jnp.minimum(jnp.searchsorted(ends, tile_start, side="right"), n_exp - 1).astype(jnp.int32)
    _, order = lax.sort((flat, jnp.arange(2 * n, dtype=jnp.int32)), num_keys=1, is_stable=True)
    s_id = jnp.arange(n_tiles * tile, dtype=jnp.int32)
    e_s = jnp.repeat(tile_expert, tile)
    k = s_id - (ends - padded)[e_s]
    src = jnp.clip((jnp.cumsum(counts) - counts)[e_s] + k, 0, 2 * n - 1)
    row_token = jnp.where(k < counts[e_s], order[src] // 2, s_id % n)
    tile_live = (tile_start < ends[n_exp - 1]).astype(jnp.int32)
    return slot.reshape(n, 2), row_token.reshape(n_tiles, 1, tile), tile_expert, tile_live


def _to_rows(ref, val):
    n, ch = val.shape[0], val.shape[1] // LANES
    for c in range(ch):
        ref[pl.ds(c, n, stride=ch), :] = val[:, c * LANES:(c + 1) * LANES]


def _from_rows(ref, start, n, ch):
    return [ref[pl.ds(start * ch + c, n, stride=ch), :] for c in range(ch)]


def _issue_rows(idx_ref, src_hbm, dst, sem, ch, unroll):
    n_idx = dst.shape[0] // ch

    def row_copy(j):
        src = src_hbm.at[pl.ds(pl.multiple_of(idx_ref[0, j] * ch, ch), ch), :]
        return pltpu.make_async_copy(src, dst.at[pl.ds(j * ch, ch), :], sem)

    if unroll:
        for j in range(n_idx):
            row_copy(j).start(priority=j % 2)
    else:
        def body(j, c):
            row_copy(j).start()
            return c
        lax.fori_loop(0, n_idx, body, 0)


def _wait_rows(src_hbm, dst, sem):
    pltpu.make_async_copy(src_hbm.at[pl.ds(0, dst.shape[0]), :], dst, sem).wait()


def _gather_rows(idx_cur, idx_next, src_hbm, buf, sem, ch):
    i = pl.program_id(0)
    n_idx = buf.shape[1] // ch
    slot = i % 2

    def row_copy(idx_ref, j, s):
        src = src_hbm.at[pl.ds(pl.multiple_of(idx_ref[0, j] * ch, ch), ch), :]
        return pltpu.make_async_copy(src, buf.at[s, pl.ds(j * ch, ch), :], sem.at[s])

    def wait_rows(s):
        pltpu.make_async_copy(src_hbm.at[pl.ds(0, n_idx * ch), :], buf.at[s], sem.at[s]).wait()

    @pl.when(i == 0)
    def _():
        def body(j, c):
            row_copy(idx_cur, j, 0).start()
            return c
        lax.fori_loop(0, n_idx, body, 0)

    wait_rows(slot)
    for j in range(n_idx):
        row_copy(idx_next, j, 1 - slot).start(priority=j % 2)
    return slot, lambda: wait_rows(1 - slot)


def _moe_group_kernel(te_ref, live_ref, tokc_ref, tokn_ref, x_hbm, wg_ref, wu_ref, wd_ref, o_ref, buf, sem):
    del te_ref
    i = pl.program_id(0)
    ch = wg_ref.shape[0] // LANES
    tile = buf.shape[1] // ch
    slot, drain = _gather_rows(tokc_ref, tokn_ref, x_hbm, buf, sem, ch)

    @pl.when(live_ref[i] > 0)
    def _():
        x = jnp.concatenate(_from_rows(buf.at[slot], 0, tile, ch), axis=1).astype(BF16)
        he = _silu(jnp.dot(x, wg_ref[...], preferred_element_type=F32)) * jnp.dot(
            x, wu_ref[...], preferred_element_type=F32)
        _to_rows(o_ref, jnp.dot(he.astype(BF16), wd_ref[...], preferred_element_type=F32))

    @pl.when(live_ref[i] == 0)
    def _():
        o_ref[...] = jnp.zeros(o_ref.shape, F32)

    pl.when(i == pl.num_programs(0) - 1)(drain)


def moe_grouped(x_rows, row_token, tile_expert, tile_live, wg, wu, wd):
    n_tiles, _, tile = row_token.shape
    dm, ff = wg.shape[1], wg.shape[2]
    ch = dm // LANES
    tok = lambda f: pl.BlockSpec((None, 1, tile), f, memory_space=pltpu.SMEM)
    grid_spec = pltpu.PrefetchScalarGridSpec(
        num_scalar_prefetch=2, grid=(n_tiles,),
        in_specs=[tok(lambda i, te, tl: (i, 0, 0)),
                  tok(lambda i, te, tl: (jnp.minimum(i + 1, n_tiles - 1), 0, 0)),
                  pl.BlockSpec(memory_space=pl.ANY),
                  pl.BlockSpec((None, dm, ff), lambda i, te, tl: (te[i], 0, 0)),
                  pl.BlockSpec((None, dm, ff), lambda i, te, tl: (te[i], 0, 0)),
                  pl.BlockSpec((None, ff, dm), lambda i, te, tl: (te[i], 0, 0))],
        out_specs=pl.BlockSpec((tile * ch, LANES), lambda i, te, tl: (i, 0)),
        scratch_shapes=[pltpu.VMEM((2, tile * ch, LANES), F32), pltpu.SemaphoreType.DMA((2,))])
    return pl.pallas_call(
        _moe_group_kernel, grid_spec=grid_spec,
        out_shape=jax.ShapeDtypeStruct((n_tiles * tile * ch, LANES), F32),
        compiler_params=_cparams(1), name="moe_grouped")(
            tile_expert, tile_live, row_token, row_token, x_rows, wg, wu, wd)


def _moe_combine_kernel(s0_ref, s1_ref, s2_ref, ys_hbm, x2_ref, rt_ref, nf_ref, o_ref, buf, sem, *, final_norm):
    tm, dm = x2_ref.shape
    ch = dm // LANES
    i = pl.program_id(0)
    depth = buf.shape[0]
    slot = i % depth

    @pl.when(i == 0)
    def _():
        _issue_rows(s0_ref, ys_hbm, buf.at[0], sem.at[0], ch, unroll=False)
        _issue_rows(s1_ref, ys_hbm, buf.at[1], sem.at[1], ch, unroll=False)

    _wait_rows(ys_hbm, buf.at[slot], sem.at[slot])
    ahead = (i + 2) % depth
    _issue_rows(s2_ref, ys_hbm, buf.at[ahead], sem.at[ahead], ch, unroll=True)
    y1 = _from_rows(buf.at[slot], 0, tm, ch)
    y2 = _from_rows(buf.at[slot], tm, tm, ch)
    rt = rt_ref[...]
    w1, w2 = rt[:, 2:3], rt[:, 3:4]
    xf = jnp.concatenate([x2_ref[:, c * LANES:(c + 1) * LANES] + (w1 * y1[c] + w2 * y2[c]) for c in range(ch)],
                         axis=1)
    if final_norm:
        xf = xf * lax.rsqrt(jnp.mean(xf * xf, axis=-1, keepdims=True) + RMS_EPS) * nf_ref[...]
    o_ref[...] = xf

    @pl.when(i == pl.num_programs(0) - 1)
    def _():
        for s in ((i + 1) % depth, ahead):
            _wait_rows(ys_hbm, buf.at[s], sem.at[s])


def moe_combine(ys_rows, slots, x2, route, nf, tm, final_norm):
    n, dm = x2.shape
    nt = n // tm
    ch = dm // LANES
    srows = slots.reshape(nt, tm, 2).transpose(0, 2, 1).reshape(nt, 1, 2 * tm)
    sspec = lambda f: pl.BlockSpec((None, 1, 2 * tm), f, memory_space=pltpu.SMEM)
    return pl.pallas_call(
        functools.partial(_moe_combine_kernel, final_norm=final_norm), grid=(nt,),
        in_specs=[sspec(lambda i: (i, 0, 0)), sspec(lambda i: (jnp.minimum(i + 1, nt - 1), 0, 0)),
                  sspec(lambda i: (jnp.minimum(i + 2, nt - 1), 0, 0)),
                  pl.BlockSpec(memory_space=pl.ANY),
                  pl.BlockSpec((tm, dm), lambda i: (i, 0)),
                  pl.BlockSpec((tm, LANES), lambda i: (i, 0)),
                  pl.BlockSpec((1, dm), lambda i: (0, 0))],
        out_specs=pl.BlockSpec((tm, dm), lambda i: (i, 0)),
        out_shape=jax.ShapeDtypeStruct((n, dm), F32),
        scratch_shapes=[pltpu.VMEM((3, 2 * tm * ch, LANES), F32), pltpu.SemaphoreType.DMA((3,))],
        compiler_params=_cparams(1), name="moe_combine")(srows, srows, srows, ys_rows, x2, route, nf)


def moe_final(xn, route, x2, wg, wu, wd, nf, final_norm):
    e1 = route[:, 0].astype(jnp.int32)
    e2 = route[:, 1].astype(jnp.int32)
    slots, row_token, tile_expert, tile_live = moe_plan(e1, e2, wg.shape[0], MOE_TILE)
    ys = moe_grouped(xn, row_token, tile_expert, tile_live, wg, wu, wd)
    return moe_combine(ys, slots, x2, route, nf, min(MOE_TILE, x2.shape[0]), final_norm)


def _pad_lanes(v, width=LANES):
    return jnp.pad(v, ((0, 0), (0, width - v.shape[1])))


def _natural_rows(arr, n_rows):
    b, d, l, w = arr.shape
    tail = arr[:, :, l - n_rows // d:, :]
    return jnp.swapaxes(tail, 1, 2).reshape(b, n_rows, w)


def kernel(x_prompt, x_sample, cache_kv_w128, cache_kv_w512, cache_kv_w2048, state_ssm, state_conv,
           norm1, w_in, conv_w, conv_b, dt_bias, a_log, d_skip, ssm_norm, w_att_out, w_ssm_out, w_o,
           norm2, w_router_coarse, b_router_coarse, w_router_fine, b_router_fine,
           w_exp_gate, w_exp_up, w_exp_down, norm_f):
    bp, sp, dm = x_prompt.shape
    bs, ts, _ = x_sample.shape
    depth = w_in.shape[0]
    n_heads = dt_bias.shape[1]
    d_inner = n_heads * SSM_HEAD_DIM
    conv_dim = conv_w.shape[2]
    n_grp = len(ATT_DILATIONS)
    off_z = n_grp * 3 * ATT_WIDTH
    off_xbc = off_z + d_inner
    off_dt = off_xbc + conv_dim
    off_gate = off_dt + n_heads
    caches = (cache_kv_w128, cache_kv_w512, cache_kv_w2048)
    for g in range(n_grp):
        assert caches[g].shape[2] == ATT_WINDOWS[g] and sp % (ATT_DILATIONS[g] * ATT_BLOCK * ATT_SUB) == 0
    assert ts <= min(ATT_DILATIONS[1:]) and ts < TPAD and sp % SSM_CHUNK == 0

    tabs_p = rope_tables(jnp.arange(sp, dtype=F32))
    tabs_s = rope_tables(jnp.tile(jnp.arange(ts, dtype=F32) + PAST_LEN, bs))
    expand = (jnp.arange(LANES)[:, None] == (jnp.arange(d_inner) // SSM_HEAD_DIM)[None, :]).astype(F32)

    xp = x_prompt.reshape(bp * sp, dm)
    xs = x_sample.reshape(bs * ts, dm)
    outs = {k: [] for k in ("kvp0", "kvp1", "kvp2", "ssm_p", "conv_p", "kvs0", "kvs1", "kvs2", "ssm_s", "conv_s")}
    n_s = bs * ts
    for layer in range(depth):
        w = w_in[layer].astype(BF16)
        w_qkv = [w[:, g * 3 * ATT_WIDTH:(g + 1) * 3 * ATT_WIDTH] for g in range(n_grp)]
        w_z, w_xbc, w_gate = w[:, off_z:off_xbc], w[:, off_xbc:off_dt], w[:, off_gate:]
        w_dt = _pad_lanes(w[:, off_dt:off_gate])
        dtb = _pad_lanes(dt_bias[layer][None])
        alog = _pad_lanes(a_log[layer][None])
        dsk_e = jnp.repeat(d_skip[layer], SSM_HEAD_DIM)[None]
        nw = ssm_norm[layer][None]
        cw, cb = conv_w[layer], conv_b[layer][None]
        w_router = _pad_lanes(jnp.concatenate([w_router_coarse[layer], w_router_fine[layer]], axis=1)).astype(BF16)
        b_router = _pad_lanes(jnp.concatenate([b_router_coarse[layer], b_router_fine[layer]])[None])
        wts = (w_att_out[layer].astype(BF16), w_ssm_out[layer].astype(BF16), w_o[layer].astype(BF16),
               w_gate, norm2[layer][None], w_router, b_router)
        wg, wu, wd = (w_exp_gate[layer].astype(BF16), w_exp_up[layer].astype(BF16),
                      w_exp_down[layer].astype(BF16))

        xn = rmsnorm_bf16(xp, norm1[layer], 1024)
        att = []
        for g, d in enumerate(ATT_DILATIONS):
            qkv = qkv_proj(xn, w_qkv[g], tabs_p, bp, sp, d, 1024, BF16, f"qkv_prompt_d{d}")
            att.append(attn_prompt(qkv))
            wnd = min(ATT_WINDOWS[g], sp)
            kv = jnp.stack([_natural_rows(qkv[1], wnd), _natural_rows(qkv[2], wnd)], axis=2)
            outs[f"kvp{g}"].append(kv.astype(F32).reshape(bp, wnd, 2, ATT_HEADS, ATT_HEAD_DIM))
        z = matmul(xn, w_z, BF16, 2048, 1024, "proj_z")
        xbc = matmul(xn, w_xbc, BF16, 2048, 1024, "proj_xbc")
        dt_raw = matmul(xn, w_dt, F32, 2048, LANES, "proj_dt")
        y_ssm, h_p = ssd_prompt(xbc, z, dt_raw, cw, cb, dtb, alog, dsk_e, nw, bp, sp)
        outs["ssm_p"].append(h_p.reshape(bp, n_heads, SSM_HEAD_DIM, SSM_STATE))
        outs["conv_p"].append(xbc.reshape(bp, sp, conv_dim)[:, sp - (SSM_CONV - 1):].astype(F32))
        x2, xn2, cmb = outproj(xp, xn, y_ssm, att, ATT_DILATIONS, bp, sp, wts, 512)
        xp = moe_final(xn2, cmb, x2, wg, wu, wd, norm_f[None], layer == depth - 1)

        xn = rmsnorm_bf16(xs, norm1[layer], n_s)
        pad_t = lambda a: jnp.pad(a.reshape(bs, ts, a.shape[-1]), ((0, 0), (0, TPAD - ts), (0, 0)))
        new = jnp.stack([qkv_proj(xn, w_qkv[g], tabs_s, 1, n_s, 1, n_s, F32, f"qkv_sample_{g}")
                         .reshape(3, bs, ts, ATT_WIDTH) for g in range(n_grp)])
        new = jnp.pad(new, ((0, 0), (0, 0), (0, 0), (0, TPAD - ts), (0, 0)))
        cl = [jnp.transpose(caches[g][layer], (0, 2, 3, 4, 1)) for g in range(n_grp)]
        res = cache_attn(new, cl, ts)
        attn_s = res[0][:, :ts]
        for g in range(n_grp):
            outs[f"kvs{g}"].append(jnp.transpose(res[1 + g], (0, 4, 1, 2, 3)))
        z = matmul(xn, w_z, F32, n_s, 512, "proj_z_s")
        xbc = matmul(xn, w_xbc, F32, n_s, 512, "proj_xbc_s")
        dt_raw = matmul(xn, w_dt, F32, n_s, LANES, "proj_dt_s")
        y_s, h_s = ssd_sample(pad_t(xbc), pad_t(z), pad_t(dt_raw), state_conv[layer],
                              state_ssm[layer].reshape(bs, d_inner, SSM_STATE), cw, cb, dtb, alog, dsk_e, nw,
                              expand, ts)
        outs["ssm_s"].append(h_s.reshape(bs, n_heads, SSM_HEAD_DIM, SSM_STATE))
        hist = jnp.concatenate([state_conv[layer], xbc.reshape(bs, ts, conv_dim)], axis=1)
        outs["conv_s"].append(hist[:, ts:])
        y_s = y_s[:, :ts].reshape(n_s, d_inner)
        x2, xn2, cmb = outproj(xs, xn, y_s, attn_s.reshape(n_s, ATT_WIDTH), (), 1, n_s, wts, min(512, n_s))
        xs = moe_final(xn2, cmb, x2, wg, wu, wd, norm_f[None], layer == depth - 1)

    st = lambda k: jnp.stack(outs[k])
    return (xp.reshape(bp, sp, dm), xs.reshape(bs, ts, dm),
            st("kvp0"), st("kvp1"), st("kvp2"), st("ssm_p"), st("conv_p"),
            st("kvs0"), st("kvs1"), st("kvs2"), st("ssm_s"), st("conv_s"))
```
